```python
import jax, jax.numpy as jnp
from jax import lax
import numpy as np

D_MODEL = 2048
BATCH = 2
SEQ = 4096
DEPTH = 4
DEC_BATCH = 32
DEC_SEQ = 8
PAST_LEN = 16384
PAGE_SIZE = 128

N_HEADS = D_MODEL // 128
HEAD_DIM = 64
N_KV = N_HEADS // 4
GQA_G = N_HEADS // N_KV
WINDOW = 128
ATT_BLOCK = WINDOW
BR_W = N_HEADS * HEAD_DIM
CONV_W = BR_W
CONV_K = 31
POOL_GROUPS = 4
POOL_WINDOWS = (2, 4, 8, 16)
POOL_W = BR_W
POOL_G = POOL_W // POOL_GROUPS
POOL_PAD = max(POOL_WINDOWS) - 1
RET_HEADS = 8
RET_DK = 64
RET_DV = BR_W // RET_HEADS
RET_QK_W = RET_HEADS * RET_DK
RET_CHUNK = 128
N_BR = 4
D_FF = 4 * D_MODEL
EPS = 1e-6
Q_W = N_HEADS * HEAD_DIM
KV_W = N_KV * HEAD_DIM
SPLIT_SIZES = (Q_W, KV_W, KV_W, 2 * CONV_W, POOL_W, RET_QK_W, RET_QK_W, BR_W, BR_W, N_BR * D_MODEL)
N_IN = sum(SPLIT_SIZES)

kernel_name = 'hybrid_gated_swa_conv_pool_retention_step'


def rms_norm(x, g):
    xf = x.astype(jnp.float32)
    y = xf * lax.rsqrt(jnp.mean(xf * xf, axis=-1, keepdims=True) + EPS)
    return (y * g.astype(jnp.float32)).astype(x.dtype)


def layer_norm(x, g, b):
    xf = x.astype(jnp.float32)
    xc = xf - jnp.mean(xf, axis=-1, keepdims=True)
    var = jnp.mean(xc * xc, axis=-1, keepdims=True)
    return (xc * lax.rsqrt(var + EPS) * g.astype(jnp.float32) + b.astype(jnp.float32)).astype(x.dtype)


def alibi_slopes():
    h = jnp.arange(1, N_HEADS + 1, dtype=jnp.float32)
    return jnp.exp2(-8.0 * h / N_HEADS).reshape(N_KV, GQA_G)


def sink_attention(q, k, v, dist, valid, sinks):
    s = jnp.einsum('bnqkgd,bnskd->bnkgqs', q.astype(jnp.float32), k.astype(jnp.float32)) * (HEAD_DIM ** -0.5)
    s = s - alibi_slopes()[:, :, None, None] * dist.astype(jnp.float32)
    s = jnp.where(valid[None, :, None, None], s, -jnp.inf)
    sink = jnp.broadcast_to(sinks.astype(jnp.float32).reshape(N_KV, GQA_G, 1, 1), s.shape[:-1] + (1,))
    p = jax.nn.softmax(jnp.concatenate([s, sink], axis=-1), axis=-1)[..., :-1]
    o = jnp.einsum('bnkgqs,bnskd->bnqkgd', p, v.astype(jnp.float32))
    return o.astype(q.dtype)


def attn_prompt(q, k, v, sinks):
    B, T = q.shape[:2]
    NB = T // ATT_BLOCK
    qb = q.reshape(B, NB, ATT_BLOCK, N_KV, GQA_G, HEAD_DIM)

    def with_prev(z):
        zb = z.reshape(B, NB, ATT_BLOCK, N_KV, HEAD_DIM)
        prev = jnp.concatenate([jnp.zeros_like(zb[:, :1]), zb[:, :-1]], axis=1)
        return jnp.concatenate([prev, zb], axis=2)

    i = jnp.arange(ATT_BLOCK)[:, None]
    j = jnp.arange(2 * ATT_BLOCK)[None, :]
    dist = ATT_BLOCK + i - j
    blk = jnp.arange(NB)[:, None, None]
    valid = ((dist >= 0) & (dist <= WINDOW))[None] & ((blk > 0) | (j[None] >= ATT_BLOCK))
    o = sink_attention(qb, with_prev(k), with_prev(v), dist, valid, sinks)
    return o.reshape(B, T, Q_W)


def attn_sample(q, k, v, k_buf, v_buf, sinks):
    B, T = q.shape[:2]
    kk = jnp.concatenate([k_buf.astype(k.dtype), k], axis=1)
    vv = jnp.concatenate([v_buf.astype(v.dtype), v], axis=1)
    i = jnp.arange(T)[:, None]
    j = jnp.arange(WINDOW + T)[None, :]
    dist = WINDOW + i - j
    valid = ((dist >= 0) & (dist <= WINDOW))[None]
    o = sink_attention(q.reshape(B, 1, T, N_KV, GQA_G, HEAD_DIM), kk[:, None], vv[:, None], dist, valid, sinks)
    return o.reshape(B, T, Q_W), kk[:, -WINDOW:], vv[:, -WINDOW:]


def conv_module(a, prefix, w_dw, b_dw, g_ln, b_ln):
    u = a[..., :CONV_W] * jax.nn.sigmoid(a[..., CONV_W:])
    ext = jnp.concatenate([prefix.astype(u.dtype), u], axis=1)
    y = lax.conv_general_dilated(ext, w_dw[:, None, :].astype(ext.dtype), window_strides=(1,), padding='VALID',
                                 dimension_numbers=('NWC', 'WIO', 'NWC'), feature_group_count=CONV_W)
    y = jax.nn.silu(layer_norm(y + b_dw, g_ln, b_ln))
    return y, ext[:, -(CONV_K - 1):]


def pool_mixer(u, prefix, pos0, w_pool, s_pool):
    B, T = u.shape[:2]
    ext_in = jnp.concatenate([prefix.astype(u.dtype), u], axis=1)
    ext = ext_in.astype(jnp.float32)
    cs = jnp.concatenate([jnp.zeros((B, 1, POOL_W), jnp.float32), lax.cumsum(ext, axis=1)], axis=1)
    pos = jnp.arange(T) + pos0
    outs = []
    for g, w in enumerate(POOL_WINDOWS):
        sl = slice(g * POOL_G, (g + 1) * POOL_G)
        wsum = cs[:, POOL_PAD + 1:POOL_PAD + 1 + T, sl] - cs[:, POOL_PAD + 1 - w:POOL_PAD + 1 - w + T, sl]
        cnt = jnp.minimum(pos + 1, w).astype(jnp.float32)
        outs.append(wsum / cnt[None, :, None])
    z = (jnp.concatenate(outs, axis=-1) - ext[:, POOL_PAD:]).reshape(B, T, POOL_GROUPS, POOL_G)
    z = jnp.einsum('btgc,gcd->btgd', z, w_pool.astype(jnp.float32)).reshape(B, T, POOL_W)
    z = z * s_pool.astype(jnp.float32)
    return z.astype(u.dtype), ext_in[:, -POOL_PAD:]


def retention(q, k, v, s0):
    B, T = q.shape[:2]
    C = RET_CHUNK if T % RET_CHUNK == 0 else T
    N = T // C
    log_g = jnp.log1p(-jnp.exp2(-5.0 - jnp.arange(RET_HEADS, dtype=jnp.float32)))
    qf = q.astype(jnp.float32).reshape(B, N, C, RET_HEADS, RET_DK)
    kf = k.astype(jnp.float32).reshape(B, N, C, RET_HEADS, RET_DK) * (RET_DK ** -0.5)
    vf = v.astype(jnp.float32).reshape(B, N, C, RET_HEADS, RET_DV)
    i = jnp.arange(C, dtype=jnp.float32)
    diff = i[:, None] - i[None, :]
    decay = jnp.where(diff >= 0, jnp.exp(log_g[:, None, None] * jnp.maximum(diff, 0.0)), 0.0)
    scores = jnp.einsum('bnihd,bnjhd->bnhij', qf, kf) * decay
    inner = jnp.einsum('bnhij,bnjhv->bnihv', scores, vf)
    k_dec = kf * jnp.exp(log_g[None, :] * (C - 1 - i)[:, None])[:, :, None]
    kv = jnp.einsum('bnjhd,bnjhv->bnhdv', k_dec, vf)
    g_chunk = jnp.exp(log_g * C)[None, :, None, None]

    def step(S, kv_n):
        return g_chunk * S + kv_n, S

    s_fin, s_prev = lax.scan(step, s0.astype(jnp.float32), jnp.moveaxis(kv, 1, 0))
    s_prev = jnp.moveaxis(s_prev, 0, 1)
    cross = jnp.einsum('bnihd,bnhdv->bnihv', qf, s_prev) * jnp.exp(log_g[None, :] * (i + 1)[:, None])[:, :, None]
    return (inner + cross).reshape(B, T, RET_HEADS, RET_DV), s_fin


def head_group_norm(o, g):
    B, T = o.shape[:2]
    oc = o - jnp.mean(o, axis=-1, keepdims=True)
    var = jnp.mean(oc * oc, axis=-1, keepdims=True)
    return (oc * lax.rsqrt(var + EPS)).reshape(B, T, BR_W) * g.astype(jnp.float32)


def trunk_layer(x, c, p, cache):
    (w_ada, b_ada, g_norm1, g_norm2, w_in, g_qnorm, g_knorm, attn_sinks, w_dw, b_dw, g_conv_ln, b_conv_ln,
     w_pool, s_pool, g_ret_norm, w_br, w_out, w_mlp1, w_mlp2) = p
    B, T, _ = x.shape
    ada = (jax.nn.silu(c) @ w_ada + b_ada)[:, None, :]
    sh1, sc1, gt1, sh2, sc2, gt2 = jnp.split(ada, 6, axis=-1)
    h = rms_norm(x, g_norm1) * (1 + sc1) + sh1
    z = h @ w_in
    pts = [int(s) for s in np.cumsum(SPLIT_SIZES)[:-1]]
    q, k, v, a_conv, u_pool, rq, rk, rv, rg, gpre = jnp.split(z, pts, axis=-1)
    q = rms_norm(q.reshape(B, T, N_HEADS, HEAD_DIM), g_qnorm)
    k = rms_norm(k.reshape(B, T, N_KV, HEAD_DIM), g_knorm)
    v = v.reshape(B, T, N_KV, HEAD_DIM)
    if cache is None:
        y_att = attn_prompt(q, k, v, attn_sinks)
        k_new, v_new = k[:, -WINDOW:], v[:, -WINDOW:]
        conv_prefix = jnp.zeros((B, CONV_K - 1, CONV_W), z.dtype)
        pool_prefix = jnp.zeros((B, POOL_PAD, POOL_W), z.dtype)
        s0 = jnp.zeros((B, RET_HEADS, RET_DK, RET_DV), jnp.float32)
        pos0 = 0
    else:
        k_buf, v_buf, conv_prefix, pool_prefix, s0 = cache
        y_att, k_new, v_new = attn_sample(q, k, v, k_buf, v_buf, attn_sinks)
        pos0 = PAST_LEN
    y_conv, conv_new = conv_module(a_conv, conv_prefix, w_dw, b_dw, g_conv_ln, b_conv_ln)
    y_pool, pool_new = pool_mixer(u_pool, pool_prefix, pos0, w_pool, s_pool)
    o_ret, s_new = retention(rq.reshape(B, T, RET_HEADS, RET_DK), rk.reshape(B, T, RET_HEADS, RET_DK),
                             rv.reshape(B, T, RET_HEADS, RET_DV), s0)
    y_ret = (head_group_norm(o_ret, g_ret_norm) * jax.nn.silu(rg.astype(jnp.float32))).astype(x.dtype)
    branches = jnp.stack([y_att, y_conv, y_pool, y_ret], axis=2)
    proj = jnp.einsum('btrc,rcd->btrd', branches, w_br)
    gates = jax.nn.sigmoid(gpre.astype(jnp.float32)).reshape(B, T, N_BR, D_MODEL)
    merged = jnp.einsum('btrd,btrd->btd', gates, proj.astype(jnp.float32)).astype(x.dtype)
    x = x + gt1 * (merged @ w_out)
    h2 = rms_norm(x, g_norm2) * (1 + sc2) + sh2
    x = x + gt2 * (jnp.square(jax.nn.relu(h2 @ w_mlp1)) @ w_mlp2)
    return x, (k_new, v_new, conv_new, pool_new, s_new)


def setup_inputs(seed: int = 0) -> dict:
    key = jax.random.key(seed)
    ks = jax.random.split(key, 32)
    f32 = jnp.float32

    def nrm(k, shape, s):
        return jax.random.normal(k, shape, f32) * s

    return {
        'x_prompt': nrm(ks[0], (BATCH, SEQ, D_MODEL), 1.0),
        'x_sample': nrm(ks[1], (DEC_BATCH, DEC_SEQ, D_MODEL), 1.0),
        'c_prompt': nrm(ks[2], (BATCH, D_MODEL), 1.0),
        'c_sample': nrm(ks[3], (DEC_BATCH, D_MODEL), 1.0),
        'cache_attn_k': nrm(ks[4], (DEPTH, DEC_BATCH, WINDOW, N_KV, HEAD_DIM), 1.0),
        'cache_attn_v': nrm(ks[5], (DEPTH, DEC_BATCH, WINDOW, N_KV, HEAD_DIM), 1.0),
        'state_conv': nrm(ks[6], (DEPTH, DEC_BATCH, CONV_K - 1, CONV_W), 0.5),
        'state_pool': nrm(ks[7], (DEPTH, DEC_BATCH, POOL_PAD, POOL_W), 1.0),
        'state_ret': nrm(ks[8], (DEPTH, DEC_BATCH, RET_HEADS, RET_DK, RET_DV), 1.0),
        'w_ada': nrm(ks[9], (DEPTH, D_MODEL, 6 * D_MODEL), 0.2 * D_MODEL ** -0.5),
        'b_ada': nrm(ks[10], (DEPTH, 6 * D_MODEL), 0.02),
        'g_norm1': 1.0 + nrm(ks[11], (DEPTH, D_MODEL), 0.02),
        'g_norm2': 1.0 + nrm(ks[12], (DEPTH, D_MODEL), 0.02),
        'w_in': nrm(ks[13], (DEPTH, D_MODEL, N_IN), D_MODEL ** -0.5),
        'g_qnorm': 1.0 + nrm(ks[14], (DEPTH, HEAD_DIM), 0.02),
        'g_knorm': 1.0 + nrm(ks[15], (DEPTH, HEAD_DIM), 0.02),
        'attn_sinks': nrm(ks[16], (DEPTH, N_HEADS), 0.5),
        'w_dw': nrm(ks[17], (DEPTH, CONV_K, CONV_W), CONV_K ** -0.5),
        'b_dw': nrm(ks[18], (DEPTH, CONV_W), 0.02),
        'g_conv_ln': 1.0 + nrm(ks[19], (DEPTH, CONV_W), 0.02),
        'b_conv_ln': nrm(ks[20], (DEPTH, CONV_W), 0.02),
        'w_pool': nrm(ks[21], (DEPTH, POOL_GROUPS, POOL_G, POOL_G), POOL_G ** -0.5),
        's_pool': 1.0 + nrm(ks[22], (DEPTH, POOL_W), 0.02),
        'g_ret_norm': 1.0 + nrm(ks[23], (DEPTH, BR_W), 0.02),
        'w_br': nrm(ks[24], (DEPTH, N_BR, BR_W, D_MODEL), BR_W ** -0.5),
        'w_out': nrm(ks[25], (DEPTH, D_MODEL, D_MODEL), D_MODEL ** -0.5),
        'w_mlp1': nrm(ks[26], (DEPTH, D_MODEL, D_FF), D_MODEL ** -0.5),
        'w_mlp2': nrm(ks[27], (DEPTH, D_FF, D_MODEL), D_FF ** -0.5),
    }


def reference(x_prompt, x_sample, c_prompt, c_sample, cache_attn_k, cache_attn_v, state_conv, state_pool, state_ret,
              w_ada, b_ada, g_norm1, g_norm2, w_in, g_qnorm, g_knorm, attn_sinks, w_dw, b_dw, g_conv_ln, b_conv_ln,
              w_pool, s_pool, g_ret_norm, w_br, w_out, w_mlp1, w_mlp2):
    yp, ys = x_prompt, x_sample
    new_p = [[], [], [], [], []]
    new_s = [[], [], [], [], []]
    for l in range(DEPTH):
        p = (w_ada[l], b_ada[l], g_norm1[l], g_norm2[l], w_in[l], g_qnorm[l], g_knorm[l], attn_sinks[l], w_dw[l],
             b_dw[l], g_conv_ln[l], b_conv_ln[l], w_pool[l], s_pool[l], g_ret_norm[l], w_br[l], w_out[l],
             w_mlp1[l], w_mlp2[l])
        yp, st_p = trunk_layer(yp, c_prompt, p, None)
        ys, st_s = trunk_layer(ys, c_sample, p, (cache_attn_k[l], cache_attn_v[l], state_conv[l], state_pool[l], state_ret[l]))
        for lst, a in zip(new_p, st_p):
            lst.append(a)
        for lst, a in zip(new_s, st_s):
            lst.append(a)
    new_attn_k_prompt = jnp.stack(new_p[0])
    new_attn_v_prompt = jnp.stack(new_p[1])
    new_conv_prompt = jnp.stack(new_p[2])
    new_pool_prompt = jnp.stack(new_p[3])
    new_ret_prompt = jnp.stack(new_p[4])
    new_attn_k_sample = jnp.stack(new_s[0])
    new_attn_v_sample = jnp.stack(new_s[1])
    new_conv_sample = jnp.stack(new_s[2])
    new_pool_sample = jnp.stack(new_s[3])
    new_ret_sample = jnp.stack(new_s[4])
    return (yp, ys, new_attn_k_prompt, new_attn_v_prompt, new_conv_prompt, new_pool_prompt, new_ret_prompt,
            new_attn_k_sample, new_attn_v_sample, new_conv_sample, new_pool_sample, new_ret_sample)
```

```python
import functools

import numpy as np
import jax
import jax.numpy as jnp
from jax import lax
from jax.experimental import pallas as pl
from jax.experimental.pallas import tpu as pltpu

F32 = jnp.float32
BF16 = jnp.bfloat16

D_MODEL = 2048
PAST_LEN = 16384
N_HEADS = 16
HEAD_DIM = 64
N_KV = 4
WINDOW = 128
BR_W = 1024
CONV_K = 31
POOL_WINDOWS = (2, 4, 8, 16)
POOL_G = 256
POOL_PAD = 15
RET_HEADS = 8
RET_DK = 64
RET_DV = 128
RET_CHUNK = 128
N_BR = 4
D_FF = 4 * D_MODEL
EPS = 1e-6
KV_W = N_KV * HEAD_DIM
N_IN = 15872

SLAB = 512
N_SLAB = N_IN // SLAB
SLAB_Q, SLAB_KV, SLAB_CONV, SLAB_POOL, SLAB_RQ, SLAB_RK, SLAB_RV, SLAB_RG, SLAB_GATE = 0, 2, 3, 7, 9, 10, 11, 13, 15

VMEM_LIMIT_BYTES = 56 * 1024 * 1024
LANES = 128
NEG_BIG = -1e30
CONV_HALO = 32
POOL_HALO = 16


def _params(*sem):
    return pltpu.CompilerParams(dimension_semantics=sem, vmem_limit_bytes=VMEM_LIMIT_BYTES)


def _nt_dot(a, b):
    return lax.dot_general(a, b, (((1,), (1,)), ((), ())), preferred_element_type=F32)


def _tn_dot(a, b):
    return lax.dot_general(a, b, (((0,), (0,)), ((), ())), preferred_element_type=F32)


def _dot(a, b):
    return jnp.dot(a, b, preferred_element_type=F32)


def _silu(x):
    return x * jax.nn.sigmoid(x)


def _ada_kernel(c_ref, w_ref, b_ref, o_ref):
    s = _silu(c_ref[...]).astype(BF16)
    o_ref[...] = _dot(s, w_ref[...].astype(BF16)) + b_ref[...]


def _ada_call(c_all, w_ada, b_ada):
    depth, d, n = w_ada.shape
    r = c_all.shape[0]
    tn = 1024
    return pl.pallas_call(
        _ada_kernel,
        grid=(depth, n // tn),
        in_specs=[
            pl.BlockSpec((r, d), lambda l, j: (0, 0)),
            pl.BlockSpec((None, d, tn), lambda l, j: (l, 0, j)),
            pl.BlockSpec((None, 1, tn), lambda l, j: (l, 0, j)),
        ],
        out_specs=pl.BlockSpec((None, r, tn), lambda l, j: (l, 0, j)),
        out_shape=jax.ShapeDtypeStruct((depth, r, n), F32),
        compiler_params=_params("arbitrary", "arbitrary"),
        name="ada",
    )(c_all, w_ada, b_ada.reshape(depth, 1, n))


class _Mod:
    def __init__(self, arr, layer, per_row, tm, tiles_per_batch):
        self.arr, self.l, self.per_row, self.tm, self.tpb = arr, layer, per_row, tm, tiles_per_batch

    def spec(self, k, width=D_MODEL, col=None):
        l, tpb, nb = self.l, self.tpb, D_MODEL // width

        def cidx(idx):
            return k * nb + (idx[col] if col is not None else 0)

        if self.per_row:
            return pl.BlockSpec((None, self.tm, width), lambda *idx: (l, idx[0], cidx(idx)))
        return pl.BlockSpec((None, None, 1, width), lambda *idx: (l, idx[0] // tpb, 0, cidx(idx)))


def _mod_rows(ref, r, rc):
    return ref[...] if ref.shape[0] == 1 else ref[pl.ds(r, rc), :]


def _modnorm_to(h_ref, x_ref, sc_ref, sh_ref, g_ref, rc):
    g = g_ref[...]

    def body(c, carry):
        r = pl.multiple_of(c * rc, rc)
        x = x_ref[pl.ds(r, rc), :]
        ms = jnp.mean(x * x, axis=-1, keepdims=True)
        y = x * lax.rsqrt(ms + EPS) * g
        h = y * (1.0 + _mod_rows(sc_ref, r, rc)) + _mod_rows(sh_ref, r, rc)
        h_ref[pl.ds(r, rc), :] = h.astype(h_ref.dtype)
        return carry

    lax.fori_loop(0, x_ref.shape[0] // rc, body, 0)


def _row_chunk(tm):
    return 128 if tm % 128 == 0 else tm


def _inproj_kernel(x_ref, sc_ref, sh_ref, g_ref, w_ref, o_ref, h_ref, *, rc):
    @pl.when(pl.program_id(1) == 0)
    def _():
        _modnorm_to(h_ref, x_ref, sc_ref, sh_ref, g_ref, rc)

    o_ref[...] = _dot(h_ref[...], w_ref[...].astype(BF16))


def _inproj_call(x, mod, g_norm, w_in, layer, tm):
    m, d = x.shape
    depth = w_in.shape[0]
    return pl.pallas_call(
        functools.partial(_inproj_kernel, rc=_row_chunk(tm)),
        grid=(m // tm, N_SLAB),
        in_specs=[
            pl.BlockSpec((tm, d), lambda i, j: (i, 0)),
            mod.spec(1), mod.spec(0),
            pl.BlockSpec((None, 1, d), lambda i, j: (layer, 0, 0)),
            pl.BlockSpec((None, d, SLAB), lambda i, j: (layer, 0, j)),
        ],
        out_specs=pl.BlockSpec((None, tm, SLAB), lambda i, j: (j, i, 0)),
        out_shape=jax.ShapeDtypeStruct((N_SLAB, m, SLAB), F32),
        scratch_shapes=[pltpu.VMEM((tm, d), BF16)],
        compiler_params=_params("arbitrary", "arbitrary"),
        name="inproj",
    )(x, mod.arr, mod.arr, g_norm.reshape(depth, 1, d), w_in)


def _group_sum_matrix(width, group):
    idx = np.arange(width) // group
    return jnp.asarray((idx[:, None] == idx[None, :]).astype(np.float32), dtype=BF16)


def _attn_kernel(*refs, layer, tq, has_init):
    if has_init:
        sinks_ref, q_ref, kvc_ref, kinit_ref, vinit_ref, gq_ref, gk_ref, gm512_ref, gm256_ref, sd_ref, o_ref, kn_ref = refs
    else:
        sinks_ref, q_ref, kvc_ref, kvp_ref, gq_ref, gk_ref, gm512_ref, gm256_ref, sd_ref, o_ref, kn_ref = refs
    n = pl.program_id(1)

    def qk_norm(x, gmat, g):
        x2 = x * x
        hi = x2.astype(BF16)
        lo = (x2 - hi.astype(F32)).astype(BF16)
        ss = _dot(hi, gmat) + _dot(lo, gmat)
        return x * lax.rsqrt(ss * (1.0 / HEAD_DIM) + EPS) * g

    gm256, gk = gm256_ref[...], gk_ref[...]
    kvc = kvc_ref[...]
    kc = qk_norm(kvc[:, :KV_W], gm256, gk)
    kn_ref[...] = kc
    vc = kvc[:, KV_W:]
    if has_init:
        kp, vp = kinit_ref[...], vinit_ref[...]
    else:
        kvp = kvp_ref[...]
        kp, vp = qk_norm(kvp[:, :KV_W], gm256, gk), kvp[:, KV_W:]
    if tq < WINDOW:
        pad = jnp.zeros((WINDOW - tq, KV_W), F32)
        kc = jnp.concatenate([kc, pad], axis=0)
        vc = jnp.concatenate([vc, pad], axis=0)
    kall = jnp.concatenate([kp, kc], axis=0)
    vall = jnp.concatenate([vp, vc], axis=0)

    nk = 2 * WINDOW
    ii = lax.broadcasted_iota(jnp.int32, (tq, nk), 0)
    jj = lax.broadcasted_iota(jnp.int32, (tq, nk), 1)
    dist = WINDOW + ii - jj
    valid = (dist >= 0) & (dist <= WINDOW)
    if not has_init:
        valid = valid & (jj >= jnp.where(n > 0, 0, WINDOW))
    distf = dist.astype(F32)

    gm512, gq = gm512_ref[...], gq_ref[...]
    qn = [qk_norm(q_ref[s], gm512, gq) for s in range(2)]
    lane_k = lax.broadcasted_iota(jnp.int32, (nk, LANES), 1) < HEAD_DIM
    lane_q = lax.broadcasted_iota(jnp.int32, (tq, LANES), 1) < HEAD_DIM
    sd = sd_ref[...]
    scale = HEAD_DIM ** -0.5

    for kv in range(N_KV):
        cs = slice(LANES * (kv // 2), LANES * (kv // 2) + LANES)

        def two_copies(a, upper=bool(kv % 2)):
            if upper:
                bot = jnp.where(lane_k, 0.0, a)
                top = pltpu.roll(bot, HEAD_DIM, 1)
            else:
                top = jnp.where(lane_k, a, 0.0)
                bot = pltpu.roll(top, HEAD_DIM, 1)
            return jnp.concatenate([top, bot], axis=0).astype(BF16)

        kd = two_copies(kall[:, cs])
        vd = two_copies(vall[:, cs])
        for pp in range(2):
            p = 2 * kv + pp
            off = LANES * (p % 4)
            qp = qn[p // 4][:, off:off + LANES].astype(BF16)
            s = _nt_dot(qp, kd)
            es, sk = [], []
            for hh in range(2):
                h = 2 * p + hh
                slope = 2.0 ** (-8.0 * (h + 1) / N_HEADS)
                sink = sinks_ref[layer, h]
                sh = s[:, nk * hh:nk * hh + nk] * scale - slope * distf
                sh = jnp.where(valid, sh, NEG_BIG)
                mx = jnp.maximum(jnp.max(sh, axis=-1, keepdims=True), sink)
                es.append(jnp.exp(sh - mx))
                sk.append(jnp.exp(sink - mx))
            e = jnp.concatenate(es, axis=1).astype(BF16)
            num = _dot(e, vd)
            den = _dot(e, sd) + jnp.where(lane_q, sk[0], sk[1])
            o_ref[:, LANES * p:LANES * p + LANES] = (num / den).astype(o_ref.dtype)


def _attn_call(z3, sinks, gq_t, gk_t, layer, nb_batch, t_len, cache=None):
    m = z3.shape[1]
    has_init = cache is not None
    tq = WINDOW if t_len % WINDOW == 0 else t_len
    nb = t_len // tq
    assert has_init == (nb == 1) or not has_init
    depth = gq_t.shape[0]
    gm512 = _group_sum_matrix(SLAB, HEAD_DIM)
    gm256 = _group_sum_matrix(KV_W, HEAD_DIM)
    sd_np = np.zeros((4 * WINDOW, LANES), np.float32)
    sd_np[:2 * WINDOW, :HEAD_DIM] = 1.0
    sd_np[2 * WINDOW:, HEAD_DIM:] = 1.0
    sd = jnp.asarray(sd_np, dtype=BF16)

    in_specs = [
        pl.BlockSpec(memory_space=pltpu.SMEM),
        pl.BlockSpec((2, tq, SLAB), lambda b, n: (0, b * nb + n, 0)),
        pl.BlockSpec((None, tq, SLAB), lambda b, n: (SLAB_KV, b * nb + n, 0)),
    ]
    args = [sinks, z3, z3]
    if has_init:
        in_specs += [pl.BlockSpec((None, None, WINDOW, KV_W), lambda b, n: (layer, b, 0, 0))] * 2
        args += [cache[0], cache[1]]
    else:
        in_specs += [pl.BlockSpec((None, WINDOW, SLAB), lambda b, n: (SLAB_KV, jnp.maximum(b * nb + n - 1, 0), 0))]
        args += [z3]
    in_specs += [
        pl.BlockSpec((None, 1, SLAB), lambda b, n: (layer, 0, 0)),
        pl.BlockSpec((None, 1, KV_W), lambda b, n: (layer, 0, 0)),
        pl.BlockSpec((SLAB, SLAB), lambda b, n: (0, 0)),
        pl.BlockSpec((KV_W, KV_W), lambda b, n: (0, 0)),
        pl.BlockSpec((4 * WINDOW, LANES), lambda b, n: (0, 0)),
    ]
    args += [gq_t, gk_t, gm512, gm256, sd]
    out_dtype = F32 if has_init else BF16
    return pl.pallas_call(
        functools.partial(_attn_kernel, layer=layer, tq=tq, has_init=has_init),
        grid=(nb_batch, nb),
        in_specs=in_specs,
        out_specs=[
            pl.BlockSpec((tq, BR_W), lambda b, n: (b * nb + n, 0)),
            pl.BlockSpec((tq, KV_W), lambda b, n: (b * nb + n, 0)),
        ],
        out_shape=[jax.ShapeDtypeStruct((m, BR_W), out_dtype), jax.ShapeDtypeStruct((m, KV_W), F32)],
        compiler_params=_params("arbitrary", "arbitrary"),
        name="attn",
    )(*args)


def _conv_kernel(*refs, tt, nt, has_init):
    if has_init:
        l0, l1, g0, g1, init_ref, w_ref, b_ref, gl_ref, bl_ref, o_ref, new_ref, ext, ybuf = refs
    else:
        l0, l1, g0, g1, w_ref, b_ref, gl_ref, bl_ref, o_ref, new_ref, ext, ybuf = refs
    t = pl.program_id(1)

    @pl.when(t == 0)
    def _():
        ext[0:CONV_HALO, :] = init_ref[...] if has_init else jnp.zeros((CONV_HALO, BR_W), F32)

    if nt > 1:
        @pl.when(t > 0)
        def _():
            ext[0:CONV_HALO, :] = ext[tt:tt + CONV_HALO, :]

    for cb, (lr, gr) in enumerate(((l0, g0), (l1, g1))):
        ext[CONV_HALO:CONV_HALO + tt, SLAB * cb:SLAB * cb + SLAB] = lr[...] * jax.nn.sigmoid(gr[...])
    new_ref[...] = ext[tt:tt + CONV_HALO, :]

    rs = min(tt, 128)
    base = CONV_HALO - (CONV_K - 1)
    for r0 in range(0, tt, rs):
        for c in range(BR_W // LANES):
            cs = slice(LANES * c, LANES * c + LANES)
            acc = jnp.broadcast_to(b_ref[:, cs], (rs, LANES))
            for k in range(CONV_K):
                acc = acc + w_ref[k:k + 1, cs] * ext[r0 + base + k:r0 + base + k + rs, cs]
            ybuf[r0:r0 + rs, cs] = acc
        y = ybuf[r0:r0 + rs, :]
        yc = y - jnp.mean(y, axis=-1, keepdims=True)
        var = jnp.mean(yc * yc, axis=-1, keepdims=True)
        yn = yc * lax.rsqrt(var + EPS) * gl_ref[...] + bl_ref[...]
        o_ref[r0:r0 + rs, :] = _silu(yn).astype(o_ref.dtype)


def _conv_call(z3, w_dw, b_dw, g_ln, b_ln, layer, nb_batch, t_len, init=None):
    m = z3.shape[1]
    has_init = init is not None
    tt = 256 if t_len % 256 == 0 else t_len
    nt = t_len // tt
    assert nt == 1 or tt >= CONV_HALO
    depth = w_dw.shape[0]

    def slab(j):
        return pl.BlockSpec((None, tt, SLAB), lambda b, t: (j, b * nt + t, 0))

    def vec():
        return pl.BlockSpec((None, 1, BR_W), lambda b, t: (layer, 0, 0))

    in_specs = [slab(SLAB_CONV), slab(SLAB_CONV + 1), slab(SLAB_CONV + 2), slab(SLAB_CONV + 3)]
    args = [z3, z3, z3, z3]
    if has_init:
        in_specs.append(pl.BlockSpec((None, None, CONV_HALO, BR_W), lambda b, t: (layer, b, 0, 0)))
        args.append(init)
    in_specs += [pl.BlockSpec((None, CONV_K, BR_W), lambda b, t: (layer, 0, 0)), vec(), vec(), vec()]
    args += [w_dw, b_dw.reshape(depth, 1, BR_W), g_ln.reshape(depth, 1, BR_W), b_ln.reshape(depth, 1, BR_W)]
    return pl.pallas_call(
        functools.partial(_conv_kernel, tt=tt, nt=nt, has_init=has_init),
        grid=(nb_batch, nt),
        in_specs=in_specs,
        out_specs=[
            pl.BlockSpec((tt, BR_W), lambda b, t: (b * nt + t, 0)),
            pl.BlockSpec((None, CONV_HALO, BR_W), lambda b, t: (b, 0, 0)),
        ],
        out_shape=[
            jax.ShapeDtypeStruct((m, BR_W), F32 if has_init else BF16),
            jax.ShapeDtypeStruct((nb_batch, CONV_HALO, BR_W), F32),
        ],
        scratch_shapes=[pltpu.VMEM((CONV_HALO + tt, BR_W), F32), pltpu.VMEM((tt, BR_W), F32)],
        compiler_params=_params("arbitrary", "arbitrary"),
        name="conv",
    )(*args)


def _pool_kernel(*refs, tt, nt, has_init, pos0):
    if has_init:
        u0, u1, init_ref, w_ref, s_ref, o_ref, ext = refs
    else:
        u0, u1, w_ref, s_ref, o_ref, ext = refs
    t = pl.program_id(1)

    @pl.when(t == 0)
    def _():
        ext[0:POOL_HALO, :] = init_ref[...] if has_init else jnp.zeros((POOL_HALO, BR_W), F32)

    if nt > 1:
        @pl.when(t > 0)
        def _():
            ext[0:POOL_HALO, :] = ext[tt:tt + POOL_HALO, :]

    ext[POOL_HALO:POOL_HALO + tt, 0:SLAB] = u0[...]
    ext[POOL_HALO:POOL_HALO + tt, SLAB:2 * SLAB] = u1[...]
    pos = pos0 + t * tt + lax.broadcasted_iota(jnp.int32, (tt, 1), 0)
    for g, w in enumerate(POOL_WINDOWS):
        cs = slice(POOL_G * g, POOL_G * g + POOL_G)
        cur = ext[POOL_HALO:POOL_HALO + tt, cs]
        wsum = cur
        for s in range(1, w):
            wsum = wsum + ext[POOL_HALO - s:POOL_HALO - s + tt, cs]
        cnt = jnp.minimum(pos + 1, w).astype(F32)
        zg = wsum / cnt - cur
        y = _dot(zg.astype(BF16), w_ref[g].astype(BF16)) * s_ref[:, cs]
        o_ref[:, cs] = y.astype(o_ref.dtype)


def _pool_call(z3, w_pool, s_pool, layer, nb_batch, t_len, pos0, init=None):
    m = z3.shape[1]
    has_init = init is not None
    tt = 256 if t_len % 256 == 0 else t_len
    nt = t_len // tt
    assert nt == 1 or tt >= POOL_HALO
    depth = w_pool.shape[0]

    def slab(j):
        return pl.BlockSpec((None, tt, SLAB), lambda b, t: (j, b * nt + t, 0))

    in_specs = [slab(SLAB_POOL), slab(SLAB_POOL + 1)]
    args = [z3, z3]
    if has_init:
        in_specs.append(pl.BlockSpec((None, None, POOL_HALO, BR_W), lambda b, t: (layer, b, 0, 0)))
        args.append(init)
    in_specs += [
        pl.BlockSpec((None, len(POOL_WINDOWS), POOL_G, POOL_G), lambda b, t: (layer, 0, 0, 0)),
        pl.BlockSpec((None, 1, BR_W), lambda b, t: (layer, 0, 0)),
    ]
    args += [w_pool, s_pool.reshape(depth, 1, BR_W)]
    return pl.pallas_call(
        functools.partial(_pool_kernel, tt=tt, nt=nt, has_init=has_init, pos0=pos0),
        grid=(nb_batch, nt),
        in_specs=in_specs,
        out_specs=pl.BlockSpec((tt, BR_W), lambda b, t: (b * nt + t, 0)),
        out_shape=jax.ShapeDtypeStruct((m, BR_W), F32 if has_init else BF16),
        scratch_shapes=[pltpu.VMEM((POOL_HALO + tt, BR_W), F32)],
        compiler_params=_params("arbitrary", "arbitrary"),
        name="pool",
    )(*args)


def _ret_constants(c, cp):
    lg = np.log1p(-np.exp2(-5.0 - np.arange(RET_HEADS, dtype=np.float64)))
    i = np.arange(c, dtype=np.float64)
    diff = i[:, None] - i[None, :]
    decay = np.where(diff >= 0, np.exp(lg[:, None, None] * np.maximum(diff, 0.0)), 0.0)
    dec = np.zeros((RET_HEADS // 2, c, 2 * cp))
    for h in range(RET_HEADS):
        dec[h // 2, :, (h % 2) * cp:(h % 2) * cp + c] = decay[h]
    kfac = np.repeat(np.exp(lg[None, :] * (c - 1 - i)[:, None]), RET_DK, axis=1) * RET_DK ** -0.5
    cfac = np.repeat(np.exp(lg[None, :] * (i + 1)[:, None]), RET_DV, axis=1)
    gch = np.repeat(np.exp(lg * c), RET_DK).reshape(RET_HEADS // 2, 2 * RET_DK, 1)
    gch = np.broadcast_to(gch, (RET_HEADS // 2, 2 * RET_DK, RET_DV))
    f = lambda a: jnp.asarray(np.ascontiguousarray(a), dtype=F32)
    return f(dec), f(kfac), f(cfac), f(gch)


def _ret_kernel(*refs, c, cp, has_init):
    if has_init:
        rq, rk, rv0, rv1, rg0, rg1, s0_ref, gn_ref, dec_ref, kf_ref, cf_ref, gch_ref, o_ref, s_ref = refs
    else:
        rq, rk, rv0, rv1, rg0, rg1, gn_ref, dec_ref, kf_ref, cf_ref, gch_ref, o_ref, s_ref = refs
    n = pl.program_id(1)

    @pl.when(n == 0)
    def _():
        s_ref[...] = s0_ref[...] if has_init else jnp.zeros(s_ref.shape, F32)

    lo = lax.broadcasted_iota(jnp.int32, (c, LANES), 1) < RET_DK
    q = rq[...]
    kraw = rk[...]
    k = kraw * (RET_DK ** -0.5)
    kdec = kraw * kf_ref[...]
    rvs, rgs = (rv0, rv1), (rg0, rg1)

    def stack_heads(a):
        a0, a1 = jnp.where(lo, a, 0.0), jnp.where(lo, 0.0, a)
        if cp > c:
            z = jnp.zeros((cp - c, LANES), F32)
            return jnp.concatenate([a0, z, a1, z], axis=0)
        return jnp.concatenate([a0, a1], axis=0)

    for p in range(RET_HEADS // 2):
        cs = slice(LANES * p, LANES * p + LANES)
        qp = q[:, cs]
        s = _nt_dot(qp.astype(BF16), stack_heads(k[:, cs]).astype(BF16)) * dec_ref[p]
        hs = (2 * p, 2 * p + 1)
        vs = [rvs[h // 4][:, LANES * (h % 4):LANES * (h % 4) + LANES] for h in hs]
        zc = jnp.zeros((c, LANES), F32)
        rows0 = jnp.concatenate([vs[0], zc], axis=1)
        rows1 = jnp.concatenate([zc, vs[1]], axis=1)
        if cp > c:
            zp = jnp.zeros((cp - c, 2 * LANES), F32)
            zq = jnp.zeros((cp - c, LANES), F32)
            vbd = jnp.concatenate([rows0, zp, rows1, zp], axis=0)
            vst = jnp.concatenate([vs[0], zq, vs[1], zq], axis=0)
        else:
            vbd = jnp.concatenate([rows0, rows1], axis=0)
            vst = jnp.concatenate([vs[0], vs[1]], axis=0)
        inner = _dot(s.astype(BF16), vbd.astype(BF16))
        sprev = s_ref[p]
        qst = jnp.concatenate([jnp.where(lo, qp, 0.0), jnp.where(lo, 0.0, qp)], axis=0)
        cross = _dot(qst.astype(BF16), sprev.astype(BF16))
        upd = _tn_dot(stack_heads(kdec[:, cs]).astype(BF16), vst.astype(BF16))
        s_ref[p] = gch_ref[p] * sprev + upd
        for hh, h in enumerate(hs):
            hc = slice(LANES * h, LANES * h + LANES)
            o = inner[:, LANES * hh:LANES * hh + LANES] + cross[c * hh:c * hh + c, :] * cf_ref[:, hc]
            oc = o - jnp.mean(o, axis=-1, keepdims=True)
            var = jnp.mean(oc * oc, axis=-1, keepdims=True)
            gate = rgs[h // 4][:, LANES * (h % 4):LANES * (h % 4) + LANES]
            y = oc * lax.rsqrt(var + EPS) * gn_ref[:, hc] * _silu(gate)
            o_ref[:, hc] = y.astype(o_ref.dtype)


def _ret_call(z3, g_ret, layer, nb_batch, t_len, init=None):
    m = z3.shape[1]
    has_init = init is not None
    c = RET_CHUNK if t_len % RET_CHUNK == 0 else t_len
    cp = max(c, 64)
    nc = t_len // c
    depth = g_ret.shape[0]
    dec, kfac, cfac, gch = _ret_constants(c, cp)
    npair = RET_HEADS // 2

    def slab(j):
        return pl.BlockSpec((None, c, SLAB), lambda b, n: (j, b * nc + n, 0))

    def const(shape):
        nd = len(shape)
        return pl.BlockSpec(shape, lambda b, n: (0,) * nd)

    in_specs = [slab(SLAB_RQ), slab(SLAB_RK), slab(SLAB_RV), slab(SLAB_RV + 1), slab(SLAB_RG), slab(SLAB_RG + 1)]
    args = [z3] * 6
    if has_init:
        in_specs.append(pl.BlockSpec((None, None, npair, LANES, RET_DV), lambda b, n: (layer, b, 0, 0, 0)))
        args.append(init)
    in_specs += [pl.BlockSpec((None, 1, BR_W), lambda b, n: (layer, 0, 0)),
                 const(dec.shape), const(kfac.shape), const(cfac.shape), const(gch.shape)]
    args += [g_ret.reshape(depth, 1, BR_W), dec, kfac, cfac, gch]
    return pl.pallas_call(
        functools.partial(_ret_kernel, c=c, cp=cp, has_init=has_init),
        grid=(nb_batch, nc),
        in_specs=in_specs,
        out_specs=[
            pl.BlockSpec((c, BR_W), lambda b, n: (b * nc + n, 0)),
            pl.BlockSpec((None, npair, LANES, RET_DV), lambda b, n: (b, 0, 0, 0)),
        ],
        out_shape=[
            jax.ShapeDtypeStruct((m, BR_W), F32 if has_init else BF16),
            jax.ShapeDtypeStruct((nb_batch, npair, LANES, RET_DV), F32),
        ],
        compiler_params=_params("arbitrary", "arbitrary"),
        name="retention",
    )(*args)


def _merge_kernel(y0, y1, y2, y3, gate_ref, w_ref, o_ref, acc_ref):
    r, c = pl.program_id(1), pl.program_id(2)
    for k, y_ref in enumerate((y0, y1, y2, y3)):
        @pl.when(r == k)
        def _(k=k, y_ref=y_ref):
            val = jax.nn.sigmoid(gate_ref[...]) * _dot(y_ref[...].astype(BF16), w_ref[...].astype(BF16))
            if k == 0:
                acc_ref[c] = val
            elif k < N_BR - 1:
                acc_ref[c] += val
            else:
                o_ref[...] = (acc_ref[c] + val).astype(o_ref.dtype)


def _merge_call(ys, z3, w_br, layer, tm):
    m = z3.shape[1]
    nc = D_MODEL // SLAB
    y_spec = pl.BlockSpec((tm, BR_W), lambda i, r, c: (i, 0))
    return pl.pallas_call(
        _merge_kernel,
        grid=(m // tm, N_BR, nc),
        in_specs=[y_spec] * N_BR + [
            pl.BlockSpec((None, tm, SLAB), lambda i, r, c: (SLAB_GATE + nc * r + c, i, 0)),
            pl.BlockSpec((None, None, BR_W, SLAB), lambda i, r, c: (layer, r, 0, c)),
        ],
        out_specs=pl.BlockSpec((tm, SLAB), lambda i, r, c: (i, jnp.where(r == N_BR - 1, c, 0))),
        out_shape=jax.ShapeDtypeStruct((m, D_MODEL), BF16),
        scratch_shapes=[pltpu.VMEM((nc, tm, SLAB), F32)],
        compiler_params=_params("arbitrary", "arbitrary", "arbitrary"),
        name="merge",
    )(*ys, z3, w_br)


def _outproj_kernel(m_ref, w_ref, x_ref, gt_ref, o_ref):
    o_ref[...] = x_ref[...] + gt_ref[...] * _dot(m_ref[...], w_ref[...].astype(BF16))


def _outproj_call(merged, x, mod, w_out, layer, tm):
    m, d = x.shape
    return pl.pallas_call(
        _outproj_kernel,
        grid=(m // tm, d // SLAB),
        in_specs=[
            pl.BlockSpec((tm, d), lambda i, c: (i, 0)),
            pl.BlockSpec((None, d, SLAB), lambda i, c: (layer, 0, c)),
            pl.BlockSpec((tm, SLAB), lambda i, c: (i, c)),
            mod.spec(2, width=SLAB, col=1),
        ],
        out_specs=pl.BlockSpec((tm, SLAB), lambda i, c: (i, c)),
        out_shape=jax.ShapeDtypeStruct((m, d), F32),
        compiler_params=_params("arbitrary", "arbitrary"),
        name="outproj",
    )(merged, w_out, x, mod.arr)


def _mlp_kernel(x_ref, sc_ref, sh_ref, gt_ref, g_ref, w1_ref, w2_ref, o_ref, h_ref, *, rc, nf):
    f = pl.program_id(1)

    @pl.when(f == 0)
    def _():
        _modnorm_to(h_ref, x_ref, sc_ref, sh_ref, g_ref, rc)

    a = _dot(h_ref[...], w1_ref[...].astype(BF16))
    a = jnp.square(jnp.maximum(a, 0.0)).astype(BF16)
    part = _dot(a, w2_ref[...].astype(BF16))

    @pl.when(f == 0)
    def _():
        o_ref[...] = part

    @pl.when(f > 0)
    def _():
        o_ref[...] += part

    @pl.when(f == nf - 1)
    def _():
        def body(cidx, carry):
            r = pl.multiple_of(cidx * rc, rc)
            rows = pl.ds(r, rc)
            o_ref[rows, :] = x_ref[rows, :] + _mod_rows(gt_ref, r, rc) * o_ref[rows, :]
            return carry

        lax.fori_loop(0, x_ref.shape[0] // rc, body, 0)


def _mlp_call(x, mod, g_norm, w1, w2, layer, tm, tf):
    m, d = x.shape
    depth, _, dff = w1.shape
    nf = dff // tf
    return pl.pallas_call(
        functools.partial(_mlp_kernel, rc=_row_chunk(tm), nf=nf),
        grid=(m // tm, nf),
        in_specs=[
            pl.BlockSpec((tm, d), lambda i, f: (i, 0)),
            mod.spec(4), mod.spec(3), mod.spec(5),
            pl.BlockSpec((None, 1, d), lambda i, f: (layer, 0, 0)),
            pl.BlockSpec((None, d, tf), lambda i, f: (layer, 0, f)),
            pl.BlockSpec((None, tf, d), lambda i, f: (layer, f, 0)),
        ],
        out_specs=pl.BlockSpec((tm, d), lambda i, f: (i, 0)),
        out_shape=jax.ShapeDtypeStruct((m, d), F32),
        scratch_shapes=[pltpu.VMEM((tm, d), BF16)],
        compiler_params=_params("arbitrary", "arbitrary"),
        name="mlp",
    )(x, mod.arr, mod.arr, mod.arr, g_norm.reshape(depth, 1, d), w1, w2)


def _layer(x, mod, p, layer, nb_batch, t_len, tm, tf, cache):
    z3 = _inproj_call(x, mod, p["g_norm1"], p["w_in"], layer, tm)
    if cache is None:
        attn_cache = conv_init = pool_init = ret_init = None
        pos0 = 0
    else:
        attn_cache, conv_init, pool_init, ret_init = cache
        pos0 = PAST_LEN
    y_att, k_norm = _attn_call(z3, p["attn_sinks"], p["gq_t"], p["gk_t"], layer, nb_batch, t_len, attn_cache)
    y_conv, conv_tail = _conv_call(z3, p["w_dw"], p["b_dw"], p["g_conv_ln"], p["b_conv_ln"], layer, nb_batch, t_len,
                                   conv_init)
    y_pool = _pool_call(z3, p["w_pool"], p["s_pool"], layer, nb_batch, t_len, pos0, pool_init)
    y_ret, s_new = _ret_call(z3, p["g_ret_norm"], layer, nb_batch, t_len, ret_init)
    merged = _merge_call((y_att, y_conv, y_pool, y_ret), z3, p["w_br"], layer, tm)
    x = _outproj_call(merged, x, mod, p["w_out"], layer, tm)
    x = _mlp_call(x, mod, p["g_norm2"], p["w_mlp1"], p["w_mlp2"], layer, tm, tf)
    return x, (z3, k_norm, conv_tail, s_new)


def kernel(x_prompt, x_sample, c_prompt, c_sample, cache_attn_k, cache_attn_v, state_conv, state_pool, state_ret,
           w_ada, b_ada, g_norm1, g_norm2, w_in, g_qnorm, g_knorm, attn_sinks, w_dw, b_dw, g_conv_ln, b_conv_ln,
           w_pool, s_pool, g_ret_norm, w_br, w_out, w_mlp1, w_mlp2):
    nb, t_len, d = x_prompt.shape
    nsb, st_len, _ = x_sample.shape
    depth = w_ada.shape[0]
    mp, ms = nb * t_len, nsb * st_len

    n_c = nb + nsb
    r_pad = -(-n_c // 16) * 16
    c_all = jnp.concatenate([c_prompt, c_sample, jnp.zeros((r_pad - n_c, d), F32)], axis=0)
    ada = _ada_call(c_all, w_ada, b_ada)
    ada_p = ada[:, :nb].reshape(depth, nb, 1, 6 * d)
    ada_s = jnp.repeat(ada[:, nb:n_c], st_len, axis=1)

    p = dict(g_norm1=g_norm1, g_norm2=g_norm2, w_in=w_in, attn_sinks=attn_sinks, w_dw=w_dw, b_dw=b_dw,
             g_conv_ln=g_conv_ln, b_conv_ln=b_conv_ln, w_pool=w_pool, s_pool=s_pool, g_ret_norm=g_ret_norm,
             w_br=w_br, w_out=w_out, w_mlp1=w_mlp1, w_mlp2=w_mlp2,
             gq_t=jnp.tile(g_qnorm, (1, SLAB // HEAD_DIM)).reshape(depth, 1, SLAB),
             gk_t=jnp.tile(g_knorm, (1, N_KV)).reshape(depth, 1, KV_W))

    tm_p = 1024 if t_len % 1024 == 0 else t_len
    tm_s = ms
    tf = 256
    cache_k2 = cache_attn_k.reshape(depth, nsb, WINDOW, KV_W)
    cache_v2 = cache_attn_v.reshape(depth, nsb, WINDOW, KV_W)
    conv_init = jnp.pad(state_conv, ((0, 0), (0, 0), (CONV_HALO - (CONV_K - 1), 0), (0, 0)))
    pool_init = jnp.pad(state_pool, ((0, 0), (0, 0), (POOL_HALO - POOL_PAD, 0), (0, 0)))
    ret_init = state_ret.reshape(depth, nsb, RET_HEADS // 2, 2 * RET_DK, RET_DV)

    xp = x_prompt.reshape(mp, d)
    xs = x_sample.reshape(ms, d)
    st_p = [[] for _ in range(5)]
    st_s = [[] for _ in range(5)]
    for l in range(depth):
        mod_p = _Mod(ada_p, l, False, tm_p, t_len // tm_p)
        xp, (z3, k_norm, conv_tail, s_new) = _layer(xp, mod_p, p, l, nb, t_len, tm_p, tf, None)
        st_p[0].append(k_norm.reshape(nb, t_len, N_KV, HEAD_DIM)[:, -WINDOW:])
        st_p[1].append(z3[SLAB_KV].reshape(nb, t_len, SLAB)[:, -WINDOW:, KV_W:].reshape(nb, WINDOW, N_KV, HEAD_DIM))
        st_p[2].append(conv_tail[:, CONV_HALO - (CONV_K - 1):])
        pool_u = z3[SLAB_POOL:SLAB_POOL + 2].reshape(2, nb, t_len, SLAB)[:, :, -POOL_PAD:]
        st_p[3].append(jnp.moveaxis(pool_u, 0, 2).reshape(nb, POOL_PAD, BR_W))
        st_p[4].append(s_new.reshape(nb, RET_HEADS, RET_DK, RET_DV))

        mod_s = _Mod(ada_s, l, True, tm_s, 1)
        cache = ((cache_k2, cache_v2), conv_init, pool_init, ret_init)
        xs, (z3, k_norm, conv_tail, s_new) = _layer(xs, mod_s, p, l, nsb, st_len, tm_s, tf, cache)
        k_new = k_norm.reshape(nsb, st_len, N_KV, HEAD_DIM)
        v_new = z3[SLAB_KV].reshape(nsb, st_len, SLAB)[:, :, KV_W:].reshape(nsb, st_len, N_KV, HEAD_DIM)
        st_s[0].append(jnp.concatenate([cache_attn_k[l], k_new], axis=1)[:, -WINDOW:])
        st_s[1].append(jnp.concatenate([cache_attn_v[l], v_new], axis=1)[:, -WINDOW:])
        st_s[2].append(conv_tail[:, CONV_HALO - (CONV_K - 1):])
        pool_u = jnp.moveaxis(z3[SLAB_POOL:SLAB_POOL + 2].reshape(2, nsb, st_len, SLAB), 0, 2).reshape(nsb, st_len, BR_W)
        st_s[3].append(jnp.concatenate([state_pool[l], pool_u], axis=1)[:, -POOL_PAD:])
        st_s[4].append(s_new.reshape(nsb, RET_HEADS, RET_DK, RET_DV))

    return (xp.reshape(nb, t_len, d), xs.reshape(nsb, st_len, d),
            *[jnp.stack(a) for a in st_p], *[jnp.stack(a) for a in st_s])
```

```python
import functools

import numpy as np
import jax
import jax.numpy as jnp
from jax import lax
from jax.experimental import pallas as pl
from jax.experimental.pallas import tpu as pltpu

F32 = jnp.float32
BF16 = jnp.bfloat16

D_MODEL = 2048
PAST_LEN = 16384
N_HEADS = 16
HEAD_DIM = 64
N_KV = 4
WINDOW = 128
BR_W = 1024
CONV_K = 31
POOL_WINDOWS = (2, 4, 8, 16)
POOL_G = 256
POOL_PAD = 15
RET_HEADS = 8
RET_DK = 64
RET_DV = 128
RET_CHUNK = 128
N_BR = 4
D_FF = 4 * D_MODEL
MLP_TF = 512
EPS = 1e-6
KV_W = N_KV * HEAD_DIM
N_IN = 15872

SLAB = 512
N_SLAB = N_IN // SLAB
SLAB_Q, SLAB_KV, SLAB_CONV, SLAB_POOL, SLAB_RQ, SLAB_RK, SLAB_RV, SLAB_RG, SLAB_GATE = 0, 2, 3, 7, 9, 10, 11, 13, 15

VMEM_LIMIT_BYTES = 56 * 1024 * 1024
LANES = 128
SUBLANES = 8
NEG_BIG = -1e30
CONV_HALO = 32
POOL_HALO = 16


def _params(*sem):
    return pltpu.CompilerParams(dimension_semantics=sem, vmem_limit_bytes=VMEM_LIMIT_BYTES)


def _nt_dot(a, b):
    return lax.dot_general(a, b, (((1,), (1,)), ((), ())), preferred_element_type=F32)


def _tn_dot(a, b):
    return lax.dot_general(a, b, (((0,), (0,)), ((), ())), preferred_element_type=F32)


def _dot(a, b):
    return jnp.dot(a, b, preferred_element_type=F32)


def _silu(x):
    return x * jax.nn.sigmoid(x)


def _ada_kernel(c_ref, w_ref, b_ref, o_ref):
    s = _silu(c_ref[...]).astype(BF16)
    o_ref[...] = _dot(s, w_ref[...].astype(BF16)) + b_ref[...]


def _ada_call(c_all, w_ada, b_ada):
    depth, d, n = w_ada.shape
    r = c_all.shape[0]
    tn = 1024
    return pl.pallas_call(
        _ada_kernel,
        grid=(depth, n // tn),
        in_specs=[
            pl.BlockSpec((r, d), lambda l, j: (0, 0)),
            pl.BlockSpec((None, d, tn), lambda l, j: (l, 0, j)),
            pl.BlockSpec((None, 1, tn), lambda l, j: (l, 0, j)),
        ],
        out_specs=pl.BlockSpec((None, r, tn), lambda l, j: (l, 0, j)),
        out_shape=jax.ShapeDtypeStruct((depth, r, n), F32),
        compiler_params=_params("arbitrary", "arbitrary"),
        name="ada",
    )(c_all, w_ada, b_ada.reshape(depth, 1, n))


class _Mod:
    def __init__(self, arr, layer, per_row, t_len):
        self.arr, self.l, self.per_row, self.t_len = arr, layer, per_row, t_len

    def spec(self, k, tm, width=D_MODEL, col=None):
        l, tpb, nb = self.l, max(self.t_len // tm, 1), D_MODEL // width

        def cidx(idx):
            return k * nb + (idx[col] if col is not None else 0)

        if self.per_row:
            return pl.BlockSpec((None, tm, width), lambda *idx: (l, idx[0], cidx(idx)))
        return pl.BlockSpec((None, None, 1, width), lambda *idx: (l, idx[0] // tpb, 0, cidx(idx)))


def _mod_rows(ref, r, rc):
    return ref[...] if ref.shape[0] == 1 else ref[pl.ds(r, rc), :]


def _modnorm_to(h_ref, x_ref, sc_ref, sh_ref, g_ref, rc):
    g = g_ref[...]

    def body(c, carry):
        r = pl.multiple_of(c * rc, rc)
        x = x_ref[pl.ds(r, rc), :]
        ms = jnp.mean(x * x, axis=-1, keepdims=True)
        y = x * lax.rsqrt(ms + EPS) * g
        h = y * (1.0 + _mod_rows(sc_ref, r, rc)) + _mod_rows(sh_ref, r, rc)
        h_ref[pl.ds(r, rc), :] = h.astype(h_ref.dtype)
        return carry

    lax.fori_loop(0, x_ref.shape[0] // rc, body, 0)


def _row_chunk(tm):
    return 128 if tm % 128 == 0 else tm


def _inproj_kernel(x_ref, sc_ref, sh_ref, g_ref, w_ref, o_ref, h_ref, *, rc, gate):
    @pl.when(pl.program_id(1) == 0)
    def _():
        _modnorm_to(h_ref, x_ref, sc_ref, sh_ref, g_ref, rc)

    z = _dot(h_ref[...], w_ref[...])
    o_ref[...] = jax.nn.sigmoid(z).astype(o_ref.dtype) if gate else z


def _inproj_call(x, mod, g_norm, w_in_s, layer, tm, slab0, nslab, gate):
    m, d = x.shape
    depth = w_in_s.shape[0]
    return pl.pallas_call(
        functools.partial(_inproj_kernel, rc=_row_chunk(tm), gate=gate),
        grid=(m // tm, nslab),
        in_specs=[
            pl.BlockSpec((tm, d), lambda i, j: (i, 0), pipeline_mode=pl.Buffered(1)),
            mod.spec(1, tm), mod.spec(0, tm),
            pl.BlockSpec((None, 1, d), lambda i, j: (layer, 0, 0)),
            pl.BlockSpec((None, None, d, SLAB), lambda i, j: (layer, slab0 + j, 0, 0)),
        ],
        out_specs=pl.BlockSpec((None, tm, SLAB), lambda i, j: (j, i, 0)),
        out_shape=jax.ShapeDtypeStruct((nslab, m, SLAB), BF16 if gate else F32),
        scratch_shapes=[pltpu.VMEM((tm, d), BF16)],
        compiler_params=_params("arbitrary", "arbitrary"),
        name="ingate" if gate else "inproj",
    )(x, mod.arr, mod.arr, g_norm.reshape(depth, 1, d), w_in_s)


def _group_sum_matrix(width, group):
    idx = np.arange(width) // group
    return jnp.asarray((idx[:, None] == idx[None, :]).astype(np.float32), dtype=BF16)


def _attn_kernel(*refs, layer, tq, has_init):
    if has_init:
        sinks_ref, q_ref, kvc_ref, kinit_ref, vinit_ref, gq_ref, gk_ref, gm512_ref, gm256_ref, sd_ref, o_ref, kn_ref = refs
    else:
        sinks_ref, q_ref, kvc_ref, kvp_ref, gq_ref, gk_ref, gm512_ref, gm256_ref, sd_ref, o_ref, kn_ref = refs
    n = pl.program_id(1)

    def qk_norm(x, gmat, g):
        x2 = x * x
        hi = x2.astype(BF16)
        lo = (x2 - hi.astype(F32)).astype(BF16)
        ss = _dot(hi, gmat) + _dot(lo, gmat)
        return x * lax.rsqrt(ss * (1.0 / HEAD_DIM) + EPS) * g

    gm256, gk = gm256_ref[...], gk_ref[...]
    kvc = kvc_ref[...]
    kc = qk_norm(kvc[:, :KV_W], gm256, gk)
    kn_ref[...] = kc
    vc = kvc[:, KV_W:]
    if has_init:
        kp, vp = kinit_ref[...], vinit_ref[...]
    else:
        kvp = kvp_ref[...]
        kp, vp = qk_norm(kvp[:, :KV_W], gm256, gk), kvp[:, KV_W:]
    if tq < WINDOW:
        pad = jnp.zeros((WINDOW - tq, KV_W), F32)
        kc = jnp.concatenate([kc, pad], axis=0)
        vc = jnp.concatenate([vc, pad], axis=0)
    kall = jnp.concatenate([kp, kc], axis=0)
    vall = jnp.concatenate([vp, vc], axis=0)

    nk = 2 * WINDOW
    ii = lax.broadcasted_iota(jnp.int32, (tq, nk), 0)
    jj = lax.broadcasted_iota(jnp.int32, (tq, nk), 1)
    dist = WINDOW + ii - jj
    valid = (dist >= 0) & (dist <= WINDOW)
    if not has_init:
        valid = valid & (jj >= jnp.where(n > 0, 0, WINDOW))
    distm = jnp.where(valid, dist.astype(F32), -NEG_BIG)

    gm512, gq = gm512_ref[...], gq_ref[...] * (HEAD_DIM ** -0.5)
    qn = [qk_norm(q_ref[s], gm512, gq) for s in range(2)]
    lane_k = lax.broadcasted_iota(jnp.int32, (nk, LANES), 1) < HEAD_DIM
    lane_q = lax.broadcasted_iota(jnp.int32, (tq, LANES), 1) < HEAD_DIM
    sd = sd_ref[...]

    def two_copies(a, upper):
        if upper:
            bot = jnp.where(lane_k, 0.0, a)
            top = pltpu.roll(bot, HEAD_DIM, 1)
        else:
            top = jnp.where(lane_k, a, 0.0)
            bot = pltpu.roll(top, HEAD_DIM, 1)
        return jnp.concatenate([top, bot], axis=0).astype(BF16)

    kds, vds = [], []
    for kv in range(N_KV):
        cs = slice(LANES * (kv // 2), LANES * (kv // 2) + LANES)
        kds.append(two_copies(kall[:, cs], bool(kv % 2)))
        vds.append(two_copies(vall[:, cs], bool(kv % 2)))
    npair = N_HEADS // 2
    scores = []
    for p in range(npair):
        off = LANES * (p % 4)
        qp = qn[p // 4][:, off:off + LANES].astype(BF16)
        scores.append(_nt_dot(qp, kds[p // 2]))
    probs, sinkw = [], []
    for p in range(npair):
        es, sk = [], []
        for hh in range(2):
            h = 2 * p + hh
            slope = 2.0 ** (-8.0 * (h + 1) / N_HEADS)
            sink = sinks_ref[layer, h]
            sh = scores[p][:, nk * hh:nk * hh + nk] - slope * distm
            mx = jnp.maximum(jnp.max(sh, axis=-1, keepdims=True), sink)
            es.append(jnp.exp(sh - mx))
            sk.append(jnp.exp(sink - mx))
        probs.append(jnp.concatenate(es, axis=1).astype(BF16))
        sinkw.append(jnp.where(lane_q, sk[0], sk[1]))
    for p in range(npair):
        num = _dot(probs[p], vds[p // 2])
        den = _dot(probs[p], sd) + sinkw[p]
        o_ref[:, LANES * p:LANES * p + LANES] = (num / den).astype(o_ref.dtype)


def _attn_call(z3, sinks, gq_t, gk_t, layer, nb_batch, t_len, cache=None):
    m = z3.shape[1]
    has_init = cache is not None
    tq = WINDOW if t_len % WINDOW == 0 else t_len
    nb = t_len // tq
    assert nb == 1 or not has_init
    depth = gq_t.shape[0]
    gm512 = _group_sum_matrix(SLAB, HEAD_DIM)
    gm256 = _group_sum_matrix(KV_W, HEAD_DIM)
    sd_np = np.zeros((4 * WINDOW, LANES), np.float32)
    sd_np[:2 * WINDOW, :HEAD_DIM] = 1.0
    sd_np[2 * WINDOW:, HEAD_DIM:] = 1.0
    sd = jnp.asarray(sd_np, dtype=BF16)

    in_specs = [
        pl.BlockSpec(memory_space=pltpu.SMEM),
        pl.BlockSpec((2, tq, SLAB), lambda b, n: (0, b * nb + n, 0)),
        pl.BlockSpec((None, tq, SLAB), lambda b, n: (SLAB_KV, b * nb + n, 0)),
    ]
    args = [sinks, z3, z3]
    if has_init:
        in_specs += [pl.BlockSpec((None, None, WINDOW, KV_W), lambda b, n: (layer, b, 0, 0))] * 2
        args += [cache[0], cache[1]]
    else:
        in_specs += [pl.BlockSpec((None, WINDOW, SLAB), lambda b, n: (SLAB_KV, jnp.maximum(b * nb + n - 1, 0), 0))]
        args += [z3]
    in_specs += [
        pl.BlockSpec((None, 1, SLAB), lambda b, n: (layer, 0, 0)),
        pl.BlockSpec((None, 1, KV_W), lambda b, n: (layer, 0, 0)),
        pl.BlockSpec((SLAB, SLAB), lambda b, n: (0, 0)),
        pl.BlockSpec((KV_W, KV_W), lambda b, n: (0, 0)),
        pl.BlockSpec((4 * WINDOW, LANES), lambda b, n: (0, 0)),
    ]
    args += [gq_t, gk_t, gm512, gm256, sd]
    out_dtype = F32 if has_init else BF16
    return pl.pallas_call(
        functools.partial(_attn_kernel, layer=layer, tq=tq, has_init=has_init),
        grid=(nb_batch, nb),
        in_specs=in_specs,
        out_specs=[
            pl.BlockSpec((tq, BR_W), lambda b, n: (b * nb + n, 0)),
            pl.BlockSpec((tq, KV_W), lambda b, n: (b * nb + n, 0)),
        ],
        out_shape=[jax.ShapeDtypeStruct((m, BR_W), out_dtype), jax.ShapeDtypeStruct((m, KV_W), F32)],
        compiler_params=_params("arbitrary", "arbitrary"),
        name="attn",
    )(*args)


def _conv_kernel(*refs, tt, nt, has_init):
    if has_init:
        l0, l1, g0, g1, init_ref, w_ref, b_ref, gl_ref, bl_ref, o_ref, new_ref, ext, ybuf, shifted = refs
    else:
        l0, l1, g0, g1, w_ref, b_ref, gl_ref, bl_ref, o_ref, new_ref, ext, ybuf, shifted = refs
    t = pl.program_id(1)

    @pl.when(t == 0)
    def _():
        ext[0:CONV_HALO, :] = init_ref[...] if has_init else jnp.zeros((CONV_HALO, BR_W), F32)

    if nt > 1:
        @pl.when(t > 0)
        def _():
            ext[0:CONV_HALO, :] = ext[tt:tt + CONV_HALO, :]

    for cb, (lr, gr) in enumerate(((l0, g0), (l1, g1))):
        ext[CONV_HALO:CONV_HALO + tt, SLAB * cb:SLAB * cb + SLAB] = lr[...] * jax.nn.sigmoid(gr[...])
    new_ref[...] = ext[tt:tt + CONV_HALO, :]

    rs = min(tt, 32)
    base = CONV_HALO - (CONV_K - 1)
    nsh = shifted.shape[1]
    for c in range(BR_W // LANES):
        cs = slice(LANES * c, LANES * c + LANES)
        for r in range(1, SUBLANES):
            shifted[r - 1] = ext[r:r + nsh, cs]

        def taps(i, carry, cs=cs):
            r0 = pl.multiple_of(i * rs, rs)
            acc = jnp.broadcast_to(b_ref[:, cs], (rs, LANES))
            for r in range(SUBLANES):
                offs = [(base + k) // SUBLANES for k in range(CONV_K) if (base + k) % SUBLANES == r]
                rows = pl.ds(r0 + SUBLANES * offs[0], rs + SUBLANES * (offs[-1] - offs[0]))
                win = ext[rows, cs] if r == 0 else shifted[r - 1, rows, :]
                for a in offs:
                    k = SUBLANES * a + r - base
                    d = SUBLANES * (a - offs[0])
                    acc = acc + w_ref[k:k + 1, cs] * win[d:d + rs]
            ybuf[pl.ds(r0, rs), cs] = acc
            return carry

        lax.fori_loop(0, tt // rs, taps, 0)

    def norm_act(i, carry):
        rows = pl.ds(pl.multiple_of(i * rs, rs), rs)
        y = ybuf[rows, :]
        yc = y - jnp.mean(y, axis=-1, keepdims=True)
        var = jnp.mean(yc * yc, axis=-1, keepdims=True)
        yn = yc * lax.rsqrt(var + EPS) * gl_ref[...] + bl_ref[...]
        o_ref[rows, :] = _silu(yn).astype(o_ref.dtype)
        return carry

    lax.fori_loop(0, tt // rs, norm_act, 0, unroll=min(4, tt // rs))


def _conv_call(z3, w_dw, b_dw, g_ln, b_ln, layer, nb_batch, t_len, init=None):
    m = z3.shape[1]
    has_init = init is not None
    tt = 256 if t_len % 256 == 0 else t_len
    nt = t_len // tt
    assert nt == 1 or tt >= CONV_HALO
    depth = w_dw.shape[0]

    def slab(j):
        return pl.BlockSpec((None, tt, SLAB), lambda b, t: (j, b * nt + t, 0))

    def vec():
        return pl.BlockSpec((None, 1, BR_W), lambda b, t: (layer, 0, 0))

    in_specs = [slab(SLAB_CONV), slab(SLAB_CONV + 1), slab(SLAB_CONV + 2), slab(SLAB_CONV + 3)]
    args = [z3, z3, z3, z3]
    if has_init:
        in_specs.append(pl.BlockSpec((None, None, CONV_HALO, BR_W), lambda b, t: (layer, b, 0, 0)))
        args.append(init)
    in_specs += [pl.BlockSpec((None, CONV_K, BR_W), lambda b, t: (layer, 0, 0)), vec(), vec(), vec()]
    args += [w_dw, b_dw.reshape(depth, 1, BR_W), g_ln.reshape(depth, 1, BR_W), b_ln.reshape(depth, 1, BR_W)]
    return pl.pallas_call(
        functools.partial(_conv_kernel, tt=tt, nt=nt, has_init=has_init),
        grid=(nb_batch, nt),
        in_specs=in_specs,
        out_specs=[
            pl.BlockSpec((tt, BR_W), lambda b, t: (b * nt + t, 0)),
            pl.BlockSpec((None, CONV_HALO, BR_W), lambda b, t: (b, 0, 0)),
        ],
        out_shape=[
            jax.ShapeDtypeStruct((m, BR_W), F32 if has_init else BF16),
            jax.ShapeDtypeStruct((nb_batch, CONV_HALO, BR_W), F32),
        ],
        scratch_shapes=[pltpu.VMEM((CONV_HALO + tt, BR_W), F32), pltpu.VMEM((tt, BR_W), F32),
                        pltpu.VMEM((SUBLANES - 1, CONV_HALO + tt - SUBLANES, LANES), F32)],
        compiler_params=_params("arbitrary", "arbitrary"),
        name="conv",
    )(*args)


def _pool_kernel(*refs, tt, nt, has_init, pos0):
    if has_init:
        u0, u1, init_ref, w_ref, s_ref, o_ref, ext = refs
    else:
        u0, u1, w_ref, s_ref, o_ref, ext = refs
    t = pl.program_id(1)

    @pl.when(t == 0)
    def _():
        ext[0:POOL_HALO, :] = init_ref[...] if has_init else jnp.zeros((POOL_HALO, BR_W), F32)

    if nt > 1:
        @pl.when(t > 0)
        def _():
            ext[0:POOL_HALO, :] = ext[tt:tt + POOL_HALO, :]

    ext[POOL_HALO:POOL_HALO + tt, 0:SLAB] = u0[...]
    ext[POOL_HALO:POOL_HALO + tt, SLAB:2 * SLAB] = u1[...]
    pos = pos0 + t * tt + lax.broadcasted_iota(jnp.int32, (tt, 1), 0)
    for g, w in enumerate(POOL_WINDOWS):
        cs = slice(POOL_G * g, POOL_G * g + POOL_G)
        cur = ext[POOL_HALO:POOL_HALO + tt, cs]
        wsum = cur
        for s in range(1, w):
            wsum = wsum + ext[POOL_HALO - s:POOL_HALO - s + tt, cs]
        cnt = jnp.minimum(pos + 1, w).astype(F32)
        zg = wsum / cnt - cur
        y = _dot(zg.astype(BF16), w_ref[g].astype(BF16)) * s_ref[:, cs]
        o_ref[:, cs] = y.astype(o_ref.dtype)


def _pool_call(z3, w_pool, s_pool, layer, nb_batch, t_len, pos0, init=None):
    m = z3.shape[1]
    has_init = init is not None
    tt = 256 if t_len % 256 == 0 else t_len
    nt = t_len // tt
    assert nt == 1 or tt >= POOL_HALO
    depth = w_pool.shape[0]

    def slab(j):
        return pl.BlockSpec((None, tt, SLAB), lambda b, t: (j, b * nt + t, 0))

    in_specs = [slab(SLAB_POOL), slab(SLAB_POOL + 1)]
    args = [z3, z3]
    if has_init:
        in_specs.append(pl.BlockSpec((None, None, POOL_HALO, BR_W), lambda b, t: (layer, b, 0, 0)))
        args.append(init)
    in_specs += [
        pl.BlockSpec((None, len(POOL_WINDOWS), POOL_G, POOL_G), lambda b, t: (layer, 0, 0, 0)),
        pl.BlockSpec((None, 1, BR_W), lambda b, t: (layer, 0, 0)),
    ]
    args += [w_pool, s_pool.reshape(depth, 1, BR_W)]
    return pl.pallas_call(
        functools.partial(_pool_kernel, tt=tt, nt=nt, has_init=has_init, pos0=pos0),
        grid=(nb_batch, nt),
        in_specs=in_specs,
        out_specs=pl.BlockSpec((tt, BR_W), lambda b, t: (b * nt + t, 0)),
        out_shape=jax.ShapeDtypeStruct((m, BR_W), F32 if has_init else BF16),
        scratch_shapes=[pltpu.VMEM((POOL_HALO + tt, BR_W), F32)],
        compiler_params=_params("arbitrary", "arbitrary"),
        name="pool",
    )(*args)


def _ret_constants(c, cp):
    lg = np.log1p(-np.exp2(-5.0 - np.arange(RET_HEADS, dtype=np.float64)))
    i = np.arange(c, dtype=np.float64)
    diff = i[:, None] - i[None, :]
    decay = np.where(diff >= 0, np.exp(lg[:, None, None] * np.maximum(diff, 0.0)), 0.0)
    dec = np.zeros((RET_HEADS // 2, c, 2 * cp))
    for h in range(RET_HEADS):
        dec[h // 2, :, (h % 2) * cp:(h % 2) * cp + c] = decay[h]
    kfac = np.repeat(np.exp(lg[None, :] * (c - 1 - i)[:, None]), RET_DK, axis=1) * RET_DK ** -0.5
    cfac = np.repeat(np.exp(lg[None, :] * (i + 1)[:, None]), RET_DV, axis=1)
    gch = np.repeat(np.exp(lg * c), RET_DK).reshape(RET_HEADS // 2, 2 * RET_DK, 1)
    gch = np.broadcast_to(gch, (RET_HEADS // 2, 2 * RET_DK, RET_DV))
    f = lambda a: jnp.asarray(np.ascontiguousarray(a), dtype=F32)
    return f(dec), f(kfac), f(cfac), f(gch)


def _ret_kernel(*refs, c, cp, has_init):
    if has_init:
        rq, rk, rv0, rv1, rg0, rg1, s0_ref, gn_ref, dec_ref, kf_ref, cf_ref, gch_ref, o_ref, s_ref = refs
    else:
        rq, rk, rv0, rv1, rg0, rg1, gn_ref, dec_ref, kf_ref, cf_ref, gch_ref, o_ref, s_ref = refs
    n = pl.program_id(1)

    @pl.when(n == 0)
    def _():
        s_ref[...] = s0_ref[...] if has_init else jnp.zeros(s_ref.shape, F32)

    lo = lax.broadcasted_iota(jnp.int32, (c, LANES), 1) < RET_DK
    q = rq[...]
    kraw = rk[...]
    k = kraw * (RET_DK ** -0.5)
    kdec = kraw * kf_ref[...]
    rvs, rgs = (rv0, rv1), (rg0, rg1)

    def stack_heads(a):
        a0, a1 = jnp.where(lo, a, 0.0), jnp.where(lo, 0.0, a)
        if cp > c:
            z = jnp.zeros((cp - c, LANES), F32)
            return jnp.concatenate([a0, z, a1, z], axis=0)
        return jnp.concatenate([a0, a1], axis=0)

    for p in range(RET_HEADS // 2):
        cs = slice(LANES * p, LANES * p + LANES)
        qp = q[:, cs]
        s = _nt_dot(qp.astype(BF16), stack_heads(k[:, cs]).astype(BF16)) * dec_ref[p]
        hs = (2 * p, 2 * p + 1)
        vs = [rvs[h // 4][:, LANES * (h % 4):LANES * (h % 4) + LANES] for h in hs]
        zc = jnp.zeros((c, LANES), F32)
        rows0 = jnp.concatenate([vs[0], zc], axis=1)
        rows1 = jnp.concatenate([zc, vs[1]], axis=1)
        if cp > c:
            zp = jnp.zeros((cp - c, 2 * LANES), F32)
            zq = jnp.zeros((cp - c, LANES), F32)
            vbd = jnp.concatenate([rows0, zp, rows1, zp], axis=0)
            vst = jnp.concatenate([vs[0], zq, vs[1], zq], axis=0)
        else:
            vbd = jnp.concatenate([rows0, rows1], axis=0)
            vst = jnp.concatenate([vs[0], vs[1]], axis=0)
        inner = _dot(s.astype(BF16), vbd.astype(BF16))
        sprev = s_ref[p]
        qst = jnp.concatenate([jnp.where(lo, qp, 0.0), jnp.where(lo, 0.0, qp)], axis=0)
        cross = _dot(qst.astype(BF16), sprev.astype(BF16))
        upd = _tn_dot(stack_heads(kdec[:, cs]).astype(BF16), vst.astype(BF16))
        s_ref[p] = gch_ref[p] * sprev + upd
        for hh, h in enumerate(hs):
            hc = slice(LANES * h, LANES * h + LANES)
            o = inner[:, LANES * hh:LANES * hh + LANES] + cross[c * hh:c * hh + c, :] * cf_ref[:, hc]
            oc = o - jnp.mean(o, axis=-1, keepdims=True)
            var = jnp.mean(oc * oc, axis=-1, keepdims=True)
            gate = rgs[h // 4][:, LANES * (h % 4):LANES * (h % 4) + LANES]
            y = oc * lax.rsqrt(var + EPS) * gn_ref[:, hc] * _silu(gate)
            o_ref[:, hc] = y.astype(o_ref.dtype)


def _ret_call(z3, g_ret, layer, nb_batch, t_len, init=None):
    m = z3.shape[1]
    has_init = init is not None
    c = RET_CHUNK if t_len % RET_CHUNK == 0 else t_len
    cp = max(c, 64)
    nc = t_len // c
    depth = g_ret.shape[0]
    dec, kfac, cfac, gch = _ret_constants(c, cp)
    npair = RET_HEADS // 2

    def slab(j):
        return pl.BlockSpec((None, c, SLAB), lambda b, n: (j, b * nc + n, 0))

    def const(shape):
        nd = len(shape)
        return pl.BlockSpec(shape, lambda b, n: (0,) * nd)

    in_specs = [slab(SLAB_RQ), slab(SLAB_RK), slab(SLAB_RV), slab(SLAB_RV + 1), slab(SLAB_RG), slab(SLAB_RG + 1)]
    args = [z3] * 6
    if has_init:
        in_specs.append(pl.BlockSpec((None, None, npair, LANES, RET_DV), lambda b, n: (layer, b, 0, 0, 0)))
        args.append(init)
    in_specs += [pl.BlockSpec((None, 1, BR_W), lambda b, n: (layer, 0, 0)),
                 const(dec.shape), const(kfac.shape), const(cfac.shape), const(gch.shape)]
    args += [g_ret.reshape(depth, 1, BR_W), dec, kfac, cfac, gch]
    return pl.pallas_call(
        functools.partial(_ret_kernel, c=c, cp=cp, has_init=has_init),
        grid=(nb_batch, nc),
        in_specs=in_specs,
        out_specs=[
            pl.BlockSpec((c, BR_W), lambda b, n: (b * nc + n, 0)),
            pl.BlockSpec((None, npair, LANES, RET_DV), lambda b, n: (b, 0, 0, 0)),
        ],
        out_shape=[
            jax.ShapeDtypeStruct((m, BR_W), F32 if has_init else BF16),
            jax.ShapeDtypeStruct((nb_batch, npair, LANES, RET_DV), F32),
        ],
        compiler_params=_params("arbitrary", "arbitrary"),
        name="retention",
    )(*args)


def _merge_kernel(y0, y1, y2, y3, gate_ref, w_ref, o_ref, acc_ref):
    r, c = pl.program_id(1), pl.program_id(2)
    for k, y_ref in enumerate((y0, y1, y2, y3)):
        @pl.when(r == k)
        def _(k=k, y_ref=y_ref):
            val = gate_ref[...].astype(F32) * _dot(y_ref[...].astype(BF16), w_ref[...])
            if k == 0:
                acc_ref[c] = val
            elif k < N_BR - 1:
                acc_ref[c] += val
            else:
                o_ref[...] = (acc_ref[c] + val).astype(o_ref.dtype)


def _merge_call(ys, gates, w_br_s, layer, tm):
    m = gates.shape[1]
    nc = D_MODEL // SLAB
    y_spec = pl.BlockSpec((tm, BR_W), lambda i, r, c: (i, 0))
    return pl.pallas_call(
        _merge_kernel,
        grid=(m // tm, N_BR, nc),
        in_specs=[y_spec] * N_BR + [
            pl.BlockSpec((None, tm, SLAB), lambda i, r, c: (nc * r + c, i, 0)),
            pl.BlockSpec((None, None, None, BR_W, SLAB), lambda i, r, c: (layer, r, c, 0, 0)),
        ],
        out_specs=pl.BlockSpec((tm, SLAB), lambda i, r, c: (i, jnp.where(r == N_BR - 1, c, 0))),
        out_shape=jax.ShapeDtypeStruct((m, D_MODEL), BF16),
        scratch_shapes=[pltpu.VMEM((nc, tm, SLAB), F32)],
        compiler_params=_params("arbitrary", "arbitrary", "arbitrary"),
        name="merge",
    )(*ys, gates, w_br_s)


def _outproj_kernel(m_ref, w_ref, x_ref, gt_ref, o_ref):
    o_ref[...] = x_ref[...] + gt_ref[...] * _dot(m_ref[...], w_ref[...])


def _outproj_call(merged, x, mod, w_out_s, layer, tm):
    m, d = x.shape
    return pl.pallas_call(
        _outproj_kernel,
        grid=(m // tm, d // SLAB),
        in_specs=[
            pl.BlockSpec((tm, d), lambda i, c: (i, 0)),
            pl.BlockSpec((None, None, d, SLAB), lambda i, c: (layer, c, 0, 0)),
            pl.BlockSpec((tm, SLAB), lambda i, c: (i, c)),
            mod.spec(2, tm, width=SLAB, col=1),
        ],
        out_specs=pl.BlockSpec((tm, SLAB), lambda i, c: (i, c)),
        out_shape=jax.ShapeDtypeStruct((m, d), F32),
        compiler_params=_params("arbitrary", "arbitrary"),
        name="outproj",
    )(merged, w_out_s, x, mod.arr)


def _mlp_kernel(x_ref, sc_ref, sh_ref, gt_ref, g_ref, w1_ref, w2_ref, o_ref, h_ref, *, rc, nf):
    f = pl.program_id(1)

    @pl.when(f == 0)
    def _():
        _modnorm_to(h_ref, x_ref, sc_ref, sh_ref, g_ref, rc)
        o_ref[...] = jnp.zeros(o_ref.shape, F32)

    a = _dot(h_ref[...], w1_ref[...])
    a = jnp.square(jnp.maximum(a, 0.0)).astype(BF16)
    for c0 in range(0, o_ref.shape[1], SLAB):
        o_ref[:, c0:c0 + SLAB] += _dot(a, w2_ref[:, c0:c0 + SLAB])

    @pl.when(f == nf - 1)
    def _():
        def body(cidx, carry):
            r = pl.multiple_of(cidx * rc, rc)
            rows = pl.ds(r, rc)
            o_ref[rows, :] = x_ref[rows, :] + _mod_rows(gt_ref, r, rc) * o_ref[rows, :]
            return carry

        lax.fori_loop(0, x_ref.shape[0] // rc, body, 0)


def _mlp_call(x, mod, g_norm, w1_s, w2_b, layer, tm):
    m, d = x.shape
    depth, nf, _, tf = w1_s.shape
    return pl.pallas_call(
        functools.partial(_mlp_kernel, rc=_row_chunk(tm), nf=nf),
        grid=(m // tm, nf),
        in_specs=[
            pl.BlockSpec((tm, d), lambda i, f: (i, 0), pipeline_mode=pl.Buffered(1)),
            mod.spec(4, tm), mod.spec(3, tm), mod.spec(5, tm),
            pl.BlockSpec((None, 1, d), lambda i, f: (layer, 0, 0)),
            pl.BlockSpec((None, None, d, tf), lambda i, f: (layer, f, 0, 0)),
            pl.BlockSpec((None, tf, d), lambda i, f: (layer, f, 0)),
        ],
        out_specs=pl.BlockSpec((tm, d), lambda i, f: (i, 0)),
        out_shape=jax.ShapeDtypeStruct((m, d), F32),
        scratch_shapes=[pltpu.VMEM((tm, d), BF16)],
        compiler_params=_params("arbitrary", "arbitrary"),
        name="mlp",
    )(x, mod.arr, mod.arr, mod.arr, g_norm.reshape(depth, 1, d), w1_s, w2_b)


def _layer(x, mod, p, layer, nb_batch, t_len, tm_in, tm, cache):
    z3 = _inproj_call(x, mod, p["g_norm1"], p["w_in_s"], layer, tm_in, 0, SLAB_GATE, False)
    gates = _inproj_call(x, mod, p["g_norm1"], p["w_in_s"], layer, tm_in, SLAB_GATE, N_SLAB - SLAB_GATE, True)
    if cache is None:
        attn_cache = conv_init = pool_init = ret_init = None
        pos0 = 0
    else:
        attn_cache, conv_init, pool_init, ret_init = cache
        pos0 = PAST_LEN
    y_att, k_norm = _attn_call(z3, p["attn_sinks"], p["gq_t"], p["gk_t"], layer, nb_batch, t_len, attn_cache)
    y_conv, conv_tail = _conv_call(z3, p["w_dw"], p["b_dw"], p["g_conv_ln"], p["b_conv_ln"], layer, nb_batch, t_len,
                                   conv_init)
    y_pool = _pool_call(z3, p["w_pool"], p["s_pool"], layer, nb_batch, t_len, pos0, pool_init)
    y_ret, s_new = _ret_call(z3, p["g_ret_norm"], layer, nb_batch, t_len, ret_init)
    merged = _merge_call((y_att, y_conv, y_pool, y_ret), gates, p["w_br_s"], layer, tm)
    x = _outproj_call(merged, x, mod, p["w_out_s"], layer, tm)
    x = _mlp_call(x, mod, p["g_norm2"], p["w_mlp1_s"], p["w_mlp2_b"], layer, tm)
    return x, (z3, k_norm, conv_tail, s_new)


def kernel(x_prompt, x_sample, c_prompt, c_sample, cache_attn_k, cache_attn_v, state_conv, state_pool, state_ret,
           w_ada, b_ada, g_norm1, g_norm2, w_in, g_qnorm, g_knorm, attn_sinks, w_dw, b_dw, g_conv_ln, b_conv_ln,
           w_pool, s_pool, g_ret_norm, w_br, w_out, w_mlp1, w_mlp2):
    nb, t_len, d = x_prompt.shape
    nsb, st_len, _ = x_sample.shape
    depth = w_ada.shape[0]
    mp, ms = nb * t_len, nsb * st_len

    n_c = nb + nsb
    r_pad = -(-n_c // 16) * 16
    c_all = jnp.concatenate([c_prompt, c_sample, jnp.zeros((r_pad - n_c, d), F32)], axis=0)
    ada = _ada_call(c_all, w_ada, b_ada)
    ada_p = ada[:, :nb].reshape(depth, nb, 1, 6 * d)
    ada_s = jnp.repeat(ada[:, nb:n_c], st_len, axis=1)

    def col_tiles(w, tn):
        lead, (k, n) = w.shape[:-2], w.shape[-2:]
        w = w.astype(BF16).reshape(*lead, k, n // tn, tn)
        return jnp.moveaxis(w, -2, -3)

    p = dict(g_norm1=g_norm1, g_norm2=g_norm2, attn_sinks=attn_sinks, w_dw=w_dw, b_dw=b_dw,
             g_conv_ln=g_conv_ln, b_conv_ln=b_conv_ln, w_pool=w_pool, s_pool=s_pool, g_ret_norm=g_ret_norm,
             w_in_s=col_tiles(w_in, SLAB), w_br_s=col_tiles(w_br, SLAB), w_out_s=col_tiles(w_out, SLAB),
             w_mlp1_s=col_tiles(w_mlp1, MLP_TF), w_mlp2_b=w_mlp2.astype(BF16),
             gq_t=jnp.tile(g_qnorm, (1, SLAB // HEAD_DIM)).reshape(depth, 1, SLAB),
             gk_t=jnp.tile(g_knorm, (1, N_KV)).reshape(depth, 1, KV_W))

    tm_p = 1024 if t_len % 1024 == 0 else t_len
    tm_p_in = 2048 if t_len % 2048 == 0 else tm_p
    tm_s = ms
    cache_k2 = cache_attn_k.reshape(depth, nsb, WINDOW, KV_W)
    cache_v2 = cache_attn_v.reshape(depth, nsb, WINDOW, KV_W)
    conv_init = jnp.pad(state_conv, ((0, 0), (0, 0), (CONV_HALO - (CONV_K - 1), 0), (0, 0)))
    pool_init = jnp.pad(state_pool, ((0, 0), (0, 0), (POOL_HALO - POOL_PAD, 0), (0, 0)))
    ret_init = state_ret.reshape(depth, nsb, RET_HEADS // 2, 2 * RET_DK, RET_DV)

    xp = x_prompt.reshape(mp, d)
    xs = x_sample.reshape(ms, d)
    st_p = [[] for _ in range(5)]
    st_s = [[] for _ in range(5)]
    for l in range(depth):
        mod_p = _Mod(ada_p, l, False, t_len)
        xp, (z3, k_norm, conv_tail, s_new) = _layer(xp, mod_p, p, l, nb, t_len, tm_p_in, tm_p, None)
        z4 = z3.reshape(SLAB_GATE, nb, t_len, SLAB)
        st_p[0].append(k_norm.reshape(nb, t_len, KV_W)[:, -WINDOW:].reshape(nb, WINDOW, N_KV, HEAD_DIM))
        st_p[1].append(z4[SLAB_KV, :, -WINDOW:, KV_W:].reshape(nb, WINDOW, N_KV, HEAD_DIM))
        st_p[2].append(conv_tail[:, CONV_HALO - (CONV_K - 1):])
        pool_u = z4[SLAB_POOL:SLAB_POOL + 2, :, -POOL_PAD:]
        st_p[3].append(jnp.moveaxis(pool_u, 0, 2).reshape(nb, POOL_PAD, BR_W))
        st_p[4].append(s_new.reshape(nb, RET_HEADS, RET_DK, RET_DV))

        mod_s = _Mod(ada_s, l, True, st_len)
        cache = ((cache_k2, cache_v2), conv_init, pool_init, ret_init)
        xs, (z3, k_norm, conv_tail, s_new) = _layer(xs, mod_s, p, l, nsb, st_len, tm_s, tm_s, cache)
        k_new = k_norm.reshape(nsb, st_len, N_KV, HEAD_DIM)
        v_new = z3[SLAB_KV].reshape(nsb, st_len, SLAB)[:, :, KV_W:].reshape(nsb, st_len, N_KV, HEAD_DIM)
        st_s[0].append(jnp.concatenate([cache_attn_k[l], k_new], axis=1)[:, -WINDOW:])
        st_s[1].append(jnp.concatenate([cache_attn_v[l], v_new], axis=1)[:, -WINDOW:])
        st_s[2].append(conv_tail[:, CONV_HALO - (CONV_K - 1):])
        pool_u = jnp.moveaxis(z3[SLAB_POOL:SLAB_POOL + 2].reshape(2, nsb, st_len, SLAB), 0, 2).reshape(nsb, st_len, BR_W)
        st_s[3].append(jnp.concatenate([state_pool[l], pool_u], axis=1)[:, -POOL_PAD:])
        st_s[4].append(s_new.reshape(nsb, RET_HEADS, RET_DK, RET_DV))

    return (xp.reshape(nb, t_len, d), xs.reshape(nsb, st_len, d),
            *[jnp.stack(a) for a in st_p], *[jnp.stack(a) for a in st_s])
```

```python
import functools

import numpy as np
import jax
import jax.numpy as jnp
from jax import lax
from jax.experimental import pallas as pl
from jax.experimental.pallas import tpu as pltpu

F32 = jnp.float32
BF16 = jnp.bfloat16

D_MODEL = 2048
PAST_LEN = 16384
N_HEADS = 16
HEAD_DIM = 64
N_KV = 4
WINDOW = 128
BR_W = 1024
CONV_K = 31
POOL_WINDOWS = (2, 4, 8, 16)
POOL_G = 256
POOL_PAD = 15
RET_HEADS = 8
RET_DK = 64
RET_DV = 128
RET_CHUNK = 128
N_BR = 4
D_FF = 4 * D_MODEL
MLP_TF = 1024
MERGE_TN = 1024
EPS = 1e-6
KV_W = N_KV * HEAD_DIM
N_IN = 15872

SLAB = 512
N_SLAB = N_IN // SLAB
SLAB_Q, SLAB_KV, SLAB_CONV, SLAB_POOL, SLAB_RQ, SLAB_RK, SLAB_RV, SLAB_RG, SLAB_GATE = 0, 2, 3, 7, 9, 10, 11, 13, 15

VMEM_LIMIT_BYTES = 56 * 1024 * 1024
LANES = 128
SUBLANES = 8
NEG_BIG = -1e30
CONV_HALO = 32
POOL_HALO = 16


def _params(*sem):
    return pltpu.CompilerParams(dimension_semantics=sem, vmem_limit_bytes=VMEM_LIMIT_BYTES)


def _nt_dot(a, b):
    return lax.dot_general(a, b, (((1,), (1,)), ((), ())), preferred_element_type=F32)


def _tn_dot(a, b):
    return lax.dot_general(a, b, (((0,), (0,)), ((), ())), preferred_element_type=F32)


def _dot(a, b):
    return jnp.dot(a, b, preferred_element_type=F32)


def _silu(x):
    return x * jax.nn.sigmoid(x)


def _ada_kernel(c_ref, w_ref, b_ref, o_ref):
    s = _silu(c_ref[...]).astype(BF16)
    o_ref[...] = _dot(s, w_ref[...].astype(BF16)) + b_ref[...]


def _ada_call(c_all, w_ada, b_ada):
    depth, d, n = w_ada.shape
    r = c_all.shape[0]
    tn = 1024
    return pl.pallas_call(
        _ada_kernel,
        grid=(depth, n // tn),
        in_specs=[
            pl.BlockSpec((r, d), lambda l, j: (0, 0)),
            pl.BlockSpec((None, d, tn), lambda l, j: (l, 0, j)),
            pl.BlockSpec((None, 1, tn), lambda l, j: (l, 0, j)),
        ],
        out_specs=pl.BlockSpec((None, r, tn), lambda l, j: (l, 0, j)),
        out_shape=jax.ShapeDtypeStruct((depth, r, n), F32),
        compiler_params=_params("arbitrary", "arbitrary"),
        name="ada",
    )(c_all, w_ada, b_ada.reshape(depth, 1, n))


class _Mod:
    def __init__(self, arr, layer, per_row, t_len):
        self.arr, self.l, self.per_row, self.t_len = arr, layer, per_row, t_len

    def spec(self, k, tm, width=D_MODEL, col=None):
        l, tpb, nb = self.l, max(self.t_len // tm, 1), D_MODEL // width

        def cidx(idx):
            return k * nb + (idx[col] if col is not None else 0)

        if self.per_row:
            return pl.BlockSpec((None, tm, width), lambda *idx: (l, idx[0], cidx(idx)))
        return pl.BlockSpec((None, None, 1, width), lambda *idx: (l, idx[0] // tpb, 0, cidx(idx)))


def _mod_rows(ref, r, rc):
    return ref[...] if ref.shape[0] == 1 else ref[pl.ds(r, rc), :]


def _modnorm_to(h_ref, x_ref, sc_ref, sh_ref, g_ref, rc):
    g = g_ref[...]

    def body(c, carry):
        r = pl.multiple_of(c * rc, rc)
        x = x_ref[pl.ds(r, rc), :]
        ms = jnp.mean(x * x, axis=-1, keepdims=True)
        y = x * lax.rsqrt(ms + EPS) * g
        h = y * (1.0 + _mod_rows(sc_ref, r, rc)) + _mod_rows(sh_ref, r, rc)
        h_ref[pl.ds(r, rc), :] = h.astype(h_ref.dtype)
        return carry

    lax.fori_loop(0, x_ref.shape[0] // rc, body, 0)


def _row_chunk(tm):
    return 128 if tm % 128 == 0 else tm


def _inproj_kernel(x_ref, sc_ref, sh_ref, g_ref, w_ref, o_ref, h_ref, *, rc):
    @pl.when(pl.program_id(1) == 0)
    def _():
        _modnorm_to(h_ref, x_ref, sc_ref, sh_ref, g_ref, rc)

    o_ref[...] = _dot(h_ref[...], w_ref[...])


def _inproj_call(x, mod, g_norm, w_in_b, layer, tm):
    m, d = x.shape
    depth = w_in_b.shape[0]
    return pl.pallas_call(
        functools.partial(_inproj_kernel, rc=_row_chunk(tm)),
        grid=(m // tm, SLAB_GATE),
        in_specs=[
            pl.BlockSpec((tm, d), lambda i, j: (i, 0), pipeline_mode=pl.Buffered(1)),
            mod.spec(1, tm), mod.spec(0, tm),
            pl.BlockSpec((None, 1, d), lambda i, j: (layer, 0, 0)),
            pl.BlockSpec((None, d, SLAB), lambda i, j: (layer, 0, j)),
        ],
        out_specs=[pl.BlockSpec((None, tm, SLAB), lambda i, j: (j, i, 0)),
                   pl.BlockSpec((tm, d), lambda i, j: (i, 0))],
        out_shape=[jax.ShapeDtypeStruct((SLAB_GATE, m, SLAB), F32), jax.ShapeDtypeStruct((m, d), BF16)],
        compiler_params=_params("arbitrary", "arbitrary"),
        name="inproj",
    )(x, mod.arr, mod.arr, g_norm.reshape(depth, 1, d), w_in_b)


def _gate_kernel(h_ref, w_ref, o_ref):
    z = _dot(h_ref[...], w_ref[...])
    o_ref[...] = (0.5 * jnp.tanh(0.5 * z) + 0.5).astype(o_ref.dtype)


def _gate_call(h, w_in_b, layer, tm):
    m, d = h.shape
    per = MERGE_TN // SLAB
    return pl.pallas_call(
        _gate_kernel,
        grid=(m // tm, N_SLAB - SLAB_GATE),
        in_specs=[
            pl.BlockSpec((tm, d), lambda i, j: (i, 0)),
            pl.BlockSpec((None, d, SLAB), lambda i, j: (layer, 0, SLAB_GATE + j)),
        ],
        out_specs=pl.BlockSpec((None, tm, SLAB), lambda i, j: (j // per, i, j % per)),
        out_shape=jax.ShapeDtypeStruct(((N_SLAB - SLAB_GATE) // per, m, MERGE_TN), BF16),
        compiler_params=_params("arbitrary", "arbitrary"),
        name="ingate",
    )(h, w_in_b)


def _group_sum_matrix(width, group):
    idx = np.arange(width) // group
    return jnp.asarray((idx[:, None] == idx[None, :]).astype(np.float32), dtype=BF16)


def _attn_kernel(*refs, layer, tq, has_init):
    if has_init:
        sinks_ref, q_ref, kvc_ref, kinit_ref, vinit_ref, gq_ref, gk_ref, gm512_ref, gm256_ref, sd_ref, o_ref, kn_ref = refs
    else:
        sinks_ref, q_ref, kvc_ref, kvp_ref, gq_ref, gk_ref, gm512_ref, gm256_ref, sd_ref, o_ref, kn_ref = refs
    n = pl.program_id(1)

    def qk_norm(x, gmat, g):
        x2 = x * x
        hi = x2.astype(BF16)
        lo = (x2 - hi.astype(F32)).astype(BF16)
        ss = _dot(hi, gmat) + _dot(lo, gmat)
        return x * lax.rsqrt(ss * (1.0 / HEAD_DIM) + EPS) * g

    gm256, gk = gm256_ref[...], gk_ref[...]
    kvc = kvc_ref[...]
    kc = qk_norm(kvc[:, :KV_W], gm256, gk)
    kn_ref[...] = kc
    vc = kvc[:, KV_W:]
    if has_init:
        kp, vp = kinit_ref[...], vinit_ref[...]
    else:
        kvp = kvp_ref[...]
        kp, vp = qk_norm(kvp[:, :KV_W], gm256, gk), kvp[:, KV_W:]
    if tq < WINDOW:
        pad = jnp.zeros((WINDOW - tq, KV_W), F32)
        kc = jnp.concatenate([kc, pad], axis=0)
        vc = jnp.concatenate([vc, pad], axis=0)
    kall = jnp.concatenate([kp, kc], axis=0)
    vall = jnp.concatenate([vp, vc], axis=0)

    nk = 2 * WINDOW
    ii = lax.broadcasted_iota(jnp.int32, (tq, nk), 0)
    jj = lax.broadcasted_iota(jnp.int32, (tq, nk), 1)
    dist = WINDOW + ii - jj
    valid = (dist >= 0) & (dist <= WINDOW)
    if not has_init:
        valid = valid & (jj >= jnp.where(n > 0, 0, WINDOW))
    distm = jnp.where(valid, dist.astype(F32), -NEG_BIG)

    gm512, gq = gm512_ref[...], gq_ref[...] * (HEAD_DIM ** -0.5)
    qn = [qk_norm(q_ref[s], gm512, gq) for s in range(2)]
    lane_k = lax.broadcasted_iota(jnp.int32, (nk, LANES), 1) < HEAD_DIM
    lane_q = lax.broadcasted_iota(jnp.int32, (tq, LANES), 1) < HEAD_DIM
    sd = sd_ref[...]

    def two_copies(a, upper):
        if upper:
            bot = jnp.where(lane_k, 0.0, a)
            top = pltpu.roll(bot, HEAD_DIM, 1)
        else:
            top = jnp.where(lane_k, a, 0.0)
            bot = pltpu.roll(top, HEAD_DIM, 1)
        return jnp.concatenate([top, bot], axis=0).astype(BF16)

    kds, vds = [], []
    for kv in range(N_KV):
        cs = slice(LANES * (kv // 2), LANES * (kv // 2) + LANES)
        kds.append(two_copies(kall[:, cs], bool(kv % 2)))
        vds.append(two_copies(vall[:, cs], bool(kv % 2)))
    npair = N_HEADS // 2
    scores = []
    for p in range(npair):
        off = LANES * (p % 4)
        qp = qn[p // 4][:, off:off + LANES].astype(BF16)
        scores.append(_nt_dot(qp, kds[p // 2]))
    probs, sinkw = [], []
    for p in range(npair):
        es, sk = [], []
        for hh in range(2):
            h = 2 * p + hh
            slope = 2.0 ** (-8.0 * (h + 1) / N_HEADS)
            sink = sinks_ref[layer, h]
            sh = scores[p][:, nk * hh:nk * hh + nk] - slope * distm
            mx = jnp.maximum(jnp.max(sh, axis=-1, keepdims=True), sink)
            es.append(jnp.exp(sh - mx))
            sk.append(jnp.exp(sink - mx))
        probs.append(jnp.concatenate(es, axis=1).astype(BF16))
        sinkw.append(jnp.where(lane_q, sk[0], sk[1]))
    for p in range(npair):
        num = _dot(probs[p], vds[p // 2])
        den = _dot(probs[p], sd) + sinkw[p]
        o_ref[:, LANES * p:LANES * p + LANES] = (num / den).astype(o_ref.dtype)


def _attn_call(z3, sinks, gq_t, gk_t, layer, nb_batch, t_len, cache=None):
    m = z3.shape[1]
    has_init = cache is not None
    tq = WINDOW if t_len % WINDOW == 0 else t_len
    nb = t_len // tq
    assert nb == 1 or not has_init
    depth = gq_t.shape[0]
    gm512 = _group_sum_matrix(SLAB, HEAD_DIM)
    gm256 = _group_sum_matrix(KV_W, HEAD_DIM)
    sd_np = np.zeros((4 * WINDOW, LANES), np.float32)
    sd_np[:2 * WINDOW, :HEAD_DIM] = 1.0
    sd_np[2 * WINDOW:, HEAD_DIM:] = 1.0
    sd = jnp.asarray(sd_np, dtype=BF16)

    in_specs = [
        pl.BlockSpec(memory_space=pltpu.SMEM),
        pl.BlockSpec((2, tq, SLAB), lambda b, n: (0, b * nb + n, 0)),
        pl.BlockSpec((None, tq, SLAB), lambda b, n: (SLAB_KV, b * nb + n, 0)),
    ]
    args = [sinks, z3, z3]
    if has_init:
        in_specs += [pl.BlockSpec((None, None, WINDOW, KV_W), lambda b, n: (layer, b, 0, 0))] * 2
        args += [cache[0], cache[1]]
    else:
        in_specs += [pl.BlockSpec((None, WINDOW, SLAB), lambda b, n: (SLAB_KV, jnp.maximum(b * nb + n - 1, 0), 0))]
        args += [z3]
    in_specs += [
        pl.BlockSpec((None, 1, SLAB), lambda b, n: (layer, 0, 0)),
        pl.BlockSpec((None, 1, KV_W), lambda b, n: (layer, 0, 0)),
        pl.BlockSpec((SLAB, SLAB), lambda b, n: (0, 0)),
        pl.BlockSpec((KV_W, KV_W), lambda b, n: (0, 0)),
        pl.BlockSpec((4 * WINDOW, LANES), lambda b, n: (0, 0)),
    ]
    args += [gq_t, gk_t, gm512, gm256, sd]
    out_dtype = F32 if has_init else BF16
    return pl.pallas_call(
        functools.partial(_attn_kernel, layer=layer, tq=tq, has_init=has_init),
        grid=(nb_batch, nb),
        in_specs=in_specs,
        out_specs=[
            pl.BlockSpec((tq, BR_W), lambda b, n: (b * nb + n, 0)),
            pl.BlockSpec((tq, KV_W), lambda b, n: (b * nb + n, 0)),
        ],
        out_shape=[jax.ShapeDtypeStruct((m, BR_W), out_dtype), jax.ShapeDtypeStruct((m, KV_W), F32)],
        compiler_params=_params("arbitrary", "arbitrary"),
        name="attn",
    )(*args)


def _conv_kernel(*refs, tt, nt, has_init):
    if has_init:
        l0, l1, g0, g1, init_ref, w_ref, b_ref, gl_ref, bl_ref, o_ref, new_ref, ext, ybuf, shifted = refs
    else:
        l0, l1, g0, g1, w_ref, b_ref, gl_ref, bl_ref, o_ref, new_ref, ext, ybuf, shifted = refs
    t = pl.program_id(1)

    @pl.when(t == 0)
    def _():
        ext[0:CONV_HALO, :] = init_ref[...] if has_init else jnp.zeros((CONV_HALO, BR_W), F32)

    if nt > 1:
        @pl.when(t > 0)
        def _():
            ext[0:CONV_HALO, :] = ext[tt:tt + CONV_HALO, :]

    for cb, (lr, gr) in enumerate(((l0, g0), (l1, g1))):
        ext[CONV_HALO:CONV_HALO + tt, SLAB * cb:SLAB * cb + SLAB] = lr[...] * jax.nn.sigmoid(gr[...])
    new_ref[...] = ext[tt:tt + CONV_HALO, :]

    rs = min(tt, 32)
    base = CONV_HALO - (CONV_K - 1)
    nsh = shifted.shape[1]
    for c in range(BR_W // LANES):
        cs = slice(LANES * c, LANES * c + LANES)
        for r in range(1, SUBLANES):
            shifted[r - 1] = ext[r:r + nsh, cs]

        def taps(i, carry, cs=cs):
            r0 = pl.multiple_of(i * rs, rs)
            acc = jnp.broadcast_to(b_ref[:, cs], (rs, LANES))
            for r in range(SUBLANES):
                offs = [(base + k) // SUBLANES for k in range(CONV_K) if (base + k) % SUBLANES == r]
                rows = pl.ds(r0 + SUBLANES * offs[0], rs + SUBLANES * (offs[-1] - offs[0]))
                win = ext[rows, cs] if r == 0 else shifted[r - 1, rows, :]
                for a in offs:
                    k = SUBLANES * a + r - base
                    d = SUBLANES * (a - offs[0])
                    acc = acc + w_ref[k:k + 1, cs] * win[d:d + rs]
            ybuf[pl.ds(r0, rs), cs] = acc
            return carry

        lax.fori_loop(0, tt // rs, taps, 0)

    def norm_act(i, carry):
        rows = pl.ds(pl.multiple_of(i * rs, rs), rs)
        y = ybuf[rows, :]
        yc = y - jnp.mean(y, axis=-1, keepdims=True)
        var = jnp.mean(yc * yc, axis=-1, keepdims=True)
        yn = yc * lax.rsqrt(var + EPS) * gl_ref[...] + bl_ref[...]
        o_ref[rows, :] = _silu(yn).astype(o_ref.dtype)
        return carry

    lax.fori_loop(0, tt // rs, norm_act, 0, unroll=min(4, tt // rs))


def _conv_call(z3, w_dw, b_dw, g_ln, b_ln, layer, nb_batch, t_len, init=None):
    m = z3.shape[1]
    has_init = init is not None
    tt = 256 if t_len % 256 == 0 else t_len
    nt = t_len // tt
    assert nt == 1 or tt >= CONV_HALO
    depth = w_dw.shape[0]

    def slab(j):
        return pl.BlockSpec((None, tt, SLAB), lambda b, t: (j, b * nt + t, 0))

    def vec():
        return pl.BlockSpec((None, 1, BR_W), lambda b, t: (layer, 0, 0))

    in_specs = [slab(SLAB_CONV), slab(SLAB_CONV + 1), slab(SLAB_CONV + 2), slab(SLAB_CONV + 3)]
    args = [z3, z3, z3, z3]
    if has_init:
        in_specs.append(pl.BlockSpec((None, None, CONV_HALO, BR_W), lambda b, t: (layer, b, 0, 0)))
        args.append(init)
    in_specs += [pl.BlockSpec((None, CONV_K, BR_W), lambda b, t: (layer, 0, 0)), vec(), vec(), vec()]
    args += [w_dw, b_dw.reshape(depth, 1, BR_W), g_ln.reshape(depth, 1, BR_W), b_ln.reshape(depth, 1, BR_W)]
    return pl.pallas_call(
        functools.partial(_conv_kernel, tt=tt, nt=nt, has_init=has_init),
        grid=(nb_batch, nt),
        in_specs=in_specs,
        out_specs=[
            pl.BlockSpec((tt, BR_W), lambda b, t: (b * nt + t, 0)),
            pl.BlockSpec((None, CONV_HALO, BR_W), lambda b, t: (b, 0, 0)),
        ],
        out_shape=[
            jax.ShapeDtypeStruct((m, BR_W), F32 if has_init else BF16),
            jax.ShapeDtypeStruct((nb_batch, CONV_HALO, BR_W), F32),
        ],
        scratch_shapes=[pltpu.VMEM((CONV_HALO + tt, BR_W), F32), pltpu.VMEM((tt, BR_W), F32),
                        pltpu.VMEM((SUBLANES - 1, CONV_HALO + tt - SUBLANES, LANES), F32)],
        compiler_params=_params("arbitrary", "arbitrary"),
        name="conv",
    )(*args)


def _pool_kernel(*refs, tt, nt, has_init, pos0):
    if has_init:
        u0, u1, init_ref, w_ref, s_ref, o_ref, ext = refs
    else:
        u0, u1, w_ref, s_ref, o_ref, ext = refs
    t = pl.program_id(1)

    @pl.when(t == 0)
    def _():
        ext[0:POOL_HALO, :] = init_ref[...] if has_init else jnp.zeros((POOL_HALO, BR_W), F32)

    if nt > 1:
        @pl.when(t > 0)
        def _():
            ext[0:POOL_HALO, :] = ext[tt:tt + POOL_HALO, :]

    ext[POOL_HALO:POOL_HALO + tt, 0:SLAB] = u0[...]
    ext[POOL_HALO:POOL_HALO + tt, SLAB:2 * SLAB] = u1[...]
    pos = pos0 + t * tt + lax.broadcasted_iota(jnp.int32, (tt, 1), 0)
    for g, w in enumerate(POOL_WINDOWS):
        cs = slice(POOL_G * g, POOL_G * g + POOL_G)
        cur = ext[POOL_HALO:POOL_HALO + tt, cs]
        wsum = cur
        for s in range(1, w):
            wsum = wsum + ext[POOL_HALO - s:POOL_HALO - s + tt, cs]
        cnt = jnp.minimum(pos + 1, w).astype(F32)
        zg = wsum / cnt - cur
        y = _dot(zg.astype(BF16), w_ref[g].astype(BF16)) * s_ref[:, cs]
        o_ref[:, cs] = y.astype(o_ref.dtype)


def _pool_call(z3, w_pool, s_pool, layer, nb_batch, t_len, pos0, init=None):
    m = z3.shape[1]
    has_init = init is not None
    tt = 256 if t_len % 256 == 0 else t_len
    nt = t_len // tt
    assert nt == 1 or tt >= POOL_HALO
    depth = w_pool.shape[0]

    def slab(j):
        return pl.BlockSpec((None, tt, SLAB), lambda b, t: (j, b * nt + t, 0))

    in_specs = [slab(SLAB_POOL), slab(SLAB_POOL + 1)]
    args = [z3, z3]
    if has_init:
        in_specs.append(pl.BlockSpec((None, None, POOL_HALO, BR_W), lambda b, t: (layer, b, 0, 0)))
        args.append(init)
    in_specs += [
        pl.BlockSpec((None, len(POOL_WINDOWS), POOL_G, POOL_G), lambda b, t: (layer, 0, 0, 0)),
        pl.BlockSpec((None, 1, BR_W), lambda b, t: (layer, 0, 0)),
    ]
    args += [w_pool, s_pool.reshape(depth, 1, BR_W)]
    return pl.pallas_call(
        functools.partial(_pool_kernel, tt=tt, nt=nt, has_init=has_init, pos0=pos0),
        grid=(nb_batch, nt),
        in_specs=in_specs,
        out_specs=pl.BlockSpec((tt, BR_W), lambda b, t: (b * nt + t, 0)),
        out_shape=jax.ShapeDtypeStruct((m, BR_W), F32 if has_init else BF16),
        scratch_shapes=[pltpu.VMEM((POOL_HALO + tt, BR_W), F32)],
        compiler_params=_params("arbitrary", "arbitrary"),
        name="pool",
    )(*args)


def _ret_constants(c, cp):
    lg = np.log1p(-np.exp2(-5.0 - np.arange(RET_HEADS, dtype=np.float64)))
    i = np.arange(c, dtype=np.float64)
    diff = i[:, None] - i[None, :]
    decay = np.where(diff >= 0, np.exp(lg[:, None, None] * np.maximum(diff, 0.0)), 0.0)
    dec = np.zeros((RET_HEADS // 2, c, 2 * cp))
    for h in range(RET_HEADS):
        dec[h // 2, :, (h % 2) * cp:(h % 2) * cp + c] = decay[h]
    kfac = np.repeat(np.exp(lg[None, :] * (c - 1 - i)[:, None]), RET_DK, axis=1) * RET_DK ** -0.5
    cfac = np.repeat(np.exp(lg[None, :] * (i + 1)[:, None]), RET_DV, axis=1)
    gch = np.repeat(np.exp(lg * c), RET_DK).reshape(RET_HEADS // 2, 2 * RET_DK, 1)
    gch = np.broadcast_to(gch, (RET_HEADS // 2, 2 * RET_DK, RET_DV))
    f = lambda a: jnp.asarray(np.ascontiguousarray(a), dtype=F32)
    return f(dec), f(kfac), f(cfac), f(gch)


def _ret_kernel(*refs, c, cp, has_init):
    if has_init:
        rq, rk, rv0, rv1, rg0, rg1, s0_ref, gn_ref, dec_ref, kf_ref, cf_ref, gch_ref, o_ref, s_ref = refs
    else:
        rq, rk, rv0, rv1, rg0, rg1, gn_ref, dec_ref, kf_ref, cf_ref, gch_ref, o_ref, s_ref = refs
    n = pl.program_id(1)

    @pl.when(n == 0)
    def _():
        s_ref[...] = s0_ref[...] if has_init else jnp.zeros(s_ref.shape, F32)

    lo = lax.broadcasted_iota(jnp.int32, (c, LANES), 1) < RET_DK
    q = rq[...]
    kraw = rk[...]
    k = kraw * (RET_DK ** -0.5)
    kdec = kraw * kf_ref[...]
    rvs, rgs = (rv0, rv1), (rg0, rg1)

    def stack_heads(a):
        a0, a1 = jnp.where(lo, a, 0.0), jnp.where(lo, 0.0, a)
        if cp > c:
            z = jnp.zeros((cp - c, LANES), F32)
            return jnp.concatenate([a0, z, a1, z], axis=0)
        return jnp.concatenate([a0, a1], axis=0)

    for p in range(RET_HEADS // 2):
        cs = slice(LANES * p, LANES * p + LANES)
        qp = q[:, cs]
        s = _nt_dot(qp.astype(BF16), stack_heads(k[:, cs]).astype(BF16)) * dec_ref[p]
        hs = (2 * p, 2 * p + 1)
        vs = [rvs[h // 4][:, LANES * (h % 4):LANES * (h % 4) + LANES] for h in hs]
        zc = jnp.zeros((c, LANES), F32)
        rows0 = jnp.concatenate([vs[0], zc], axis=1)
        rows1 = jnp.concatenate([zc, vs[1]], axis=1)
        if cp > c:
            zp = jnp.zeros((cp - c, 2 * LANES), F32)
            zq = jnp.zeros((cp - c, LANES), F32)
            vbd = jnp.concatenate([rows0, zp, rows1, zp], axis=0)
            vst = jnp.concatenate([vs[0], zq, vs[1], zq], axis=0)
        else:
            vbd = jnp.concatenate([rows0, rows1], axis=0)
            vst = jnp.concatenate([vs[0], vs[1]], axis=0)
        inner = _dot(s.astype(BF16), vbd.astype(BF16))
        sprev = s_ref[p]
        qst = jnp.concatenate([jnp.where(lo, qp, 0.0), jnp.where(lo, 0.0, qp)], axis=0)
        cross = _dot(qst.astype(BF16), sprev.astype(BF16))
        upd = _tn_dot(stack_heads(kdec[:, cs]).astype(BF16), vst.astype(BF16))
        s_ref[p] = gch_ref[p] * sprev + upd
        for hh, h in enumerate(hs):
            hc = slice(LANES * h, LANES * h + LANES)
            o = inner[:, LANES * hh:LANES * hh + LANES] + cross[c * hh:c * hh + c, :] * cf_ref[:, hc]
            oc = o - jnp.mean(o, axis=-1, keepdims=True)
            var = jnp.mean(oc * oc, axis=-1, keepdims=True)
            gate = rgs[h // 4][:, LANES * (h % 4):LANES * (h % 4) + LANES]
            y = oc * lax.rsqrt(var + EPS) * gn_ref[:, hc] * _silu(gate)
            o_ref[:, hc] = y.astype(o_ref.dtype)


def _ret_call(z3, g_ret, layer, nb_batch, t_len, init=None):
    m = z3.shape[1]
    has_init = init is not None
    c = RET_CHUNK if t_len % RET_CHUNK == 0 else t_len
    cp = max(c, 64)
    nc = t_len // c
    depth = g_ret.shape[0]
    dec, kfac, cfac, gch = _ret_constants(c, cp)
    npair = RET_HEADS // 2

    def slab(j):
        return pl.BlockSpec((None, c, SLAB), lambda b, n: (j, b * nc + n, 0))

    def const(shape):
        nd = len(shape)
        return pl.BlockSpec(shape, lambda b, n: (0,) * nd)

    in_specs = [slab(SLAB_RQ), slab(SLAB_RK), slab(SLAB_RV), slab(SLAB_RV + 1), slab(SLAB_RG), slab(SLAB_RG + 1)]
    args = [z3] * 6
    if has_init:
        in_specs.append(pl.BlockSpec((None, None, npair, LANES, RET_DV), lambda b, n: (layer, b, 0, 0, 0)))
        args.append(init)
    in_specs += [pl.BlockSpec((None, 1, BR_W), lambda b, n: (layer, 0, 0)),
                 const(dec.shape), const(kfac.shape), const(cfac.shape), const(gch.shape)]
    args += [g_ret.reshape(depth, 1, BR_W), dec, kfac, cfac, gch]
    return pl.pallas_call(
        functools.partial(_ret_kernel, c=c, cp=cp, has_init=has_init),
        grid=(nb_batch, nc),
        in_specs=in_specs,
        out_specs=[
            pl.BlockSpec((c, BR_W), lambda b, n: (b * nc + n, 0)),
            pl.BlockSpec((None, npair, LANES, RET_DV), lambda b, n: (b, 0, 0, 0)),
        ],
        out_shape=[
            jax.ShapeDtypeStruct((m, BR_W), F32 if has_init else BF16),
            jax.ShapeDtypeStruct((nb_batch, npair, LANES, RET_DV), F32),
        ],
        compiler_params=_params("arbitrary", "arbitrary"),
        name="retention",
    )(*args)


def _merge_kernel(y0, y1, y2, y3, gate_ref, w_ref, o_ref, acc_ref):
    r, c = pl.program_id(1), pl.program_id(2)
    for k, y_ref in enumerate((y0, y1, y2, y3)):
        @pl.when(r == k)
        def _(k=k, y_ref=y_ref):
            val = gate_ref[...].astype(F32) * _dot(y_ref[...].astype(BF16), w_ref[...])
            if k == 0:
                acc_ref[c] = val
            elif k < N_BR - 1:
                acc_ref[c] += val
            else:
                o_ref[...] = (acc_ref[c] + val).astype(o_ref.dtype)


def _merge_call(ys, gates, w_br_b, layer, tm):
    m = gates.shape[1]
    tn = MERGE_TN
    nc = D_MODEL // tn
    y_spec = pl.BlockSpec((tm, BR_W), lambda i, r, c: (i, 0))
    return pl.pallas_call(
        _merge_kernel,
        grid=(m // tm, N_BR, nc),
        in_specs=[y_spec] * N_BR + [
            pl.BlockSpec((None, tm, tn), lambda i, r, c: (nc * r + c, i, 0)),
            pl.BlockSpec((None, None, BR_W, tn), lambda i, r, c: (layer, r, 0, c)),
        ],
        out_specs=pl.BlockSpec((tm, tn), lambda i, r, c: (i, jnp.where(r == N_BR - 1, c, 0))),
        out_shape=jax.ShapeDtypeStruct((m, D_MODEL), BF16),
        scratch_shapes=[pltpu.VMEM((nc, tm, tn), F32)],
        compiler_params=_params("arbitrary", "arbitrary", "arbitrary"),
        name="merge",
    )(*ys, gates, w_br_b)


def _outproj_kernel(m_ref, w_ref, x_ref, gt_ref, o_ref):
    o_ref[...] = x_ref[...] + gt_ref[...] * _dot(m_ref[...], w_ref[...])


def _outproj_call(merged, x, mod, w_out_b, layer, tm):
    m, d = x.shape
    return pl.pallas_call(
        _outproj_kernel,
        grid=(m // tm, d // SLAB),
        in_specs=[
            pl.BlockSpec((tm, d), lambda i, c: (i, 0)),
            pl.BlockSpec((None, d, SLAB), lambda i, c: (layer, 0, c)),
            pl.BlockSpec((tm, SLAB), lambda i, c: (i, c)),
            mod.spec(2, tm, width=SLAB, col=1),
        ],
        out_specs=pl.BlockSpec((tm, SLAB), lambda i, c: (i, c)),
        out_shape=jax.ShapeDtypeStruct((m, d), F32),
        compiler_params=_params("arbitrary", "arbitrary"),
        name="outproj",
    )(merged, w_out_b, x, mod.arr)


def _mlp_kernel(x_ref, sc_ref, sh_ref, gt_ref, g_ref, w1_ref, w2_ref, o_ref, h_ref, *, rc, nf):
    f = pl.program_id(1)

    @pl.when(f == 0)
    def _():
        _modnorm_to(h_ref, x_ref, sc_ref, sh_ref, g_ref, rc)
        o_ref[...] = jnp.zeros(o_ref.shape, F32)

    a = _dot(h_ref[...], w1_ref[...])
    a = jnp.square(jnp.maximum(a, 0.0)).astype(BF16)
    for c0 in range(0, o_ref.shape[1], SLAB):
        o_ref[:, c0:c0 + SLAB] += _dot(a, w2_ref[:, c0:c0 + SLAB])

    @pl.when(f == nf - 1)
    def _():
        def body(cidx, carry):
            r = pl.multiple_of(cidx * rc, rc)
            rows = pl.ds(r, rc)
            o_ref[rows, :] = x_ref[rows, :] + _mod_rows(gt_ref, r, rc) * o_ref[rows, :]
            return carry

        lax.fori_loop(0, x_ref.shape[0] // rc, body, 0)


def _mlp_call(x, mod, g_norm, w1_b, w2_b, layer, tm):
    m, d = x.shape
    depth, _, dff = w1_b.shape
    tf = MLP_TF
    nf = dff // tf
    return pl.pallas_call(
        functools.partial(_mlp_kernel, rc=_row_chunk(tm), nf=nf),
        grid=(m // tm, nf),
        in_specs=[
            pl.BlockSpec((tm, d), lambda i, f: (i, 0), pipeline_mode=pl.Buffered(1)),
            mod.spec(4, tm), mod.spec(3, tm), mod.spec(5, tm),
            pl.BlockSpec((None, 1, d), lambda i, f: (layer, 0, 0)),
            pl.BlockSpec((None, d, tf), lambda i, f: (layer, 0, f)),
            pl.BlockSpec((None, tf, d), lambda i, f: (layer, f, 0)),
        ],
        out_specs=pl.BlockSpec((tm, d), lambda i, f: (i, 0)),
        out_shape=jax.ShapeDtypeStruct((m, d), F32),
        scratch_shapes=[pltpu.VMEM((tm, d), BF16)],
        compiler_params=_params("arbitrary", "arbitrary"),
        name="mlp",
    )(x, mod.arr, mod.arr, mod.arr, g_norm.reshape(depth, 1, d), w1_b, w2_b)


def _layer(x, mod, p, layer, nb_batch, t_len, tm_in, tm, cache):
    z3, h = _inproj_call(x, mod, p["g_norm1"], p["w_in_b"], layer, tm_in)
    gates = _gate_call(h, p["w_in_b"], layer, tm_in)
    if cache is None:
        attn_cache = conv_init = pool_init = ret_init = None
        pos0 = 0
    else:
        attn_cache, conv_init, pool_init, ret_init = cache
        pos0 = PAST_LEN
    y_att, k_norm = _attn_call(z3, p["attn_sinks"], p["gq_t"], p["gk_t"], layer, nb_batch, t_len, attn_cache)
    y_conv, conv_tail = _conv_call(z3, p["w_dw"], p["b_dw"], p["g_conv_ln"], p["b_conv_ln"], layer, nb_batch, t_len,
                                   conv_init)
    y_pool = _pool_call(z3, p["w_pool"], p["s_pool"], layer, nb_batch, t_len, pos0, pool_init)
    y_ret, s_new = _ret_call(z3, p["g_ret_norm"], layer, nb_batch, t_len, ret_init)
    merged = _merge_call((y_att, y_conv, y_pool, y_ret), gates, p["w_br_b"], layer, tm)
    x = _outproj_call(merged, x, mod, p["w_out_b"], layer, tm_in)
    x = _mlp_call(x, mod, p["g_norm2"], p["w_mlp1_b"], p["w_mlp2_b"], layer, tm)
    return x, (z3, k_norm, conv_tail, s_new)


def kernel(x_prompt, x_sample, c_prompt, c_sample, cache_attn_k, cache_attn_v, state_conv, state_pool, state_ret,
           w_ada, b_ada, g_norm1, g_norm2, w_in, g_qnorm, g_knorm, attn_sinks, w_dw, b_dw, g_conv_ln, b_conv_ln,
           w_pool, s_pool, g_ret_norm, w_br, w_out, w_mlp1, w_mlp2):
    nb, t_len, d = x_prompt.shape
    nsb, st_len, _ = x_sample.shape
    depth = w_ada.shape[0]
    mp, ms = nb * t_len, nsb * st_len

    n_c = nb + nsb
    r_pad = -(-n_c // 16) * 16
    c_all = jnp.concatenate([c_prompt, c_sample, jnp.zeros((r_pad - n_c, d), F32)], axis=0)
    ada = _ada_call(c_all, w_ada, b_ada)
    ada_p = ada[:, :nb].reshape(depth, nb, 1, 6 * d)
    ada_s = jnp.repeat(ada[:, nb:n_c], st_len, axis=1)

    p = dict(g_norm1=g_norm1, g_norm2=g_norm2, attn_sinks=attn_sinks, w_dw=w_dw, b_dw=b_dw,
             g_conv_ln=g_conv_ln, b_conv_ln=b_conv_ln, w_pool=w_pool, s_pool=s_pool, g_ret_norm=g_ret_norm,
             w_in_b=w_in.astype(BF16), w_br_b=w_br.astype(BF16), w_out_b=w_out.astype(BF16),
             w_mlp1_b=w_mlp1.astype(BF16), w_mlp2_b=w_mlp2.astype(BF16),
             gq_t=jnp.tile(g_qnorm, (1, SLAB // HEAD_DIM)).reshape(depth, 1, SLAB),
             gk_t=jnp.tile(g_knorm, (1, N_KV)).reshape(depth, 1, KV_W))

    tm_p = 1024 if t_len % 1024 == 0 else t_len
    tm_p_in = 2048 if t_len % 2048 == 0 else tm_p
    tm_s = ms
    cache_k2 = cache_attn_k.reshape(depth, nsb, WINDOW, KV_W)
    cache_v2 = cache_attn_v.reshape(depth, nsb, WINDOW, KV_W)
    conv_init = jnp.pad(state_conv, ((0, 0), (0, 0), (CONV_HALO - (CONV_K - 1), 0), (0, 0)))
    pool_init = jnp.pad(state_pool, ((0, 0), (0, 0), (POOL_HALO - POOL_PAD, 0), (0, 0)))
    ret_init = state_ret.reshape(depth, nsb, RET_HEADS // 2, 2 * RET_DK, RET_DV)

    xp = x_prompt.reshape(mp, d)
    xs = x_sample.reshape(ms, d)
    st_p = [[] for _ in range(5)]
    st_s = [[] for _ in range(5)]
    for l in range(depth):
        mod_p = _Mod(ada_p, l, False, t_len)
        xp, (z3, k_norm, conv_tail, s_new) = _layer(xp, mod_p, p, l, nb, t_len, tm_p_in, tm_p, None)
        z4 = z3.reshape(SLAB_GATE, nb, t_len, SLAB)
        st_p[0].append(k_norm.reshape(nb, t_len, KV_W)[:, -WINDOW:].reshape(nb, WINDOW, N_KV, HEAD_DIM))
        st_p[1].append(z4[SLAB_KV, :, -WINDOW:, KV_W:].reshape(nb, WINDOW, N_KV, HEAD_DIM))
        st_p[2].append(conv_tail[:, CONV_HALO - (CONV_K - 1):])
        pool_u = z4[SLAB_POOL:SLAB_POOL + 2, :, -POOL_PAD:]
        st_p[3].append(jnp.moveaxis(pool_u, 0, 2).reshape(nb, POOL_PAD, BR_W))
        st_p[4].append(s_new.reshape(nb, RET_HEADS, RET_DK, RET_DV))

        mod_s = _Mod(ada_s, l, True, st_len)
        cache = ((cache_k2, cache_v2), conv_init, pool_init, ret_init)
        xs, (z3, k_norm, conv_tail, s_new) = _layer(xs, mod_s, p, l, nsb, st_len, tm_s, tm_s, cache)
        k_new = k_norm.reshape(nsb, st_len, N_KV, HEAD_DIM)
        v_new = z3[SLAB_KV].reshape(nsb, st_len, SLAB)[:, :, KV_W:].reshape(nsb, st_len, N_KV, HEAD_DIM)
        st_s[0].append(jnp.concatenate([cache_attn_k[l], k_new], axis=1)[:, -WINDOW:])
        st_s[1].append(jnp.concatenate([cache_attn_v[l], v_new], axis=1)[:, -WINDOW:])
        st_s[2].append(conv_tail[:, CONV_HALO - (CONV_K - 1):])
        pool_u = jnp.moveaxis(z3[SLAB_POOL:SLAB_POOL + 2].reshape(2, nsb, st_len, SLAB), 0, 2).reshape(nsb, st_len, BR_W)
        st_s[3].append(jnp.concatenate([state_pool[l], pool_u], axis=1)[:, -POOL_PAD:])
        st_s[4].append(s_new.reshape(nsb, RET_HEADS, RET_DK, RET_DV))

    return (xp.reshape(nb, t_len, d), xs.reshape(nsb, st_len, d),
            *[jnp.stack(a) for a in st_p], *[jnp.stack(a) for a in st_s])
```

```python
import functools

import numpy as np
import jax
import jax.numpy as jnp
from jax import lax
from jax.experimental import pallas as pl
from jax.experimental.pallas import tpu as pltpu

F32 = jnp.float32
BF16 = jnp.bfloat16

D_MODEL = 2048
PAST_LEN = 16384
N_HEADS = 16
HEAD_DIM = 64
N_KV = 4
WINDOW = 128
BR_W = 1024
CONV_K = 31
POOL_WINDOWS = (2, 4, 8, 16)
POOL_G = 256
POOL_PAD = 15
RET_HEADS = 8
RET_DK = 64
RET_DV = 128
RET_CHUNK = 128
N_BR = 4
D_FF = 4 * D_MODEL
MLP_TF = 1024
MLP_TF_MASTER = 512
MERGE_TN = 1024
EPS = 1e-6
KV_W = N_KV * HEAD_DIM
N_IN = 15872

SLAB = 512
N_SLAB = N_IN // SLAB
SLAB_Q, SLAB_KV, SLAB_CONV, SLAB_POOL, SLAB_RQ, SLAB_RK, SLAB_RV, SLAB_RG, SLAB_GATE = 0, 2, 3, 7, 9, 10, 11, 13, 15

VMEM_LIMIT_BYTES = 56 * 1024 * 1024
LANES = 128
SUBLANES = 8
NEG_BIG = -1e30
CONV_HALO = 32
POOL_HALO = 16


def _params(*sem):
    return pltpu.CompilerParams(dimension_semantics=sem, vmem_limit_bytes=VMEM_LIMIT_BYTES)


def _nt_dot(a, b):
    return lax.dot_general(a, b, (((1,), (1,)), ((), ())), preferred_element_type=F32)


def _tn_dot(a, b):
    return lax.dot_general(a, b, (((0,), (0,)), ((), ())), preferred_element_type=F32)


def _dot(a, b):
    return jnp.dot(a, b, preferred_element_type=F32)


def _silu(x):
    return x * jax.nn.sigmoid(x)


def _ada_kernel(c_ref, w_ref, b_ref, o_ref):
    s = _silu(c_ref[...]).astype(BF16)
    o_ref[...] = _dot(s, w_ref[...].astype(BF16)) + b_ref[...]


def _ada_call(c_all, w_ada, b_ada):
    depth, d, n = w_ada.shape
    r = c_all.shape[0]
    tn = 1024
    return pl.pallas_call(
        _ada_kernel,
        grid=(depth, n // tn),
        in_specs=[
            pl.BlockSpec((r, d), lambda l, j: (0, 0)),
            pl.BlockSpec((None, d, tn), lambda l, j: (l, 0, j)),
            pl.BlockSpec((None, 1, tn), lambda l, j: (l, 0, j)),
        ],
        out_specs=pl.BlockSpec((None, r, tn), lambda l, j: (l, 0, j)),
        out_shape=jax.ShapeDtypeStruct((depth, r, n), F32),
        compiler_params=_params("arbitrary", "arbitrary"),
        name="ada",
    )(c_all, w_ada, b_ada.reshape(depth, 1, n))


class _Mod:
    def __init__(self, arr, layer, per_row, t_len):
        self.arr, self.l, self.per_row, self.t_len = arr, layer, per_row, t_len

    def spec(self, k, tm, width=D_MODEL, col=None):
        l, tpb, nb = self.l, max(self.t_len // tm, 1), D_MODEL // width

        def cidx(idx):
            return k * nb + (idx[col] if col is not None else 0)

        if self.per_row:
            return pl.BlockSpec((None, tm, width), lambda *idx: (l, idx[0], cidx(idx)))
        return pl.BlockSpec((None, None, 1, width), lambda *idx: (l, idx[0] // tpb, 0, cidx(idx)))


def _mod_rows(ref, r, rc):
    return ref[...] if ref.shape[0] == 1 else ref[pl.ds(r, rc), :]


def _modnorm_to(h_ref, x_ref, sc_ref, sh_ref, g_ref, rc):
    g = g_ref[...]

    def body(c, carry):
        r = pl.multiple_of(c * rc, rc)
        x = x_ref[pl.ds(r, rc), :]
        ms = jnp.mean(x * x, axis=-1, keepdims=True)
        y = x * lax.rsqrt(ms + EPS) * g
        h = y * (1.0 + _mod_rows(sc_ref, r, rc)) + _mod_rows(sh_ref, r, rc)
        h_ref[pl.ds(r, rc), :] = h.astype(h_ref.dtype)
        return carry

    lax.fori_loop(0, x_ref.shape[0] // rc, body, 0)


def _row_chunk(tm):
    return 128 if tm % 128 == 0 else tm


class _Weight:
    def __init__(self, arr, layer=None, shape=None, col0=0):
        self.arr, self.layer, self.col0 = arr, layer, col0
        self.shape = tuple(shape if shape is not None else (arr.shape[1:] if layer is not None else arr.shape))

    @property
    def master(self):
        return self.layer is not None

    def spec(self, block, index):
        if not self.master:
            return pl.BlockSpec(block, index)
        layer, col0 = self.layer, self.col0

        def master_index(*g):
            idx = tuple(index(*g))
            return (layer,) + idx[:-1] + (idx[-1] + col0,)

        return pl.BlockSpec((None,) + tuple(block), master_index)

    def emit(self, block, index):
        return pl.BlockSpec(block, index), jax.ShapeDtypeStruct(self.shape, BF16)


def _load_weight(w_ref, copy_ref):
    if copy_ref is None:
        return w_ref[...]
    w = w_ref[...].astype(BF16)
    copy_ref[...] = w
    return w


def _single_tile(m, tm, w):
    assert not w.master or m == tm, "a master weight must be streamed by a single row tile"


def _inproj_kernel(x_ref, sc_ref, sh_ref, g_ref, w_ref, o_ref, h_ref, wb_ref=None, *, rc):
    @pl.when(pl.program_id(1) == 0)
    def _():
        _modnorm_to(h_ref, x_ref, sc_ref, sh_ref, g_ref, rc)

    o_ref[...] = _dot(h_ref[...], _load_weight(w_ref, wb_ref))


def _inproj_call(x, mod, g_norm, w, layer, tm):
    m, d = x.shape
    _single_tile(m, tm, w)
    depth = g_norm.shape[0]
    wblock, windex = (d, SLAB), lambda i, j: (0, j)
    out_specs = [pl.BlockSpec((None, tm, SLAB), lambda i, j: (j, i, 0)), pl.BlockSpec((tm, d), lambda i, j: (i, 0))]
    out_shape = [jax.ShapeDtypeStruct((SLAB_GATE, m, SLAB), F32), jax.ShapeDtypeStruct((m, d), BF16)]
    if w.master:
        spec, shape = w.emit(wblock, windex)
        out_specs.append(spec)
        out_shape.append(shape)
    return pl.pallas_call(
        functools.partial(_inproj_kernel, rc=_row_chunk(tm)),
        grid=(m // tm, SLAB_GATE),
        in_specs=[
            pl.BlockSpec((tm, d), lambda i, j: (i, 0), pipeline_mode=pl.Buffered(1)),
            mod.spec(1, tm), mod.spec(0, tm),
            pl.BlockSpec((None, 1, d), lambda i, j: (layer, 0, 0)),
            w.spec(wblock, windex),
        ],
        out_specs=out_specs,
        out_shape=out_shape,
        compiler_params=_params("arbitrary", "arbitrary"),
        name="inproj",
    )(x, mod.arr, mod.arr, g_norm.reshape(depth, 1, d), w.arr)


def _gate_kernel(h_ref, w_ref, o_ref, wb_ref=None):
    z = _dot(h_ref[...], _load_weight(w_ref, wb_ref))
    o_ref[...] = (0.5 * jnp.tanh(0.5 * z) + 0.5).astype(o_ref.dtype)


def _gate_call(h, w, tm):
    m, d = h.shape
    _single_tile(m, tm, w)
    per = MERGE_TN // SLAB
    wblock, windex = (d, SLAB), lambda i, j: (0, j)
    out_specs = [pl.BlockSpec((None, tm, SLAB), lambda i, j: (j // per, i, j % per))]
    out_shape = [jax.ShapeDtypeStruct(((N_SLAB - SLAB_GATE) // per, m, MERGE_TN), BF16)]
    if w.master:
        spec, shape = w.emit(wblock, windex)
        out_specs.append(spec)
        out_shape.append(shape)
    return pl.pallas_call(
        _gate_kernel,
        grid=(m // tm, N_SLAB - SLAB_GATE),
        in_specs=[pl.BlockSpec((tm, d), lambda i, j: (i, 0)), w.spec(wblock, windex)],
        out_specs=out_specs,
        out_shape=out_shape,
        compiler_params=_params("arbitrary", "arbitrary"),
        name="ingate",
    )(h, w.arr)


def _group_sum_matrix(width, group):
    idx = np.arange(width) // group
    return jnp.asarray((idx[:, None] == idx[None, :]).astype(np.float32), dtype=BF16)


def _attn_kernel(*refs, layer, tq, has_init):
    if has_init:
        sinks_ref, q_ref, kvc_ref, kinit_ref, vinit_ref, gq_ref, gk_ref, gm512_ref, gm256_ref, sd_ref, o_ref, kn_ref = refs
    else:
        sinks_ref, q_ref, kvc_ref, kvp_ref, gq_ref, gk_ref, gm512_ref, gm256_ref, sd_ref, o_ref, kn_ref = refs
    n = pl.program_id(1)

    def qk_norm(x, gmat, g):
        x2 = x * x
        hi = x2.astype(BF16)
        lo = (x2 - hi.astype(F32)).astype(BF16)
        ss = _dot(hi, gmat) + _dot(lo, gmat)
        return x * lax.rsqrt(ss * (1.0 / HEAD_DIM) + EPS) * g

    gm256, gk = gm256_ref[...], gk_ref[...]
    kvc = kvc_ref[...]
    kc = qk_norm(kvc[:, :KV_W], gm256, gk)
    kn_ref[...] = kc
    vc = kvc[:, KV_W:]
    if has_init:
        kp, vp = kinit_ref[...], vinit_ref[...]
    else:
        kvp = kvp_ref[...]
        kp, vp = qk_norm(kvp[:, :KV_W], gm256, gk), kvp[:, KV_W:]
    if tq < WINDOW:
        pad = jnp.zeros((WINDOW - tq, KV_W), F32)
        kc = jnp.concatenate([kc, pad], axis=0)
        vc = jnp.concatenate([vc, pad], axis=0)
    kall = jnp.concatenate([kp, kc], axis=0)
    vall = jnp.concatenate([vp, vc], axis=0)

    nk = 2 * WINDOW
    ii = lax.broadcasted_iota(jnp.int32, (tq, nk), 0)
    jj = lax.broadcasted_iota(jnp.int32, (tq, nk), 1)
    dist = WINDOW + ii - jj
    valid = (dist >= 0) & (dist <= WINDOW)
    if not has_init:
        valid = valid & (jj >= jnp.where(n > 0, 0, WINDOW))
    distm = jnp.where(valid, dist.astype(F32), -NEG_BIG)

    gm512, gq = gm512_ref[...], gq_ref[...] * (HEAD_DIM ** -0.5)
    qn = [qk_norm(q_ref[s], gm512, gq) for s in range(2)]
    lane_k = lax.broadcasted_iota(jnp.int32, (nk, LANES), 1) < HEAD_DIM
    lane_q = lax.broadcasted_iota(jnp.int32, (tq, LANES), 1) < HEAD_DIM
    sd = sd_ref[...]

    def two_copies(a, upper):
        if upper:
            bot = jnp.where(lane_k, 0.0, a)
            top = pltpu.roll(bot, HEAD_DIM, 1)
        else:
            top = jnp.where(lane_k, a, 0.0)
            bot = pltpu.roll(top, HEAD_DIM, 1)
        return jnp.concatenate([top, bot], axis=0).astype(BF16)

    kds, vds = [], []
    for kv in range(N_KV):
        cs = slice(LANES * (kv // 2), LANES * (kv // 2) + LANES)
        kds.append(two_copies(kall[:, cs], bool(kv % 2)))
        vds.append(two_copies(vall[:, cs], bool(kv % 2)))
    npair = N_HEADS // 2
    scores = []
    for p in range(npair):
        off = LANES * (p % 4)
        qp = qn[p // 4][:, off:off + LANES].astype(BF16)
        scores.append(_nt_dot(qp, kds[p // 2]))
    probs, sinkw = [], []
    for p in range(npair):
        es, sk = [], []
        for hh in range(2):
            h = 2 * p + hh
            slope = 2.0 ** (-8.0 * (h + 1) / N_HEADS)
            sink = sinks_ref[layer, h]
            sh = scores[p][:, nk * hh:nk * hh + nk] - slope * distm
            mx = jnp.maximum(jnp.max(sh, axis=-1, keepdims=True), sink)
            es.append(jnp.exp(sh - mx))
            sk.append(jnp.exp(sink - mx))
        probs.append(jnp.concatenate(es, axis=1).astype(BF16))
        sinkw.append(jnp.where(lane_q, sk[0], sk[1]))
    for p in range(npair):
        num = _dot(probs[p], vds[p // 2])
        den = _dot(probs[p], sd) + sinkw[p]
        o_ref[:, LANES * p:LANES * p + LANES] = (num / den).astype(o_ref.dtype)


def _attn_call(z3, sinks, gq_t, gk_t, layer, nb_batch, t_len, cache=None):
    m = z3.shape[1]
    has_init = cache is not None
    tq = WINDOW if t_len % WINDOW == 0 else t_len
    nb = t_len // tq
    assert nb == 1 or not has_init
    depth = gq_t.shape[0]
    gm512 = _group_sum_matrix(SLAB, HEAD_DIM)
    gm256 = _group_sum_matrix(KV_W, HEAD_DIM)
    sd_np = np.zeros((4 * WINDOW, LANES), np.float32)
    sd_np[:2 * WINDOW, :HEAD_DIM] = 1.0
    sd_np[2 * WINDOW:, HEAD_DIM:] = 1.0
    sd = jnp.asarray(sd_np, dtype=BF16)

    in_specs = [
        pl.BlockSpec(memory_space=pltpu.SMEM),
        pl.BlockSpec((2, tq, SLAB), lambda b, n: (0, b * nb + n, 0)),
        pl.BlockSpec((None, tq, SLAB), lambda b, n: (SLAB_KV, b * nb + n, 0)),
    ]
    args = [sinks, z3, z3]
    if has_init:
        in_specs += [pl.BlockSpec((None, None, WINDOW, KV_W), lambda b, n: (layer, b, 0, 0))] * 2
        args += [cache[0], cache[1]]
    else:
        in_specs += [pl.BlockSpec((None, WINDOW, SLAB), lambda b, n: (SLAB_KV, jnp.maximum(b * nb + n - 1, 0), 0))]
        args += [z3]
    in_specs += [
        pl.BlockSpec((None, 1, SLAB), lambda b, n: (layer, 0, 0)),
        pl.BlockSpec((None, 1, KV_W), lambda b, n: (layer, 0, 0)),
        pl.BlockSpec((SLAB, SLAB), lambda b, n: (0, 0)),
        pl.BlockSpec((KV_W, KV_W), lambda b, n: (0, 0)),
        pl.BlockSpec((4 * WINDOW, LANES), lambda b, n: (0, 0)),
    ]
    args += [gq_t, gk_t, gm512, gm256, sd]
    out_dtype = F32 if has_init else BF16
    return pl.pallas_call(
        functools.partial(_attn_kernel, layer=layer, tq=tq, has_init=has_init),
        grid=(nb_batch, nb),
        in_specs=in_specs,
        out_specs=[
            pl.BlockSpec((tq, BR_W), lambda b, n: (b * nb + n, 0)),
            pl.BlockSpec((tq, KV_W), lambda b, n: (b * nb + n, 0)),
        ],
        out_shape=[jax.ShapeDtypeStruct((m, BR_W), out_dtype), jax.ShapeDtypeStruct((m, KV_W), F32)],
        compiler_params=_params("arbitrary", "arbitrary"),
        name="attn",
    )(*args)


def _conv_kernel(*refs, tt, nt, has_init):
    if has_init:
        l0, l1, g0, g1, init_ref, w_ref, b_ref, gl_ref, bl_ref, o_ref, new_ref, ext, ybuf, shifted = refs
    else:
        l0, l1, g0, g1, w_ref, b_ref, gl_ref, bl_ref, o_ref, new_ref, ext, ybuf, shifted = refs
    t = pl.program_id(1)

    @pl.when(t == 0)
    def _():
        ext[0:CONV_HALO, :] = init_ref[...] if has_init else jnp.zeros((CONV_HALO, BR_W), F32)

    if nt > 1:
        @pl.when(t > 0)
        def _():
            ext[0:CONV_HALO, :] = ext[tt:tt + CONV_HALO, :]

    for cb, (lr, gr) in enumerate(((l0, g0), (l1, g1))):
        ext[CONV_HALO:CONV_HALO + tt, SLAB * cb:SLAB * cb + SLAB] = lr[...] * jax.nn.sigmoid(gr[...])
    new_ref[...] = ext[tt:tt + CONV_HALO, :]

    rs = min(tt, 32)
    base = CONV_HALO - (CONV_K - 1)
    nsh = shifted.shape[1]
    for c in range(BR_W // LANES):
        cs = slice(LANES * c, LANES * c + LANES)
        for r in range(1, SUBLANES):
            shifted[r - 1] = ext[r:r + nsh, cs]

        def taps(i, carry, cs=cs):
            r0 = pl.multiple_of(i * rs, rs)
            acc = jnp.broadcast_to(b_ref[:, cs], (rs, LANES))
            for r in range(SUBLANES):
                offs = [(base + k) // SUBLANES for k in range(CONV_K) if (base + k) % SUBLANES == r]
                rows = pl.ds(r0 + SUBLANES * offs[0], rs + SUBLANES * (offs[-1] - offs[0]))
                win = ext[rows, cs] if r == 0 else shifted[r - 1, rows, :]
                for a in offs:
                    k = SUBLANES * a + r - base
                    d = SUBLANES * (a - offs[0])
                    acc = acc + w_ref[k:k + 1, cs] * win[d:d + rs]
            ybuf[pl.ds(r0, rs), cs] = acc
            return carry

        lax.fori_loop(0, tt // rs, taps, 0)

    def norm_act(i, carry):
        rows = pl.ds(pl.multiple_of(i * rs, rs), rs)
        y = ybuf[rows, :]
        yc = y - jnp.mean(y, axis=-1, keepdims=True)
        var = jnp.mean(yc * yc, axis=-1, keepdims=True)
        yn = yc * lax.rsqrt(var + EPS) * gl_ref[...] + bl_ref[...]
        o_ref[rows, :] = _silu(yn).astype(o_ref.dtype)
        return carry

    lax.fori_loop(0, tt // rs, norm_act, 0, unroll=min(4, tt // rs))


def _conv_call(z3, w_dw, b_dw, g_ln, b_ln, layer, nb_batch, t_len, init=None):
    m = z3.shape[1]
    has_init = init is not None
    tt = 256 if t_len % 256 == 0 else t_len
    nt = t_len // tt
    assert nt == 1 or tt >= CONV_HALO
    depth = w_dw.shape[0]

    def slab(j):
        return pl.BlockSpec((None, tt, SLAB), lambda b, t: (j, b * nt + t, 0))

    def vec():
        return pl.BlockSpec((None, 1, BR_W), lambda b, t: (layer, 0, 0))

    in_specs = [slab(SLAB_CONV), slab(SLAB_CONV + 1), slab(SLAB_CONV + 2), slab(SLAB_CONV + 3)]
    args = [z3, z3, z3, z3]
    if has_init:
        in_specs.append(pl.BlockSpec((None, None, CONV_HALO, BR_W), lambda b, t: (layer, b, 0, 0)))
        args.append(init)
    in_specs += [pl.BlockSpec((None, CONV_K, BR_W), lambda b, t: (layer, 0, 0)), vec(), vec(), vec()]
    args += [w_dw, b_dw.reshape(depth, 1, BR_W), g_ln.reshape(depth, 1, BR_W), b_ln.reshape(depth, 1, BR_W)]
    return pl.pallas_call(
        functools.partial(_conv_kernel, tt=tt, nt=nt, has_init=has_init),
        grid=(nb_batch, nt),
        in_specs=in_specs,
        out_specs=[
            pl.BlockSpec((tt, BR_W), lambda b, t: (b * nt + t, 0)),
            pl.BlockSpec((None, CONV_HALO, BR_W), lambda b, t: (b, 0, 0)),
        ],
        out_shape=[
            jax.ShapeDtypeStruct((m, BR_W), F32 if has_init else BF16),
            jax.ShapeDtypeStruct((nb_batch, CONV_HALO, BR_W), F32),
        ],
        scratch_shapes=[pltpu.VMEM((CONV_HALO + tt, BR_W), F32), pltpu.VMEM((tt, BR_W), F32),
                        pltpu.VMEM((SUBLANES - 1, CONV_HALO + tt - SUBLANES, LANES), F32)],
        compiler_params=_params("arbitrary", "arbitrary"),
        name="conv",
    )(*args)


def _pool_kernel(*refs, tt, nt, has_init, pos0):
    if has_init:
        u0, u1, init_ref, w_ref, s_ref, o_ref, ext = refs
    else:
        u0, u1, w_ref, s_ref, o_ref, ext = refs
    t = pl.program_id(1)

    @pl.when(t == 0)
    def _():
        ext[0:POOL_HALO, :] = init_ref[...] if has_init else jnp.zeros((POOL_HALO, BR_W), F32)

    if nt > 1:
        @pl.when(t > 0)
        def _():
            ext[0:POOL_HALO, :] = ext[tt:tt + POOL_HALO, :]

    ext[POOL_HALO:POOL_HALO + tt, 0:SLAB] = u0[...]
    ext[POOL_HALO:POOL_HALO + tt, SLAB:2 * SLAB] = u1[...]
    pos = pos0 + t * tt + lax.broadcasted_iota(jnp.int32, (tt, 1), 0)
    for g, w in enumerate(POOL_WINDOWS):
        cs = slice(POOL_G * g, POOL_G * g + POOL_G)
        cur = ext[POOL_HALO:POOL_HALO + tt, cs]
        wsum = cur
        for s in range(1, w):
            wsum = wsum + ext[POOL_HALO - s:POOL_HALO - s + tt, cs]
        cnt = jnp.minimum(pos + 1, w).astype(F32)
        zg = wsum / cnt - cur
        y = _dot(zg.astype(BF16), w_ref[g].astype(BF16)) * s_ref[:, cs]
        o_ref[:, cs] = y.astype(o_ref.dtype)


def _pool_call(z3, w_pool, s_pool, layer, nb_batch, t_len, pos0, init=None):
    m = z3.shape[1]
    has_init = init is not None
    tt = 256 if t_len % 256 == 0 else t_len
    nt = t_len // tt
    assert nt == 1 or tt >= POOL_HALO
    depth = w_pool.shape[0]

    def slab(j):
        return pl.BlockSpec((None, tt, SLAB), lambda b, t: (j, b * nt + t, 0))

    in_specs = [slab(SLAB_POOL), slab(SLAB_POOL + 1)]
    args = [z3, z3]
    if has_init:
        in_specs.append(pl.BlockSpec((None, None, POOL_HALO, BR_W), lambda b, t: (layer, b, 0, 0)))
        args.append(init)
    in_specs += [
        pl.BlockSpec((None, len(POOL_WINDOWS), POOL_G, POOL_G), lambda b, t: (layer, 0, 0, 0)),
        pl.BlockSpec((None, 1, BR_W), lambda b, t: (layer, 0, 0)),
    ]
    args += [w_pool, s_pool.reshape(depth, 1, BR_W)]
    return pl.pallas_call(
        functools.partial(_pool_kernel, tt=tt, nt=nt, has_init=has_init, pos0=pos0),
        grid=(nb_batch, nt),
        in_specs=in_specs,
        out_specs=pl.BlockSpec((tt, BR_W), lambda b, t: (b * nt + t, 0)),
        out_shape=jax.ShapeDtypeStruct((m, BR_W), F32 if has_init else BF16),
        scratch_shapes=[pltpu.VMEM((POOL_HALO + tt, BR_W), F32)],
        compiler_params=_params("arbitrary", "arbitrary"),
        name="pool",
    )(*args)


def _ret_constants(c, cp):
    lg = np.log1p(-np.exp2(-5.0 - np.arange(RET_HEADS, dtype=np.float64)))
    i = np.arange(c, dtype=np.float64)
    diff = i[:, None] - i[None, :]
    decay = np.where(diff >= 0, np.exp(lg[:, None, None] * np.maximum(diff, 0.0)), 0.0)
    dec = np.zeros((RET_HEADS // 2, c, 2 * cp))
    for h in range(RET_HEADS):
        dec[h // 2, :, (h % 2) * cp:(h % 2) * cp + c] = decay[h]
    kfac = np.repeat(np.exp(lg[None, :] * (c - 1 - i)[:, None]), RET_DK, axis=1) * RET_DK ** -0.5
    cfac = np.repeat(np.exp(lg[None, :] * (i + 1)[:, None]), RET_DV, axis=1)
    gch = np.repeat(np.exp(lg * c), RET_DK).reshape(RET_HEADS // 2, 2 * RET_DK, 1)
    gch = np.broadcast_to(gch, (RET_HEADS // 2, 2 * RET_DK, RET_DV))
    f = lambda a: jnp.asarray(np.ascontiguousarray(a), dtype=F32)
    return f(dec), f(kfac), f(cfac), f(gch)


def _ret_kernel(*refs, c, cp, has_init):
    if has_init:
        rq, rk, rv0, rv1, rg0, rg1, s0_ref, gn_ref, dec_ref, kf_ref, cf_ref, gch_ref, o_ref, s_ref = refs
    else:
        rq, rk, rv0, rv1, rg0, rg1, gn_ref, dec_ref, kf_ref, cf_ref, gch_ref, o_ref, s_ref = refs
    n = pl.program_id(1)

    @pl.when(n == 0)
    def _():
        s_ref[...] = s0_ref[...] if has_init else jnp.zeros(s_ref.shape, F32)

    lo = lax.broadcasted_iota(jnp.int32, (c, LANES), 1) < RET_DK
    q = rq[...]
    kraw = rk[...]
    k = kraw * (RET_DK ** -0.5)
    kdec = kraw * kf_ref[...]
    rvs, rgs = (rv0, rv1), (rg0, rg1)

    def stack_heads(a):
        a0, a1 = jnp.where(lo, a, 0.0), jnp.where(lo, 0.0, a)
        if cp > c:
            z = jnp.zeros((cp - c, LANES), F32)
            return jnp.concatenate([a0, z, a1, z], axis=0)
        return jnp.concatenate([a0, a1], axis=0)

    for p in range(RET_HEADS // 2):
        cs = slice(LANES * p, LANES * p + LANES)
        qp = q[:, cs]
        s = _nt_dot(qp.astype(BF16), stack_heads(k[:, cs]).astype(BF16)) * dec_ref[p]
        hs = (2 * p, 2 * p + 1)
        vs = [rvs[h // 4][:, LANES * (h % 4):LANES * (h % 4) + LANES] for h in hs]
        zc = jnp.zeros((c, LANES), F32)
        rows0 = jnp.concatenate([vs[0], zc], axis=1)
        rows1 = jnp.concatenate([zc, vs[1]], axis=1)
        if cp > c:
            zp = jnp.zeros((cp - c, 2 * LANES), F32)
            zq = jnp.zeros((cp - c, LANES), F32)
            vbd = jnp.concatenate([rows0, zp, rows1, zp], axis=0)
            vst = jnp.concatenate([vs[0], zq, vs[1], zq], axis=0)
        else:
            vbd = jnp.concatenate([rows0, rows1], axis=0)
            vst = jnp.concatenate([vs[0], vs[1]], axis=0)
        inner = _dot(s.astype(BF16), vbd.astype(BF16))
        sprev = s_ref[p]
        qst = jnp.concatenate([jnp.where(lo, qp, 0.0), jnp.where(lo, 0.0, qp)], axis=0)
        cross = _dot(qst.astype(BF16), sprev.astype(BF16))
        upd = _tn_dot(stack_heads(kdec[:, cs]).astype(BF16), vst.astype(BF16))
        s_ref[p] = gch_ref[p] * sprev + upd
        for hh, h in enumerate(hs):
            hc = slice(LANES * h, LANES * h + LANES)
            o = inner[:, LANES * hh:LANES * hh + LANES] + cross[c * hh:c * hh + c, :] * cf_ref[:, hc]
            oc = o - jnp.mean(o, axis=-1, keepdims=True)
            var = jnp.mean(oc * oc, axis=-1, keepdims=True)
            gate = rgs[h // 4][:, LANES * (h % 4):LANES * (h % 4) + LANES]
            y = oc * lax.rsqrt(var + EPS) * gn_ref[:, hc] * _silu(gate)
            o_ref[:, hc] = y.astype(o_ref.dtype)


def _ret_call(z3, g_ret, layer, nb_batch, t_len, init=None):
    m = z3.shape[1]
    has_init = init is not None
    c = RET_CHUNK if t_len % RET_CHUNK == 0 else t_len
    cp = max(c, 64)
    nc = t_len // c
    depth = g_ret.shape[0]
    dec, kfac, cfac, gch = _ret_constants(c, cp)
    npair = RET_HEADS // 2

    def slab(j):
        return pl.BlockSpec((None, c, SLAB), lambda b, n: (j, b * nc + n, 0))

    def const(shape):
        nd = len(shape)
        return pl.BlockSpec(shape, lambda b, n: (0,) * nd)

    in_specs = [slab(SLAB_RQ), slab(SLAB_RK), slab(SLAB_RV), slab(SLAB_RV + 1), slab(SLAB_RG), slab(SLAB_RG + 1)]
    args = [z3] * 6
    if has_init:
        in_specs.append(pl.BlockSpec((None, None, npair, LANES, RET_DV), lambda b, n: (layer, b, 0, 0, 0)))
        args.append(init)
    in_specs += [pl.BlockSpec((None, 1, BR_W), lambda b, n: (layer, 0, 0)),
                 const(dec.shape), const(kfac.shape), const(cfac.shape), const(gch.shape)]
    args += [g_ret.reshape(depth, 1, BR_W), dec, kfac, cfac, gch]
    return pl.pallas_call(
        functools.partial(_ret_kernel, c=c, cp=cp, has_init=has_init),
        grid=(nb_batch, nc),
        in_specs=in_specs,
        out_specs=[
            pl.BlockSpec((c, BR_W), lambda b, n: (b * nc + n, 0)),
            pl.BlockSpec((None, npair, LANES, RET_DV), lambda b, n: (b, 0, 0, 0)),
        ],
        out_shape=[
            jax.ShapeDtypeStruct((m, BR_W), F32 if has_init else BF16),
            jax.ShapeDtypeStruct((nb_batch, npair, LANES, RET_DV), F32),
        ],
        compiler_params=_params("arbitrary", "arbitrary"),
        name="retention",
    )(*args)


def _merge_kernel(y0, y1, y2, y3, gate_ref, w_ref, o_ref, *rest):
    wb_ref, acc_ref = rest if len(rest) == 2 else (None, rest[0])
    r, c = pl.program_id(1), pl.program_id(2)
    if wb_ref is not None:
        wb_ref[...] = w_ref[...].astype(BF16)
        w_ref = wb_ref
    for k, y_ref in enumerate((y0, y1, y2, y3)):
        @pl.when(r == k)
        def _(k=k, y_ref=y_ref):
            val = gate_ref[...].astype(F32) * _dot(y_ref[...].astype(BF16), w_ref[...])
            if k == 0:
                acc_ref[c] = val
            elif k < N_BR - 1:
                acc_ref[c] += val
            else:
                o_ref[...] = (acc_ref[c] + val).astype(o_ref.dtype)


def _merge_call(ys, gates, w, tm):
    m = gates.shape[1]
    _single_tile(m, tm, w)
    tn = MERGE_TN
    nc = D_MODEL // tn
    y_spec = pl.BlockSpec((tm, BR_W), lambda i, r, c: (i, 0))
    wblock, windex = (None, BR_W, tn), lambda i, r, c: (r, 0, c)
    out_specs = [pl.BlockSpec((tm, tn), lambda i, r, c: (i, jnp.where(r == N_BR - 1, c, 0)))]
    out_shape = [jax.ShapeDtypeStruct((m, D_MODEL), BF16)]
    if w.master:
        spec, shape = w.emit(wblock, windex)
        out_specs.append(spec)
        out_shape.append(shape)
    return pl.pallas_call(
        _merge_kernel,
        grid=(m // tm, N_BR, nc),
        in_specs=[y_spec] * N_BR + [
            pl.BlockSpec((None, tm, tn), lambda i, r, c: (nc * r + c, i, 0)),
            w.spec(wblock, windex),
        ],
        out_specs=out_specs,
        out_shape=out_shape,
        scratch_shapes=[pltpu.VMEM((nc, tm, tn), F32)],
        compiler_params=_params("arbitrary", "arbitrary", "arbitrary"),
        name="merge",
    )(*ys, gates, w.arr)


def _outproj_kernel(m_ref, w_ref, x_ref, gt_ref, o_ref, wb_ref=None):
    o_ref[...] = x_ref[...] + gt_ref[...] * _dot(m_ref[...], _load_weight(w_ref, wb_ref))


def _outproj_call(merged, x, mod, w, tm):
    m, d = x.shape
    _single_tile(m, tm, w)
    wblock, windex = (d, SLAB), lambda i, c: (0, c)
    out_specs = [pl.BlockSpec((tm, SLAB), lambda i, c: (i, c))]
    out_shape = [jax.ShapeDtypeStruct((m, d), F32)]
    if w.master:
        spec, shape = w.emit(wblock, windex)
        out_specs.append(spec)
        out_shape.append(shape)
    return pl.pallas_call(
        _outproj_kernel,
        grid=(m // tm, d // SLAB),
        in_specs=[
            pl.BlockSpec((tm, d), lambda i, c: (i, 0)),
            w.spec(wblock, windex),
            pl.BlockSpec((tm, SLAB), lambda i, c: (i, c)),
            mod.spec(2, tm, width=SLAB, col=1),
        ],
        out_specs=out_specs,
        out_shape=out_shape,
        compiler_params=_params("arbitrary", "arbitrary"),
        name="outproj",
    )(merged, w.arr, x, mod.arr)


def _mlp_kernel(x_ref, sc_ref, sh_ref, gt_ref, g_ref, w1_ref, w2_ref, o_ref, *rest, rc, nf):
    (w1b_ref, w2b_ref, h_ref) = rest if len(rest) == 3 else (None, None, rest[0])
    f = pl.program_id(1)

    @pl.when(f == 0)
    def _():
        _modnorm_to(h_ref, x_ref, sc_ref, sh_ref, g_ref, rc)
        o_ref[...] = jnp.zeros(o_ref.shape, F32)

    a = _dot(h_ref[...], _load_weight(w1_ref, w1b_ref))
    a = jnp.square(jnp.maximum(a, 0.0)).astype(BF16)
    if w2b_ref is not None:
        w2b_ref[...] = w2_ref[...].astype(BF16)
        w2_ref = w2b_ref
    for c0 in range(0, o_ref.shape[1], SLAB):
        o_ref[:, c0:c0 + SLAB] += _dot(a, w2_ref[:, c0:c0 + SLAB])

    @pl.when(f == nf - 1)
    def _():
        def body(cidx, carry):
            r = pl.multiple_of(cidx * rc, rc)
            rows = pl.ds(r, rc)
            o_ref[rows, :] = x_ref[rows, :] + _mod_rows(gt_ref, r, rc) * o_ref[rows, :]
            return carry

        lax.fori_loop(0, x_ref.shape[0] // rc, body, 0)


def _mlp_call(x, mod, g_norm, w1, w2, layer, tm, tf):
    m, d = x.shape
    _single_tile(m, tm, w1)
    assert w1.master == w2.master
    depth = g_norm.shape[0]
    nf = w1.shape[1] // tf
    w1block, w1index = (d, tf), lambda i, f: (0, f)
    w2block, w2index = (tf, d), lambda i, f: (f, 0)
    out_specs = [pl.BlockSpec((tm, d), lambda i, f: (i, 0))]
    out_shape = [jax.ShapeDtypeStruct((m, d), F32)]
    if w1.master:
        for spec, shape in (w1.emit(w1block, w1index), w2.emit(w2block, w2index)):
            out_specs.append(spec)
            out_shape.append(shape)
    return pl.pallas_call(
        functools.partial(_mlp_kernel, rc=_row_chunk(tm), nf=nf),
        grid=(m // tm, nf),
        in_specs=[
            pl.BlockSpec((tm, d), lambda i, f: (i, 0), pipeline_mode=pl.Buffered(1)),
            mod.spec(4, tm), mod.spec(3, tm), mod.spec(5, tm),
            pl.BlockSpec((None, 1, d), lambda i, f: (layer, 0, 0)),
            w1.spec(w1block, w1index),
            w2.spec(w2block, w2index),
        ],
        out_specs=out_specs,
        out_shape=out_shape,
        scratch_shapes=[pltpu.VMEM((tm, d), BF16)],
        compiler_params=_params("arbitrary", "arbitrary"),
        name="mlp",
    )(x, mod.arr, mod.arr, mod.arr, g_norm.reshape(depth, 1, d), w1.arr, w2.arr)


def _layer(x, mod, p, w, layer, nb_batch, t_len, tm_in, tm, tf, cache):
    copies = {}

    def split(outs, *names):
        outs = list(outs)
        for name in reversed(names):
            if w[name].master:
                copies[name] = outs.pop()
        return outs if len(outs) > 1 else outs[0]

    z3, h = split(_inproj_call(x, mod, p["g_norm1"], w["in_mix"], layer, tm_in), "in_mix")
    gates = split(_gate_call(h, w["in_gate"], tm_in), "in_gate")
    if cache is None:
        attn_cache = conv_init = pool_init = ret_init = None
        pos0 = 0
    else:
        attn_cache, conv_init, pool_init, ret_init = cache
        pos0 = PAST_LEN
    y_att, k_norm = _attn_call(z3, p["attn_sinks"], p["gq_t"], p["gk_t"], layer, nb_batch, t_len, attn_cache)
    y_conv, conv_tail = _conv_call(z3, p["w_dw"], p["b_dw"], p["g_conv_ln"], p["b_conv_ln"], layer, nb_batch, t_len,
                                   conv_init)
    y_pool = _pool_call(z3, p["w_pool"], p["s_pool"], layer, nb_batch, t_len, pos0, pool_init)
    y_ret, s_new = _ret_call(z3, p["g_ret_norm"], layer, nb_batch, t_len, ret_init)
    merged = split(_merge_call((y_att, y_conv, y_pool, y_ret), gates, w["br"], tm), "br")
    x = split(_outproj_call(merged, x, mod, w["out"], tm_in), "out")
    x = split(_mlp_call(x, mod, p["g_norm2"], w["mlp1"], w["mlp2"], layer, tm, tf), "mlp1", "mlp2")
    return x, (z3, k_norm, conv_tail, s_new), copies


def kernel(x_prompt, x_sample, c_prompt, c_sample, cache_attn_k, cache_attn_v, state_conv, state_pool, state_ret,
           w_ada, b_ada, g_norm1, g_norm2, w_in, g_qnorm, g_knorm, attn_sinks, w_dw, b_dw, g_conv_ln, b_conv_ln,
           w_pool, s_pool, g_ret_norm, w_br, w_out, w_mlp1, w_mlp2):
    nb, t_len, d = x_prompt.shape
    nsb, st_len, _ = x_sample.shape
    depth = w_ada.shape[0]
    mp, ms = nb * t_len, nsb * st_len

    n_c = nb + nsb
    r_pad = -(-n_c // 16) * 16
    c_all = jnp.concatenate([c_prompt, c_sample, jnp.zeros((r_pad - n_c, d), F32)], axis=0)
    ada = _ada_call(c_all, w_ada, b_ada)
    ada_p = ada[:, :nb].reshape(depth, nb, 1, 6 * d)
    ada_s = jnp.repeat(ada[:, nb:n_c], st_len, axis=1)

    p = dict(g_norm1=g_norm1, g_norm2=g_norm2, attn_sinks=attn_sinks, w_dw=w_dw, b_dw=b_dw,
             g_conv_ln=g_conv_ln, b_conv_ln=b_conv_ln, w_pool=w_pool, s_pool=s_pool, g_ret_norm=g_ret_norm,
             gq_t=jnp.tile(g_qnorm, (1, SLAB // HEAD_DIM)).reshape(depth, 1, SLAB),
             gk_t=jnp.tile(g_knorm, (1, N_KV)).reshape(depth, 1, KV_W))

    tm_p = 1024 if t_len % 1024 == 0 else t_len
    tm_p_in = 2048 if t_len % 2048 == 0 else tm_p
    tm_s = ms
    n_mix = SLAB_GATE * SLAB
    cache_k2 = cache_attn_k.reshape(depth, nsb, WINDOW, KV_W)
    cache_v2 = cache_attn_v.reshape(depth, nsb, WINDOW, KV_W)
    conv_init = jnp.pad(state_conv, ((0, 0), (0, 0), (CONV_HALO - (CONV_K - 1), 0), (0, 0)))
    pool_init = jnp.pad(state_pool, ((0, 0), (0, 0), (POOL_HALO - POOL_PAD, 0), (0, 0)))
    ret_init = state_ret.reshape(depth, nsb, RET_HEADS // 2, 2 * RET_DK, RET_DV)

    xp = x_prompt.reshape(mp, d)
    xs = x_sample.reshape(ms, d)
    st_p = [[] for _ in range(5)]
    st_s = [[] for _ in range(5)]
    for l in range(depth):
        masters = dict(in_mix=_Weight(w_in, l, shape=(d, n_mix)),
                       in_gate=_Weight(w_in, l, shape=(d, N_IN - n_mix), col0=SLAB_GATE),
                       br=_Weight(w_br, l), out=_Weight(w_out, l), mlp1=_Weight(w_mlp1, l), mlp2=_Weight(w_mlp2, l))
        mod_s = _Mod(ada_s, l, True, st_len)
        cache = ((cache_k2, cache_v2), conv_init, pool_init, ret_init)
        xs, (z3, k_norm, conv_tail, s_new), copies = _layer(xs, mod_s, p, masters, l, nsb, st_len, tm_s, tm_s,
                                                            MLP_TF_MASTER, cache)
        k_new = k_norm.reshape(nsb, st_len, N_KV, HEAD_DIM)
        v_new = z3[SLAB_KV].reshape(nsb, st_len, SLAB)[:, :, KV_W:].reshape(nsb, st_len, N_KV, HEAD_DIM)
        st_s[0].append(jnp.concatenate([cache_attn_k[l], k_new], axis=1)[:, -WINDOW:])
        st_s[1].append(jnp.concatenate([cache_attn_v[l], v_new], axis=1)[:, -WINDOW:])
        st_s[2].append(conv_tail[:, CONV_HALO - (CONV_K - 1):])
        pool_u = jnp.moveaxis(z3[SLAB_POOL:SLAB_POOL + 2].reshape(2, nsb, st_len, SLAB), 0, 2).reshape(nsb, st_len, BR_W)
        st_s[3].append(jnp.concatenate([state_pool[l], pool_u], axis=1)[:, -POOL_PAD:])
        st_s[4].append(s_new.reshape(nsb, RET_HEADS, RET_DK, RET_DV))

        mod_p = _Mod(ada_p, l, False, t_len)
        wcopy = {name: _Weight(arr) for name, arr in copies.items()}
        xp, (z3, k_norm, conv_tail, s_new), _ = _layer(xp, mod_p, p, wcopy, l, nb, t_len, tm_p_in, tm_p, MLP_TF, None)
        z4 = z3.reshape(SLAB_GATE, nb, t_len, SLAB)
        st_p[0].append(k_norm.reshape(nb, t_len, KV_W)[:, -WINDOW:].reshape(nb, WINDOW, N_KV, HEAD_DIM))
        st_p[1].append(z4[SLAB_KV, :, -WINDOW:, KV_W:].reshape(nb, WINDOW, N_KV, HEAD_DIM))
        st_p[2].append(conv_tail[:, CONV_HALO - (CONV_K - 1):])
        pool_u = z4[SLAB_POOL:SLAB_POOL + 2, :, -POOL_PAD:]
        st_p[3].append(jnp.moveaxis(pool_u, 0, 2).reshape(nb, POOL_PAD, BR_W))
        st_p[4].append(s_new.reshape(nb, RET_HEADS, RET_DK, RET_DV))

    return (xp.reshape(nb, t_len, d), xs.reshape(nsb, st_len, d),
            *[jnp.stack(a) for a in st_p], *[jnp.stack(a) for a in st_s])
```

```python
import functools

import numpy as np
import jax
import jax.numpy as jnp
from jax import lax
from jax.experimental import pallas as pl
from jax.experimental.pallas import tpu as pltpu

F32 = jnp.float32
BF16 = jnp.bfloat16

D_MODEL = 2048
PAST_LEN = 16384
N_HEADS = 16
HEAD_DIM = 64
N_KV = 4
WINDOW = 128
BR_W = 1024
CONV_K = 31
POOL_WINDOWS = (2, 4, 8, 16)
POOL_G = 256
POOL_PAD = 15
RET_HEADS = 8
RET_DK = 64
RET_DV = 128
RET_CHUNK = 128
N_BR = 4
D_FF = 4 * D_MODEL
MLP_TF = 1024
MLP_TF_MASTER = 512
MERGE_TN = 1024
EPS = 1e-6
KV_W = N_KV * HEAD_DIM
N_IN = 15872

SLAB = 512
N_SLAB = N_IN // SLAB
SLAB_Q, SLAB_KV, SLAB_CONV, SLAB_POOL, SLAB_RQ, SLAB_RK, SLAB_RV, SLAB_RG, SLAB_GATE = 0, 2, 3, 7, 9, 10, 11, 13, 15

VMEM_LIMIT_BYTES = 56 * 1024 * 1024
LANES = 128
SUBLANES = 8
NEG_BIG = -1e30
CONV_HALO = 32
POOL_HALO = 16


def _params(*sem):
    return pltpu.CompilerParams(dimension_semantics=sem, vmem_limit_bytes=VMEM_LIMIT_BYTES)


def _nt_dot(a, b):
    return lax.dot_general(a, b, (((1,), (1,)), ((), ())), preferred_element_type=F32)


def _tn_dot(a, b):
    return lax.dot_general(a, b, (((0,), (0,)), ((), ())), preferred_element_type=F32)


def _dot(a, b):
    return jnp.dot(a, b, preferred_element_type=F32)


def _silu(x):
    return x * jax.nn.sigmoid(x)


def _ada_kernel(c_ref, w_ref, b_ref, o_ref):
    s = _silu(c_ref[...]).astype(BF16)
    o_ref[...] = _dot(s, w_ref[...].astype(BF16)) + b_ref[...]


def _ada_call(c_all, w_ada, b_ada):
    depth, d, n = w_ada.shape
    r = c_all.shape[0]
    tn = 1024
    return pl.pallas_call(
        _ada_kernel,
        grid=(depth, n // tn),
        in_specs=[
            pl.BlockSpec((r, d), lambda l, j: (0, 0)),
            pl.BlockSpec((None, d, tn), lambda l, j: (l, 0, j)),
            pl.BlockSpec((None, 1, tn), lambda l, j: (l, 0, j)),
        ],
        out_specs=pl.BlockSpec((None, r, tn), lambda l, j: (l, 0, j)),
        out_shape=jax.ShapeDtypeStruct((depth, r, n), F32),
        compiler_params=_params("arbitrary", "arbitrary"),
        name="ada",
    )(c_all, w_ada, b_ada.reshape(depth, 1, n))


class _Mod:
    def __init__(self, arr, layer, per_row, t_len):
        self.arr, self.l, self.per_row, self.t_len = arr, layer, per_row, t_len

    def spec(self, k, tm, width=D_MODEL, col=None):
        l, tpb, nb = self.l, max(self.t_len // tm, 1), D_MODEL // width

        def cidx(idx):
            return k * nb + (idx[col] if col is not None else 0)

        if self.per_row:
            return pl.BlockSpec((None, tm, width), lambda *idx: (l, idx[0], cidx(idx)))
        return pl.BlockSpec((None, None, 1, width), lambda *idx: (l, idx[0] // tpb, 0, cidx(idx)))


def _mod_rows(ref, r, rc):
    return ref[...] if ref.shape[0] == 1 else ref[pl.ds(r, rc), :]


def _modnorm_to(h_ref, x_ref, sc_ref, sh_ref, g_ref, rc):
    g = g_ref[...]

    def body(c, carry):
        r = pl.multiple_of(c * rc, rc)
        x = x_ref[pl.ds(r, rc), :]
        ms = jnp.mean(x * x, axis=-1, keepdims=True)
        y = x * lax.rsqrt(ms + EPS) * g
        h = y * (1.0 + _mod_rows(sc_ref, r, rc)) + _mod_rows(sh_ref, r, rc)
        h_ref[pl.ds(r, rc), :] = h.astype(h_ref.dtype)
        return carry

    lax.fori_loop(0, x_ref.shape[0] // rc, body, 0)


def _row_chunk(tm):
    return 128 if tm % 128 == 0 else tm


class _Weight:
    def __init__(self, arr, layer=None, shape=None, col0=0):
        self.arr, self.layer, self.col0 = arr, layer, col0
        self.shape = tuple(shape if shape is not None else (arr.shape[1:] if layer is not None else arr.shape))

    @property
    def master(self):
        return self.layer is not None

    def spec(self, block, index):
        if not self.master:
            return pl.BlockSpec(block, index)
        layer, col0 = self.layer, self.col0

        def master_index(*g):
            idx = tuple(index(*g))
            return (layer,) + idx[:-1] + (idx[-1] + col0,)

        return pl.BlockSpec((None,) + tuple(block), master_index)

    def emit(self, block, index):
        return pl.BlockSpec(block, index), jax.ShapeDtypeStruct(self.shape, BF16)


def _load_weight(w_ref, copy_ref):
    if copy_ref is None:
        return w_ref[...]
    w = w_ref[...].astype(BF16)
    copy_ref[...] = w
    return w


def _single_tile(m, tm, w):
    assert not w.master or m == tm, "a master weight must be streamed by a single row tile"


def _inproj_kernel(x_ref, sc_ref, sh_ref, g_ref, w_ref, o_ref, h_ref, wb_ref=None, *, rc):
    @pl.when(pl.program_id(1) == 0)
    def _():
        _modnorm_to(h_ref, x_ref, sc_ref, sh_ref, g_ref, rc)

    o_ref[...] = _dot(h_ref[...], _load_weight(w_ref, wb_ref))


def _inproj_call(x, mod, g_norm, w, layer, tm):
    m, d = x.shape
    _single_tile(m, tm, w)
    depth = g_norm.shape[0]
    wblock, windex = (d, SLAB), lambda i, j: (0, j)
    out_specs = [pl.BlockSpec((None, tm, SLAB), lambda i, j: (j, i, 0)), pl.BlockSpec((tm, d), lambda i, j: (i, 0))]
    out_shape = [jax.ShapeDtypeStruct((SLAB_GATE, m, SLAB), F32), jax.ShapeDtypeStruct((m, d), BF16)]
    if w.master:
        spec, shape = w.emit(wblock, windex)
        out_specs.append(spec)
        out_shape.append(shape)
    return pl.pallas_call(
        functools.partial(_inproj_kernel, rc=_row_chunk(tm)),
        grid=(m // tm, SLAB_GATE),
        in_specs=[
            pl.BlockSpec((tm, d), lambda i, j: (i, 0), pipeline_mode=pl.Buffered(1)),
            mod.spec(1, tm), mod.spec(0, tm),
            pl.BlockSpec((None, 1, d), lambda i, j: (layer, 0, 0)),
            w.spec(wblock, windex),
        ],
        out_specs=out_specs,
        out_shape=out_shape,
        compiler_params=_params("arbitrary", "arbitrary"),
        name="inproj",
    )(x, mod.arr, mod.arr, g_norm.reshape(depth, 1, d), w.arr)


def _gate_kernel(h_ref, w_ref, o_ref, wb_ref=None):
    z = _dot(h_ref[...], _load_weight(w_ref, wb_ref))
    o_ref[...] = (0.5 * jnp.tanh(0.5 * z) + 0.5).astype(o_ref.dtype)


def _gate_call(h, w, tm):
    m, d = h.shape
    _single_tile(m, tm, w)
    per = MERGE_TN // SLAB
    wblock, windex = (d, SLAB), lambda i, j: (0, j)
    out_specs = [pl.BlockSpec((None, tm, SLAB), lambda i, j: (j // per, i, j % per))]
    out_shape = [jax.ShapeDtypeStruct(((N_SLAB - SLAB_GATE) // per, m, MERGE_TN), BF16)]
    if w.master:
        spec, shape = w.emit(wblock, windex)
        out_specs.append(spec)
        out_shape.append(shape)
    return pl.pallas_call(
        _gate_kernel,
        grid=(m // tm, N_SLAB - SLAB_GATE),
        in_specs=[pl.BlockSpec((tm, d), lambda i, j: (i, 0)), w.spec(wblock, windex)],
        out_specs=out_specs,
        out_shape=out_shape,
        compiler_params=_params("arbitrary", "arbitrary"),
        name="ingate",
    )(h, w.arr)


def _group_sum_matrix(width, group):
    idx = np.arange(width) // group
    return jnp.asarray((idx[:, None] == idx[None, :]).astype(np.float32), dtype=BF16)


SHORT_GROUP = 4


def _group_size(nb_batch, nblocks, has_init):
    return SHORT_GROUP if has_init and nblocks == 1 and nb_batch % SHORT_GROUP == 0 else 1


def _attn_kernel(*refs, layer, tq, has_init, group):
    if group == 1:
        blocks = [_attn_block(*refs, layer=layer, tq=tq, has_init=has_init)]
    else:
        sinks_ref, q_ref, kvc_ref, kinit_ref, vinit_ref, gq_ref, gk_ref, gm512_ref, gm256_ref, sd_ref, o_ref, kn_ref = refs
        blocks = []
        for bi in range(group):
            rows = pl.ds(bi * tq, tq)
            blocks.append(_attn_block(
                sinks_ref, q_ref.at[:, rows, :], kvc_ref.at[rows, :], kinit_ref.at[bi], vinit_ref.at[bi],
                gq_ref, gk_ref, gm512_ref, gm256_ref, sd_ref, o_ref.at[rows, :], kn_ref.at[rows, :],
                layer=layer, tq=tq, has_init=True))
    while blocks:
        blocks = [b for b in blocks if next(b, _DONE) is not _DONE]


_DONE = object()


def _attn_block(*refs, layer, tq, has_init):
    if has_init:
        sinks_ref, q_ref, kvc_ref, kinit_ref, vinit_ref, gq_ref, gk_ref, gm512_ref, gm256_ref, sd_ref, o_ref, kn_ref = refs
    else:
        sinks_ref, q_ref, kvc_ref, kvp_ref, gq_ref, gk_ref, gm512_ref, gm256_ref, sd_ref, o_ref, kn_ref = refs
    n = pl.program_id(1)

    def qk_norm(x, gmat, g):
        x2 = x * x
        hi = x2.astype(BF16)
        lo = (x2 - hi.astype(F32)).astype(BF16)
        ss = _dot(hi, gmat) + _dot(lo, gmat)
        return x * lax.rsqrt(ss * (1.0 / HEAD_DIM) + EPS) * g

    gm256, gk = gm256_ref[...], gk_ref[...]
    kvc = kvc_ref[...]
    kc = qk_norm(kvc[:, :KV_W], gm256, gk)
    kn_ref[...] = kc
    vc = kvc[:, KV_W:]
    if has_init:
        kp, vp = kinit_ref[...], vinit_ref[...]
    else:
        kvp = kvp_ref[...]
        kp, vp = qk_norm(kvp[:, :KV_W], gm256, gk), kvp[:, KV_W:]
    if tq < WINDOW:
        pad = jnp.zeros((WINDOW - tq, KV_W), F32)
        kc = jnp.concatenate([kc, pad], axis=0)
        vc = jnp.concatenate([vc, pad], axis=0)
    kall = jnp.concatenate([kp, kc], axis=0)
    vall = jnp.concatenate([vp, vc], axis=0)

    nk = 2 * WINDOW
    ii = lax.broadcasted_iota(jnp.int32, (tq, nk), 0)
    jj = lax.broadcasted_iota(jnp.int32, (tq, nk), 1)
    dist = WINDOW + ii - jj
    valid = (dist >= 0) & (dist <= WINDOW)
    if not has_init:
        valid = valid & (jj >= jnp.where(n > 0, 0, WINDOW))
    distm = jnp.where(valid, dist.astype(F32), -NEG_BIG)

    gm512, gq = gm512_ref[...], gq_ref[...] * (HEAD_DIM ** -0.5)
    qn = [qk_norm(q_ref[s], gm512, gq) for s in range(2)]
    lane_k = lax.broadcasted_iota(jnp.int32, (nk, LANES), 1) < HEAD_DIM
    lane_q = lax.broadcasted_iota(jnp.int32, (tq, LANES), 1) < HEAD_DIM
    sd = sd_ref[...]

    def two_copies(a, upper):
        if upper:
            bot = jnp.where(lane_k, 0.0, a)
            top = pltpu.roll(bot, HEAD_DIM, 1)
        else:
            top = jnp.where(lane_k, a, 0.0)
            bot = pltpu.roll(top, HEAD_DIM, 1)
        return jnp.concatenate([top, bot], axis=0).astype(BF16)

    kds, vds = [], []
    for kv in range(N_KV):
        cs = slice(LANES * (kv // 2), LANES * (kv // 2) + LANES)
        kds.append(two_copies(kall[:, cs], bool(kv % 2)))
        vds.append(two_copies(vall[:, cs], bool(kv % 2)))
    npair = N_HEADS // 2
    yield
    scores = []
    for p in range(npair):
        off = LANES * (p % 4)
        qp = qn[p // 4][:, off:off + LANES].astype(BF16)
        scores.append(_nt_dot(qp, kds[p // 2]))
    yield
    probs, sinkw = [], []
    for p in range(npair):
        es, sk = [], []
        for hh in range(2):
            h = 2 * p + hh
            slope = 2.0 ** (-8.0 * (h + 1) / N_HEADS)
            sink = sinks_ref[layer, h]
            sh = scores[p][:, nk * hh:nk * hh + nk] - slope * distm
            mx = jnp.maximum(jnp.max(sh, axis=-1, keepdims=True), sink)
            es.append(jnp.exp(sh - mx))
            sk.append(jnp.exp(sink - mx))
        probs.append(jnp.concatenate(es, axis=1).astype(BF16))
        sinkw.append(jnp.where(lane_q, sk[0], sk[1]))
    yield
    for p in range(npair):
        num = _dot(probs[p], vds[p // 2])
        den = _dot(probs[p], sd) + sinkw[p]
        o_ref[:, LANES * p:LANES * p + LANES] = (num / den).astype(o_ref.dtype)


def _attn_call(z3, sinks, gq_t, gk_t, layer, nb_batch, t_len, cache=None):
    m = z3.shape[1]
    has_init = cache is not None
    tq = WINDOW if t_len % WINDOW == 0 else t_len
    nb = t_len // tq
    assert nb == 1 or not has_init
    depth = gq_t.shape[0]
    gm512 = _group_sum_matrix(SLAB, HEAD_DIM)
    gm256 = _group_sum_matrix(KV_W, HEAD_DIM)
    sd_np = np.zeros((4 * WINDOW, LANES), np.float32)
    sd_np[:2 * WINDOW, :HEAD_DIM] = 1.0
    sd_np[2 * WINDOW:, HEAD_DIM:] = 1.0
    sd = jnp.asarray(sd_np, dtype=BF16)

    group = _group_size(nb_batch, nb, has_init)
    rows = group * tq
    in_specs = [
        pl.BlockSpec(memory_space=pltpu.SMEM),
        pl.BlockSpec((2, rows, SLAB), lambda b, n: (0, b * nb + n, 0)),
        pl.BlockSpec((None, rows, SLAB), lambda b, n: (SLAB_KV, b * nb + n, 0)),
    ]
    args = [sinks, z3, z3]
    if has_init:
        cache_spec = pl.BlockSpec((None, group, WINDOW, KV_W), lambda b, n: (layer, b, 0, 0))
        if group == 1:
            cache_spec = pl.BlockSpec((None, None, WINDOW, KV_W), lambda b, n: (layer, b, 0, 0))
        in_specs += [cache_spec] * 2
        args += [cache[0], cache[1]]
    else:
        in_specs += [pl.BlockSpec((None, WINDOW, SLAB), lambda b, n: (SLAB_KV, jnp.maximum(b * nb + n - 1, 0), 0))]
        args += [z3]
    in_specs += [
        pl.BlockSpec((None, 1, SLAB), lambda b, n: (layer, 0, 0)),
        pl.BlockSpec((None, 1, KV_W), lambda b, n: (layer, 0, 0)),
        pl.BlockSpec((SLAB, SLAB), lambda b, n: (0, 0)),
        pl.BlockSpec((KV_W, KV_W), lambda b, n: (0, 0)),
        pl.BlockSpec((4 * WINDOW, LANES), lambda b, n: (0, 0)),
    ]
    args += [gq_t, gk_t, gm512, gm256, sd]
    out_dtype = F32 if has_init else BF16
    return pl.pallas_call(
        functools.partial(_attn_kernel, layer=layer, tq=tq, has_init=has_init, group=group),
        grid=(nb_batch // group, nb),
        in_specs=in_specs,
        out_specs=[
            pl.BlockSpec((rows, BR_W), lambda b, n: (b * nb + n, 0)),
            pl.BlockSpec((rows, KV_W), lambda b, n: (b * nb + n, 0)),
        ],
        out_shape=[jax.ShapeDtypeStruct((m, BR_W), out_dtype), jax.ShapeDtypeStruct((m, KV_W), F32)],
        compiler_params=_params("arbitrary", "arbitrary"),
        name="attn",
    )(*args)


def _conv_kernel(*refs, tt, nt, has_init):
    if has_init:
        l0, l1, g0, g1, init_ref, w_ref, b_ref, gl_ref, bl_ref, o_ref, new_ref, ext, ybuf, shifted = refs
    else:
        l0, l1, g0, g1, w_ref, b_ref, gl_ref, bl_ref, o_ref, new_ref, ext, ybuf, shifted = refs
    t = pl.program_id(1)

    @pl.when(t == 0)
    def _():
        ext[0:CONV_HALO, :] = init_ref[...] if has_init else jnp.zeros((CONV_HALO, BR_W), F32)

    if nt > 1:
        @pl.when(t > 0)
        def _():
            ext[0:CONV_HALO, :] = ext[tt:tt + CONV_HALO, :]

    for cb, (lr, gr) in enumerate(((l0, g0), (l1, g1))):
        ext[CONV_HALO:CONV_HALO + tt, SLAB * cb:SLAB * cb + SLAB] = lr[...] * jax.nn.sigmoid(gr[...])
    new_ref[...] = ext[tt:tt + CONV_HALO, :]

    rs = min(tt, 32)
    base = CONV_HALO - (CONV_K - 1)
    nsh = shifted.shape[1]
    for c in range(BR_W // LANES):
        cs = slice(LANES * c, LANES * c + LANES)
        for r in range(1, SUBLANES):
            shifted[r - 1] = ext[r:r + nsh, cs]

        def taps(i, carry, cs=cs):
            r0 = pl.multiple_of(i * rs, rs)
            acc = jnp.broadcast_to(b_ref[:, cs], (rs, LANES))
            for r in range(SUBLANES):
                offs = [(base + k) // SUBLANES for k in range(CONV_K) if (base + k) % SUBLANES == r]
                rows = pl.ds(r0 + SUBLANES * offs[0], rs + SUBLANES * (offs[-1] - offs[0]))
                win = ext[rows, cs] if r == 0 else shifted[r - 1, rows, :]
                for a in offs:
                    k = SUBLANES * a + r - base
                    d = SUBLANES * (a - offs[0])
                    acc = acc + w_ref[k:k + 1, cs] * win[d:d + rs]
            ybuf[pl.ds(r0, rs), cs] = acc
            return carry

        lax.fori_loop(0, tt // rs, taps, 0)

    def norm_act(i, carry):
        rows = pl.ds(pl.multiple_of(i * rs, rs), rs)
        y = ybuf[rows, :]
        yc = y - jnp.mean(y, axis=-1, keepdims=True)
        var = jnp.mean(yc * yc, axis=-1, keepdims=True)
        yn = yc * lax.rsqrt(var + EPS) * gl_ref[...] + bl_ref[...]
        o_ref[rows, :] = _silu(yn).astype(o_ref.dtype)
        return carry

    lax.fori_loop(0, tt // rs, norm_act, 0, unroll=min(4, tt // rs))


def _conv_short_kernel(l0, l1, g0, g1, init_ref, w_ref, b_ref, gl_ref, bl_ref, o_ref, new_ref, ext, ybuf, *,
                       nbat, t_len):
    ext[:, 0:CONV_HALO, :] = init_ref[...]
    for cb, (lr, gr) in enumerate(((l0, g0), (l1, g1))):
        u = lr[...] * jax.nn.sigmoid(gr[...])
        ext[:, CONV_HALO:CONV_HALO + t_len, SLAB * cb:SLAB * cb + SLAB] = u.reshape(nbat, t_len, SLAB)
    new_ref[...] = ext[:, t_len:t_len + CONV_HALO, :]
    base = CONV_HALO - (CONV_K - 1)
    for c in range(BR_W // LANES):
        cs = slice(LANES * c, LANES * c + LANES)
        acc = jnp.broadcast_to(b_ref[:, cs], (nbat, t_len, LANES))
        for k in range(CONV_K):
            acc = acc + w_ref[k:k + 1, cs] * ext[:, base + k:base + k + t_len, cs]
        ybuf[:, cs] = acc.reshape(nbat * t_len, LANES)
    rs = 32

    def norm_act(i, carry):
        rows = pl.ds(pl.multiple_of(i * rs, rs), rs)
        y = ybuf[rows, :]
        yc = y - jnp.mean(y, axis=-1, keepdims=True)
        var = jnp.mean(yc * yc, axis=-1, keepdims=True)
        yn = yc * lax.rsqrt(var + EPS) * gl_ref[...] + bl_ref[...]
        o_ref[rows, :] = _silu(yn).astype(o_ref.dtype)
        return carry

    lax.fori_loop(0, nbat * t_len // rs, norm_act, 0, unroll=min(4, nbat * t_len // rs))


def _short_batched(nb_batch, t_len, has_init):
    return has_init and t_len == SUBLANES and (nb_batch * t_len) % 32 == 0


def _conv_short_call(z3, init, w_dw, b_dw, g_ln, b_ln, layer, nb_batch, t_len):
    m = z3.shape[1]
    depth = w_dw.shape[0]
    slab = lambda j: pl.BlockSpec((None, m, SLAB), lambda i: (j, 0, 0))
    vec = lambda: pl.BlockSpec((None, 1, BR_W), lambda i: (layer, 0, 0))
    return pl.pallas_call(
        functools.partial(_conv_short_kernel, nbat=nb_batch, t_len=t_len),
        grid=(1,),
        in_specs=[slab(SLAB_CONV), slab(SLAB_CONV + 1), slab(SLAB_CONV + 2), slab(SLAB_CONV + 3),
                  pl.BlockSpec((None, nb_batch, CONV_HALO, BR_W), lambda i: (layer, 0, 0, 0)),
                  pl.BlockSpec((None, CONV_K, BR_W), lambda i: (layer, 0, 0)), vec(), vec(), vec()],
        out_specs=[pl.BlockSpec((m, BR_W), lambda i: (0, 0)),
                   pl.BlockSpec((nb_batch, CONV_HALO, BR_W), lambda i: (0, 0, 0))],
        out_shape=[jax.ShapeDtypeStruct((m, BR_W), F32), jax.ShapeDtypeStruct((nb_batch, CONV_HALO, BR_W), F32)],
        scratch_shapes=[pltpu.VMEM((nb_batch, CONV_HALO + t_len, BR_W), F32), pltpu.VMEM((m, BR_W), F32)],
        compiler_params=_params("arbitrary"),
        name="conv_short",
    )(z3, z3, z3, z3, init, w_dw, b_dw.reshape(depth, 1, BR_W), g_ln.reshape(depth, 1, BR_W),
      b_ln.reshape(depth, 1, BR_W))


def _conv_call(z3, w_dw, b_dw, g_ln, b_ln, layer, nb_batch, t_len, init=None):
    m = z3.shape[1]
    has_init = init is not None
    if _short_batched(nb_batch, t_len, has_init):
        return _conv_short_call(z3, init, w_dw, b_dw, g_ln, b_ln, layer, nb_batch, t_len)
    tt = 256 if t_len % 256 == 0 else t_len
    nt = t_len // tt
    assert nt == 1 or tt >= CONV_HALO
    depth = w_dw.shape[0]

    def slab(j):
        return pl.BlockSpec((None, tt, SLAB), lambda b, t: (j, b * nt + t, 0))

    def vec():
        return pl.BlockSpec((None, 1, BR_W), lambda b, t: (layer, 0, 0))

    in_specs = [slab(SLAB_CONV), slab(SLAB_CONV + 1), slab(SLAB_CONV + 2), slab(SLAB_CONV + 3)]
    args = [z3, z3, z3, z3]
    if has_init:
        in_specs.append(pl.BlockSpec((None, None, CONV_HALO, BR_W), lambda b, t: (layer, b, 0, 0)))
        args.append(init)
    in_specs += [pl.BlockSpec((None, CONV_K, BR_W), lambda b, t: (layer, 0, 0)), vec(), vec(), vec()]
    args += [w_dw, b_dw.reshape(depth, 1, BR_W), g_ln.reshape(depth, 1, BR_W), b_ln.reshape(depth, 1, BR_W)]
    return pl.pallas_call(
        functools.partial(_conv_kernel, tt=tt, nt=nt, has_init=has_init),
        grid=(nb_batch, nt),
        in_specs=in_specs,
        out_specs=[
            pl.BlockSpec((tt, BR_W), lambda b, t: (b * nt + t, 0)),
            pl.BlockSpec((None, CONV_HALO, BR_W), lambda b, t: (b, 0, 0)),
        ],
        out_shape=[
            jax.ShapeDtypeStruct((m, BR_W), F32 if has_init else BF16),
            jax.ShapeDtypeStruct((nb_batch, CONV_HALO, BR_W), F32),
        ],
        scratch_shapes=[pltpu.VMEM((CONV_HALO + tt, BR_W), F32), pltpu.VMEM((tt, BR_W), F32),
                        pltpu.VMEM((SUBLANES - 1, CONV_HALO + tt - SUBLANES, LANES), F32)],
        compiler_params=_params("arbitrary", "arbitrary"),
        name="conv",
    )(*args)


def _pool_kernel(*refs, tt, nt, has_init, pos0):
    if has_init:
        u0, u1, init_ref, w_ref, s_ref, o_ref, ext = refs
    else:
        u0, u1, w_ref, s_ref, o_ref, ext = refs
    t = pl.program_id(1)

    @pl.when(t == 0)
    def _():
        ext[0:POOL_HALO, :] = init_ref[...] if has_init else jnp.zeros((POOL_HALO, BR_W), F32)

    if nt > 1:
        @pl.when(t > 0)
        def _():
            ext[0:POOL_HALO, :] = ext[tt:tt + POOL_HALO, :]

    ext[POOL_HALO:POOL_HALO + tt, 0:SLAB] = u0[...]
    ext[POOL_HALO:POOL_HALO + tt, SLAB:2 * SLAB] = u1[...]
    pos = pos0 + t * tt + lax.broadcasted_iota(jnp.int32, (tt, 1), 0)
    for g, w in enumerate(POOL_WINDOWS):
        cs = slice(POOL_G * g, POOL_G * g + POOL_G)
        cur = ext[POOL_HALO:POOL_HALO + tt, cs]
        wsum = cur
        for s in range(1, w):
            wsum = wsum + ext[POOL_HALO - s:POOL_HALO - s + tt, cs]
        cnt = jnp.minimum(pos + 1, w).astype(F32)
        zg = wsum / cnt - cur
        y = _dot(zg.astype(BF16), w_ref[g].astype(BF16)) * s_ref[:, cs]
        o_ref[:, cs] = y.astype(o_ref.dtype)


def _pool_short_kernel(u0, u1, init_ref, w_ref, s_ref, o_ref, ext, *, nbat, t_len, pos0):
    ext[:, 0:POOL_HALO, :] = init_ref[...]
    ext[:, POOL_HALO:POOL_HALO + t_len, 0:SLAB] = u0[...].reshape(nbat, t_len, SLAB)
    ext[:, POOL_HALO:POOL_HALO + t_len, SLAB:2 * SLAB] = u1[...].reshape(nbat, t_len, SLAB)
    pos = pos0 + lax.broadcasted_iota(jnp.int32, (nbat, t_len, POOL_G), 1)
    for g, w in enumerate(POOL_WINDOWS):
        cs = slice(POOL_G * g, POOL_G * g + POOL_G)
        cur = ext[:, POOL_HALO:POOL_HALO + t_len, cs]
        wsum = cur
        for s in range(1, w):
            wsum = wsum + ext[:, POOL_HALO - s:POOL_HALO - s + t_len, cs]
        cnt = jnp.minimum(pos + 1, w).astype(F32)
        zg = (wsum / cnt - cur).reshape(nbat * t_len, POOL_G)
        y = _dot(zg.astype(BF16), w_ref[g].astype(BF16)) * s_ref[:, cs]
        o_ref[:, cs] = y.astype(o_ref.dtype)


def _pool_short_call(z3, init, w_pool, s_pool, layer, nb_batch, t_len, pos0):
    m = z3.shape[1]
    depth = w_pool.shape[0]
    slab = lambda j: pl.BlockSpec((None, m, SLAB), lambda i: (j, 0, 0))
    return pl.pallas_call(
        functools.partial(_pool_short_kernel, nbat=nb_batch, t_len=t_len, pos0=pos0),
        grid=(1,),
        in_specs=[slab(SLAB_POOL), slab(SLAB_POOL + 1),
                  pl.BlockSpec((None, nb_batch, POOL_HALO, BR_W), lambda i: (layer, 0, 0, 0)),
                  pl.BlockSpec((None, len(POOL_WINDOWS), POOL_G, POOL_G), lambda i: (layer, 0, 0, 0)),
                  pl.BlockSpec((None, 1, BR_W), lambda i: (layer, 0, 0))],
        out_specs=pl.BlockSpec((m, BR_W), lambda i: (0, 0)),
        out_shape=jax.ShapeDtypeStruct((m, BR_W), F32),
        scratch_shapes=[pltpu.VMEM((nb_batch, POOL_HALO + t_len, BR_W), F32)],
        compiler_params=_params("arbitrary"),
        name="pool_short",
    )(z3, z3, init, w_pool, s_pool.reshape(depth, 1, BR_W))


def _pool_call(z3, w_pool, s_pool, layer, nb_batch, t_len, pos0, init=None):
    m = z3.shape[1]
    has_init = init is not None
    if _short_batched(nb_batch, t_len, has_init):
        return _pool_short_call(z3, init, w_pool, s_pool, layer, nb_batch, t_len, pos0)
    tt = 256 if t_len % 256 == 0 else t_len
    nt = t_len // tt
    assert nt == 1 or tt >= POOL_HALO
    depth = w_pool.shape[0]

    def slab(j):
        return pl.BlockSpec((None, tt, SLAB), lambda b, t: (j, b * nt + t, 0))

    in_specs = [slab(SLAB_POOL), slab(SLAB_POOL + 1)]
    args = [z3, z3]
    if has_init:
        in_specs.append(pl.BlockSpec((None, None, POOL_HALO, BR_W), lambda b, t: (layer, b, 0, 0)))
        args.append(init)
    in_specs += [
        pl.BlockSpec((None, len(POOL_WINDOWS), POOL_G, POOL_G), lambda b, t: (layer, 0, 0, 0)),
        pl.BlockSpec((None, 1, BR_W), lambda b, t: (layer, 0, 0)),
    ]
    args += [w_pool, s_pool.reshape(depth, 1, BR_W)]
    return pl.pallas_call(
        functools.partial(_pool_kernel, tt=tt, nt=nt, has_init=has_init, pos0=pos0),
        grid=(nb_batch, nt),
        in_specs=in_specs,
        out_specs=pl.BlockSpec((tt, BR_W), lambda b, t: (b * nt + t, 0)),
        out_shape=jax.ShapeDtypeStruct((m, BR_W), F32 if has_init else BF16),
        scratch_shapes=[pltpu.VMEM((POOL_HALO + tt, BR_W), F32)],
        compiler_params=_params("arbitrary", "arbitrary"),
        name="pool",
    )(*args)


def _ret_constants(c, cp):
    lg = np.log1p(-np.exp2(-5.0 - np.arange(RET_HEADS, dtype=np.float64)))
    i = np.arange(c, dtype=np.float64)
    diff = i[:, None] - i[None, :]
    decay = np.where(diff >= 0, np.exp(lg[:, None, None] * np.maximum(diff, 0.0)), 0.0)
    dec = np.zeros((RET_HEADS // 2, c, 2 * cp))
    for h in range(RET_HEADS):
        dec[h // 2, :, (h % 2) * cp:(h % 2) * cp + c] = decay[h]
    kfac = np.repeat(np.exp(lg[None, :] * (c - 1 - i)[:, None]), RET_DK, axis=1) * RET_DK ** -0.5
    cfac = np.repeat(np.exp(lg[None, :] * (i + 1)[:, None]), RET_DV, axis=1)
    gch = np.repeat(np.exp(lg * c), RET_DK).reshape(RET_HEADS // 2, 2 * RET_DK, 1)
    gch = np.broadcast_to(gch, (RET_HEADS // 2, 2 * RET_DK, RET_DV))
    f = lambda a: jnp.asarray(np.ascontiguousarray(a), dtype=F32)
    return f(dec), f(kfac), f(cfac), f(gch)


def _ret_kernel(*refs, c, cp, has_init, group):
    if group == 1:
        return _ret_chunk(*refs, c=c, cp=cp, has_init=has_init)
    rq, rk, rv0, rv1, rg0, rg1, s0_ref, gn_ref, dec_ref, kf_ref, cf_ref, gch_ref, o_ref, s_ref = refs
    for bi in range(group):
        rows = pl.ds(bi * c, c)
        _ret_chunk(rq.at[rows, :], rk.at[rows, :], rv0.at[rows, :], rv1.at[rows, :], rg0.at[rows, :], rg1.at[rows, :],
                   s0_ref.at[bi], gn_ref, dec_ref, kf_ref, cf_ref, gch_ref, o_ref.at[rows, :], s_ref.at[bi],
                   c=c, cp=cp, has_init=True)


def _ret_chunk(*refs, c, cp, has_init):
    if has_init:
        rq, rk, rv0, rv1, rg0, rg1, s0_ref, gn_ref, dec_ref, kf_ref, cf_ref, gch_ref, o_ref, s_ref = refs
    else:
        rq, rk, rv0, rv1, rg0, rg1, gn_ref, dec_ref, kf_ref, cf_ref, gch_ref, o_ref, s_ref = refs
    n = pl.program_id(1)

    @pl.when(n == 0)
    def _():
        s_ref[...] = s0_ref[...] if has_init else jnp.zeros(s_ref.shape, F32)

    lo = lax.broadcasted_iota(jnp.int32, (c, LANES), 1) < RET_DK
    q = rq[...]
    kraw = rk[...]
    k = kraw * (RET_DK ** -0.5)
    kdec = kraw * kf_ref[...]
    rvs, rgs = (rv0, rv1), (rg0, rg1)

    def stack_heads(a):
        a0, a1 = jnp.where(lo, a, 0.0), jnp.where(lo, 0.0, a)
        if cp > c:
            z = jnp.zeros((cp - c, LANES), F32)
            return jnp.concatenate([a0, z, a1, z], axis=0)
        return jnp.concatenate([a0, a1], axis=0)

    npair = RET_HEADS // 2
    zc = jnp.zeros((c, LANES), F32)
    scores, vbds, vsts, inners, crosses = [], [], [], [], []
    for p in range(npair):
        cs = slice(LANES * p, LANES * p + LANES)
        scores.append(_nt_dot(q[:, cs].astype(BF16), stack_heads(k[:, cs]).astype(BF16)) * dec_ref[p])
        vs = [rvs[h // 4][:, LANES * (h % 4):LANES * (h % 4) + LANES] for h in (2 * p, 2 * p + 1)]
        rows0 = jnp.concatenate([vs[0], zc], axis=1)
        rows1 = jnp.concatenate([zc, vs[1]], axis=1)
        if cp > c:
            zp = jnp.zeros((cp - c, 2 * LANES), F32)
            zq = jnp.zeros((cp - c, LANES), F32)
            vbds.append(jnp.concatenate([rows0, zp, rows1, zp], axis=0).astype(BF16))
            vsts.append(jnp.concatenate([vs[0], zq, vs[1], zq], axis=0).astype(BF16))
        else:
            vbds.append(jnp.concatenate([rows0, rows1], axis=0).astype(BF16))
            vsts.append(jnp.concatenate([vs[0], vs[1]], axis=0).astype(BF16))
    for p in range(npair):
        cs = slice(LANES * p, LANES * p + LANES)
        qp = q[:, cs]
        inners.append(_dot(scores[p].astype(BF16), vbds[p]))
        sprev = s_ref[p]
        qst = jnp.concatenate([jnp.where(lo, qp, 0.0), jnp.where(lo, 0.0, qp)], axis=0)
        crosses.append(_dot(qst.astype(BF16), sprev.astype(BF16)))
        upd = _tn_dot(stack_heads(kdec[:, cs]).astype(BF16), vsts[p])
        s_ref[p] = gch_ref[p] * sprev + upd
    for p in range(npair):
        inner, cross = inners[p], crosses[p]
        for hh, h in enumerate((2 * p, 2 * p + 1)):
            hc = slice(LANES * h, LANES * h + LANES)
            o = inner[:, LANES * hh:LANES * hh + LANES] + cross[c * hh:c * hh + c, :] * cf_ref[:, hc]
            oc = o - jnp.mean(o, axis=-1, keepdims=True)
            var = jnp.mean(oc * oc, axis=-1, keepdims=True)
            gate = rgs[h // 4][:, LANES * (h % 4):LANES * (h % 4) + LANES]
            y = oc * lax.rsqrt(var + EPS) * gn_ref[:, hc] * _silu(gate)
            o_ref[:, hc] = y.astype(o_ref.dtype)


def _ret_call(z3, g_ret, layer, nb_batch, t_len, init=None):
    m = z3.shape[1]
    has_init = init is not None
    c = RET_CHUNK if t_len % RET_CHUNK == 0 else t_len
    cp = max(c, 64)
    nc = t_len // c
    depth = g_ret.shape[0]
    dec, kfac, cfac, gch = _ret_constants(c, cp)
    npair = RET_HEADS // 2

    group = _group_size(nb_batch, nc, has_init)
    state_block = (npair, LANES, RET_DV) if group == 1 else (group, npair, LANES, RET_DV)
    lead = (None,) if group == 1 else ()

    def slab(j):
        return pl.BlockSpec((None, group * c, SLAB), lambda b, n: (j, b * nc + n, 0))

    def const(shape):
        nd = len(shape)
        return pl.BlockSpec(shape, lambda b, n: (0,) * nd)

    in_specs = [slab(SLAB_RQ), slab(SLAB_RK), slab(SLAB_RV), slab(SLAB_RV + 1), slab(SLAB_RG), slab(SLAB_RG + 1)]
    args = [z3] * 6
    if has_init:
        in_specs.append(pl.BlockSpec((None,) + lead + state_block, lambda b, n: (layer, b, 0, 0, 0)))
        args.append(init)
    in_specs += [pl.BlockSpec((None, 1, BR_W), lambda b, n: (layer, 0, 0)),
                 const(dec.shape), const(kfac.shape), const(cfac.shape), const(gch.shape)]
    args += [g_ret.reshape(depth, 1, BR_W), dec, kfac, cfac, gch]
    return pl.pallas_call(
        functools.partial(_ret_kernel, c=c, cp=cp, has_init=has_init, group=group),
        grid=(nb_batch // group, nc),
        in_specs=in_specs,
        out_specs=[
            pl.BlockSpec((group * c, BR_W), lambda b, n: (b * nc + n, 0)),
            pl.BlockSpec(lead + state_block, lambda b, n: (b, 0, 0, 0)),
        ],
        out_shape=[
            jax.ShapeDtypeStruct((m, BR_W), F32 if has_init else BF16),
            jax.ShapeDtypeStruct((nb_batch, npair, LANES, RET_DV), F32),
        ],
        compiler_params=_params("arbitrary", "arbitrary"),
        name="retention",
    )(*args)


def _merge_kernel(y0, y1, y2, y3, gate_ref, w_ref, o_ref, *rest):
    wb_ref, acc_ref = rest if len(rest) == 2 else (None, rest[0])
    r, c = pl.program_id(1), pl.program_id(2)
    if wb_ref is not None:
        wb_ref[...] = w_ref[...].astype(BF16)
        w_ref = wb_ref
    for k, y_ref in enumerate((y0, y1, y2, y3)):
        @pl.when(r == k)
        def _(k=k, y_ref=y_ref):
            val = gate_ref[...].astype(F32) * _dot(y_ref[...].astype(BF16), w_ref[...])
            if k == 0:
                acc_ref[c] = val
            elif k < N_BR - 1:
                acc_ref[c] += val
            else:
                o_ref[...] = (acc_ref[c] + val).astype(o_ref.dtype)


def _merge_call(ys, gates, w, tm):
    m = gates.shape[1]
    _single_tile(m, tm, w)
    tn = MERGE_TN
    nc = D_MODEL // tn
    y_spec = pl.BlockSpec((tm, BR_W), lambda i, r, c: (i, 0))
    wblock, windex = (None, BR_W, tn), lambda i, r, c: (r, 0, c)
    out_specs = [pl.BlockSpec((tm, tn), lambda i, r, c: (i, jnp.where(r == N_BR - 1, c, 0)))]
    out_shape = [jax.ShapeDtypeStruct((m, D_MODEL), BF16)]
    if w.master:
        spec, shape = w.emit(wblock, windex)
        out_specs.append(spec)
        out_shape.append(shape)
    return pl.pallas_call(
        _merge_kernel,
        grid=(m // tm, N_BR, nc),
        in_specs=[y_spec] * N_BR + [
            pl.BlockSpec((None, tm, tn), lambda i, r, c: (nc * r + c, i, 0)),
            w.spec(wblock, windex),
        ],
        out_specs=out_specs,
        out_shape=out_shape,
        scratch_shapes=[pltpu.VMEM((nc, tm, tn), F32)],
        compiler_params=_params("arbitrary", "arbitrary", "arbitrary"),
        name="merge",
    )(*ys, gates, w.arr)


def _outproj_kernel(m_ref, w_ref, x_ref, gt_ref, o_ref, wb_ref=None):
    o_ref[...] = x_ref[...] + gt_ref[...] * _dot(m_ref[...], _load_weight(w_ref, wb_ref))


def _outproj_call(merged, x, mod, w, tm):
    m, d = x.shape
    _single_tile(m, tm, w)
    wblock, windex = (d, SLAB), lambda i, c: (0, c)
    out_specs = [pl.BlockSpec((tm, SLAB), lambda i, c: (i, c))]
    out_shape = [jax.ShapeDtypeStruct((m, d), F32)]
    if w.master:
        spec, shape = w.emit(wblock, windex)
        out_specs.append(spec)
        out_shape.append(shape)
    return pl.pallas_call(
        _outproj_kernel,
        grid=(m // tm, d // SLAB),
        in_specs=[
            pl.BlockSpec((tm, d), lambda i, c: (i, 0)),
            w.spec(wblock, windex),
            pl.BlockSpec((tm, SLAB), lambda i, c: (i, c)),
            mod.spec(2, tm, width=SLAB, col=1),
        ],
        out_specs=out_specs,
        out_shape=out_shape,
        compiler_params=_params("arbitrary", "arbitrary"),
        name="outproj",
    )(merged, w.arr, x, mod.arr)


def _mlp_kernel(x_ref, sc_ref, sh_ref, gt_ref, g_ref, w1_ref, w2_ref, o_ref, *rest, rc, nf):
    (w1b_ref, w2b_ref, h_ref) = rest if len(rest) == 3 else (None, None, rest[0])
    f = pl.program_id(1)

    @pl.when(f == 0)
    def _():
        _modnorm_to(h_ref, x_ref, sc_ref, sh_ref, g_ref, rc)
        o_ref[...] = jnp.zeros(o_ref.shape, F32)

    a = _dot(h_ref[...], _load_weight(w1_ref, w1b_ref))
    a = jnp.square(jnp.maximum(a, 0.0)).astype(BF16)
    if w2b_ref is not None:
        w2b_ref[...] = w2_ref[...].astype(BF16)
        w2_ref = w2b_ref
    for c0 in range(0, o_ref.shape[1], SLAB):
        o_ref[:, c0:c0 + SLAB] += _dot(a, w2_ref[:, c0:c0 + SLAB])

    @pl.when(f == nf - 1)
    def _():
        def body(cidx, carry):
            r = pl.multiple_of(cidx * rc, rc)
            rows = pl.ds(r, rc)
            o_ref[rows, :] = x_ref[rows, :] + _mod_rows(gt_ref, r, rc) * o_ref[rows, :]
            return carry

        lax.fori_loop(0, x_ref.shape[0] // rc, body, 0)


def _mlp_call(x, mod, g_norm, w1, w2, layer, tm, tf):
    m, d = x.shape
    _single_tile(m, tm, w1)
    assert w1.master == w2.master
    depth = g_norm.shape[0]
    nf = w1.shape[1] // tf
    w1block, w1index = (d, tf), lambda i, f: (0, f)
    w2block, w2index = (tf, d), lambda i, f: (f, 0)
    out_specs = [pl.BlockSpec((tm, d), lambda i, f: (i, 0))]
    out_shape = [jax.ShapeDtypeStruct((m, d), F32)]
    if w1.master:
        for spec, shape in (w1.emit(w1block, w1index), w2.emit(w2block, w2index)):
            out_specs.append(spec)
            out_shape.append(shape)
    return pl.pallas_call(
        functools.partial(_mlp_kernel, rc=_row_chunk(tm), nf=nf),
        grid=(m // tm, nf),
        in_specs=[
            pl.BlockSpec((tm, d), lambda i, f: (i, 0), pipeline_mode=pl.Buffered(1)),
            mod.spec(4, tm), mod.spec(3, tm), mod.spec(5, tm),
            pl.BlockSpec((None, 1, d), lambda i, f: (layer, 0, 0)),
            w1.spec(w1block, w1index),
            w2.spec(w2block, w2index),
        ],
        out_specs=out_specs,
        out_shape=out_shape,
        scratch_shapes=[pltpu.VMEM((tm, d), BF16)],
        compiler_params=_params("arbitrary", "arbitrary"),
        name="mlp",
    )(x, mod.arr, mod.arr, mod.arr, g_norm.reshape(depth, 1, d), w1.arr, w2.arr)


def _layer(x, mod, p, w, layer, nb_batch, t_len, tm_in, tm, tf, cache):
    copies = {}

    def split(outs, *names):
        outs = list(outs)
        for name in reversed(names):
            if w[name].master:
                copies[name] = outs.pop()
        return outs if len(outs) > 1 else outs[0]

    z3, h = split(_inproj_call(x, mod, p["g_norm1"], w["in_mix"], layer, tm_in), "in_mix")
    gates = split(_gate_call(h, w["in_gate"], tm_in), "in_gate")
    if cache is None:
        attn_cache = conv_init = pool_init = ret_init = None
        pos0 = 0
    else:
        attn_cache, conv_init, pool_init, ret_init = cache
        pos0 = PAST_LEN
    y_att, k_norm = _attn_call(z3, p["attn_sinks"], p["gq_t"], p["gk_t"], layer, nb_batch, t_len, attn_cache)
    y_conv, conv_tail = _conv_call(z3, p["w_dw"], p["b_dw"], p["g_conv_ln"], p["b_conv_ln"], layer, nb_batch, t_len,
                                   conv_init)
    y_pool = _pool_call(z3, p["w_pool"], p["s_pool"], layer, nb_batch, t_len, pos0, pool_init)
    y_ret, s_new = _ret_call(z3, p["g_ret_norm"], layer, nb_batch, t_len, ret_init)
    merged = split(_merge_call((y_att, y_conv, y_pool, y_ret), gates, w["br"], tm), "br")
    x = split(_outproj_call(merged, x, mod, w["out"], tm_in), "out")
    x = split(_mlp_call(x, mod, p["g_norm2"], w["mlp1"], w["mlp2"], layer, tm, tf), "mlp1", "mlp2")
    return x, (z3, k_norm, conv_tail, s_new), copies


def kernel(x_prompt, x_sample, c_prompt, c_sample, cache_attn_k, cache_attn_v, state_conv, state_pool, state_ret,
           w_ada, b_ada, g_norm1, g_norm2, w_in, g_qnorm, g_knorm, attn_sinks, w_dw, b_dw, g_conv_ln, b_conv_ln,
           w_pool, s_pool, g_ret_norm, w_br, w_out, w_mlp1, w_mlp2):
    nb, t_len, d = x_prompt.shape
    nsb, st_len, _ = x_sample.shape
    depth = w_ada.shape[0]
    mp, ms = nb * t_len, nsb * st_len

    n_c = nb + nsb
    r_pad = -(-n_c // 16) * 16
    c_all = jnp.concatenate([c_prompt, c_sample, jnp.zeros((r_pad - n_c, d), F32)], axis=0)
    ada = _ada_call(c_all, w_ada, b_ada)
    ada_p = ada[:, :nb].reshape(depth, nb, 1, 6 * d)
    ada_s = jnp.repeat(ada[:, nb:n_c], st_len, axis=1)

    p = dict(g_norm1=g_norm1, g_norm2=g_norm2, attn_sinks=attn_sinks, w_dw=w_dw, b_dw=b_dw,
             g_conv_ln=g_conv_ln, b_conv_ln=b_conv_ln, w_pool=w_pool, s_pool=s_pool, g_ret_norm=g_ret_norm,
             gq_t=jnp.tile(g_qnorm, (1, SLAB // HEAD_DIM)).reshape(depth, 1, SLAB),
             gk_t=jnp.tile(g_knorm, (1, N_KV)).reshape(depth, 1, KV_W))

    tm_p = 1024 if t_len % 1024 == 0 else t_len
    tm_p_in = 2048 if t_len % 2048 == 0 else tm_p
    tm_s = ms
    n_mix = SLAB_GATE * SLAB
    cache_k2 = cache_attn_k.reshape(depth, nsb, WINDOW, KV_W)
    cache_v2 = cache_attn_v.reshape(depth, nsb, WINDOW, KV_W)
    conv_init = jnp.pad(state_conv, ((0, 0), (0, 0), (CONV_HALO - (CONV_K - 1), 0), (0, 0)))
    pool_init = jnp.pad(state_pool, ((0, 0), (0, 0), (POOL_HALO - POOL_PAD, 0), (0, 0)))
    ret_init = state_ret.reshape(depth, nsb, RET_HEADS // 2, 2 * RET_DK, RET_DV)

    xp = x_prompt.reshape(mp, d)
    xs = x_sample.reshape(ms, d)
    st_p = [[] for _ in range(5)]
    st_s = [[] for _ in range(5)]
    for l in range(depth):
        masters = dict(in_mix=_Weight(w_in, l, shape=(d, n_mix)),
                       in_gate=_Weight(w_in, l, shape=(d, N_IN - n_mix), col0=SLAB_GATE),
                       br=_Weight(w_br, l), out=_Weight(w_out, l), mlp1=_Weight(w_mlp1, l), mlp2=_Weight(w_mlp2, l))
        mod_s = _Mod(ada_s, l, True, st_len)
        cache = ((cache_k2, cache_v2), conv_init, pool_init, ret_init)
        xs, (z3, k_norm, conv_tail, s_new), copies = _layer(xs, mod_s, p, masters, l, nsb, st_len, tm_s, tm_s,
                                                            MLP_TF_MASTER, cache)
        k_new = k_norm.reshape(nsb, st_len, N_KV, HEAD_DIM)
        v_new = z3[SLAB_KV].reshape(nsb, st_len, SLAB)[:, :, KV_W:].reshape(nsb, st_len, N_KV, HEAD_DIM)
        st_s[0].append(jnp.concatenate([cache_attn_k[l], k_new], axis=1)[:, -WINDOW:])
        st_s[1].append(jnp.concatenate([cache_attn_v[l], v_new], axis=1)[:, -WINDOW:])
        st_s[2].append(conv_tail[:, CONV_HALO - (CONV_K - 1):])
        pool_u = jnp.moveaxis(z3[SLAB_POOL:SLAB_POOL + 2].reshape(2, nsb, st_len, SLAB), 0, 2).reshape(nsb, st_len, BR_W)
        st_s[3].append(jnp.concatenate([state_pool[l], pool_u], axis=1)[:, -POOL_PAD:])
        st_s[4].append(s_new.reshape(nsb, RET_HEADS, RET_DK, RET_DV))

        mod_p = _Mod(ada_p, l, False, t_len)
        wcopy = {name: _Weight(arr) for name, arr in copies.items()}
        xp, (z3, k_norm, conv_tail, s_new), _ = _layer(xp, mod_p, p, wcopy, l, nb, t_len, tm_p_in, tm_p, MLP_TF, None)
        z4 = z3.reshape(SLAB_GATE, nb, t_len, SLAB)
        st_p[0].append(k_norm.reshape(nb, t_len, KV_W)[:, -WINDOW:].reshape(nb, WINDOW, N_KV, HEAD_DIM))
        st_p[1].append(z4[SLAB_KV, :, -WINDOW:, KV_W:].reshape(nb, WINDOW, N_KV, HEAD_DIM))
        st_p[2].append(conv_tail[:, CONV_HALO - (CONV_K - 1):])
        pool_u = z4[SLAB_POOL:SLAB_POOL + 2, :, -POOL_PAD:]
        st_p[3].append(jnp.moveaxis(pool_u, 0, 2).reshape(nb, POOL_PAD, BR_W))
        st_p[4].append(s_new.reshape(nb, RET_HEADS, RET_DK, RET_DV))

    return (xp.reshape(nb, t_len, d), xs.reshape(nsb, st_len, d),
            *[jnp.stack(a) for a in st_p], *[jnp.stack(a) for a in st_s])
```

```python
import functools

import numpy as np
import jax
import jax.numpy as jnp
from jax import lax
from jax.experimental import pallas as pl
from jax.experimental.pallas import tpu as pltpu

F32 = jnp.float32
BF16 = jnp.bfloat16

D_MODEL = 2048
PAST_LEN = 16384
N_HEADS = 16
HEAD_DIM = 64
N_KV = 4
WINDOW = 128
BR_W = 1024
CONV_K = 31
POOL_WINDOWS = (2, 4, 8, 16)
POOL_G = 256
POOL_PAD = 15
RET_HEADS = 8
RET_DK = 64
RET_DV = 128
RET_CHUNK = 128
N_BR = 4
D_FF = 4 * D_MODEL
MLP_TF = 1024
MLP_TF_MASTER = 512
MERGE_TN = 1024
EPS = 1e-6
KV_W = N_KV * HEAD_DIM
N_IN = 15872

SLAB = 512
N_SLAB = N_IN // SLAB
SLAB_Q, SLAB_KV, SLAB_CONV, SLAB_POOL, SLAB_RQ, SLAB_RK, SLAB_RV, SLAB_RG, SLAB_GATE = 0, 2, 3, 7, 9, 10, 11, 13, 15

VMEM_LIMIT_BYTES = 56 * 1024 * 1024
LANES = 128
SUBLANES = 8
NEG_BIG = -1e30
CONV_HALO = 32
POOL_HALO = 16


def _params(*sem):
    return pltpu.CompilerParams(dimension_semantics=sem, vmem_limit_bytes=VMEM_LIMIT_BYTES)


def _nt_dot(a, b):
    return lax.dot_general(a, b, (((1,), (1,)), ((), ())), preferred_element_type=F32)


def _tn_dot(a, b):
    return lax.dot_general(a, b, (((0,), (0,)), ((), ())), preferred_element_type=F32)


def _dot(a, b):
    return jnp.dot(a, b, preferred_element_type=F32)


def _silu(x):
    return x * jax.nn.sigmoid(x)


def _ada_kernel(c_ref, w_ref, b_ref, o_ref):
    s = _silu(c_ref[...]).astype(BF16)
    o_ref[...] = _dot(s, w_ref[...].astype(BF16)) + b_ref[...]


def _ada_call(c_all, w_ada, b_ada):
    depth, d, n = w_ada.shape
    r = c_all.shape[0]
    tn = 1024
    return pl.pallas_call(
        _ada_kernel,
        grid=(depth, n // tn),
        in_specs=[
            pl.BlockSpec((r, d), lambda l, j: (0, 0)),
            pl.BlockSpec((None, d, tn), lambda l, j: (l, 0, j)),
            pl.BlockSpec((None, 1, tn), lambda l, j: (l, 0, j)),
        ],
        out_specs=pl.BlockSpec((None, r, tn), lambda l, j: (l, 0, j)),
        out_shape=jax.ShapeDtypeStruct((depth, r, n), F32),
        compiler_params=_params("arbitrary", "arbitrary"),
        name="ada",
    )(c_all, w_ada, b_ada.reshape(depth, 1, n))


class _Mod:
    def __init__(self, arr, layer, per_row, t_len):
        self.arr, self.l, self.per_row, self.t_len = arr, layer, per_row, t_len

    def spec(self, k, tm, width=D_MODEL, col=None):
        l, tpb, nb = self.l, max(self.t_len // tm, 1), D_MODEL // width

        def cidx(idx):
            return k * nb + (idx[col] if col is not None else 0)

        if self.per_row:
            return pl.BlockSpec((None, tm, width), lambda *idx: (l, idx[0], cidx(idx)))
        return pl.BlockSpec((None, None, 1, width), lambda *idx: (l, idx[0] // tpb, 0, cidx(idx)))


def _mod_rows(ref, r, rc):
    return ref[...] if ref.shape[0] == 1 else ref[pl.ds(r, rc), :]


def _modnorm_to(h_ref, x_ref, sc_ref, sh_ref, g_ref, rc):
    g = g_ref[...]

    def body(c, carry):
        r = pl.multiple_of(c * rc, rc)
        x = x_ref[pl.ds(r, rc), :]
        ms = jnp.mean(x * x, axis=-1, keepdims=True)
        y = x * lax.rsqrt(ms + EPS) * g
        h = y * (1.0 + _mod_rows(sc_ref, r, rc)) + _mod_rows(sh_ref, r, rc)
        h_ref[pl.ds(r, rc), :] = h.astype(h_ref.dtype)
        return carry

    lax.fori_loop(0, x_ref.shape[0] // rc, body, 0)


def _row_chunk(tm):
    return 128 if tm % 128 == 0 else tm


class _Weight:
    def __init__(self, arr, layer=None, shape=None, col0=0):
        self.arr, self.layer, self.col0 = arr, layer, col0
        self.shape = tuple(shape if shape is not None else (arr.shape[1:] if layer is not None else arr.shape))

    @property
    def master(self):
        return self.layer is not None

    def spec(self, block, index):
        if not self.master:
            return pl.BlockSpec(block, index)
        layer, col0 = self.layer, self.col0

        def master_index(*g):
            idx = tuple(index(*g))
            return (layer,) + idx[:-1] + (idx[-1] + col0,)

        return pl.BlockSpec((None,) + tuple(block), master_index)

    def emit(self, block, index):
        return pl.BlockSpec(block, index), jax.ShapeDtypeStruct(self.shape, BF16)


def _load_weight(w_ref, copy_ref):
    if copy_ref is None:
        return w_ref[...]
    w = w_ref[...].astype(BF16)
    copy_ref[...] = w
    return w


def _single_tile(m, tm, w):
    assert not w.master or m == tm, "a master weight must be streamed by a single row tile"


def _inproj_kernel(x_ref, sc_ref, sh_ref, g_ref, w_ref, o_ref, h_ref, wb_ref=None, *, rc):
    @pl.when(pl.program_id(1) == 0)
    def _():
        _modnorm_to(h_ref, x_ref, sc_ref, sh_ref, g_ref, rc)

    o_ref[...] = _dot(h_ref[...], _load_weight(w_ref, wb_ref))


def _inproj_call(x, mod, g_norm, w, layer, tm):
    m, d = x.shape
    _single_tile(m, tm, w)
    depth = g_norm.shape[0]
    wblock, windex = (d, SLAB), lambda i, j: (0, j)
    out_specs = [pl.BlockSpec((None, tm, SLAB), lambda i, j: (j, i, 0)), pl.BlockSpec((tm, d), lambda i, j: (i, 0))]
    out_shape = [jax.ShapeDtypeStruct((SLAB_GATE, m, SLAB), F32), jax.ShapeDtypeStruct((m, d), BF16)]
    if w.master:
        spec, shape = w.emit(wblock, windex)
        out_specs.append(spec)
        out_shape.append(shape)
    return pl.pallas_call(
        functools.partial(_inproj_kernel, rc=_row_chunk(tm)),
        grid=(m // tm, SLAB_GATE),
        in_specs=[
            pl.BlockSpec((tm, d), lambda i, j: (i, 0), pipeline_mode=pl.Buffered(1)),
            mod.spec(1, tm), mod.spec(0, tm),
            pl.BlockSpec((None, 1, d), lambda i, j: (layer, 0, 0)),
            w.spec(wblock, windex),
        ],
        out_specs=out_specs,
        out_shape=out_shape,
        compiler_params=_params("arbitrary", "arbitrary"),
        name="inproj",
    )(x, mod.arr, mod.arr, g_norm.reshape(depth, 1, d), w.arr)


def _gate_kernel(h_ref, w_ref, o_ref, wb_ref=None):
    z = _dot(h_ref[...], _load_weight(w_ref, wb_ref))
    o_ref[...] = (0.5 * jnp.tanh(0.5 * z) + 0.5).astype(o_ref.dtype)


def _gate_call(h, w, tm):
    m, d = h.shape
    _single_tile(m, tm, w)
    per = MERGE_TN // SLAB
    wblock, windex = (d, SLAB), lambda i, j: (0, j)
    out_specs = [pl.BlockSpec((None, tm, SLAB), lambda i, j: (j // per, i, j % per))]
    out_shape = [jax.ShapeDtypeStruct(((N_SLAB - SLAB_GATE) // per, m, MERGE_TN), BF16)]
    if w.master:
        spec, shape = w.emit(wblock, windex)
        out_specs.append(spec)
        out_shape.append(shape)
    return pl.pallas_call(
        _gate_kernel,
        grid=(m // tm, N_SLAB - SLAB_GATE),
        in_specs=[pl.BlockSpec((tm, d), lambda i, j: (i, 0)), w.spec(wblock, windex)],
        out_specs=out_specs,
        out_shape=out_shape,
        compiler_params=_params("arbitrary", "arbitrary"),
        name="ingate",
    )(h, w.arr)


def _group_sum_matrix(width, group):
    idx = np.arange(width) // group
    return jnp.asarray((idx[:, None] == idx[None, :]).astype(np.float32), dtype=BF16)


SHORT_GROUP = 4
RET_CHUNKS_PER_STEP = 4


def _group_size(nb_batch, nblocks, has_init):
    return SHORT_GROUP if has_init and nblocks == 1 and nb_batch % SHORT_GROUP == 0 else 1


def _attn_kernel(*refs, layer, tq, has_init, group):
    if group == 1:
        blocks = [_attn_block(*refs, layer=layer, tq=tq, has_init=has_init)]
    else:
        sinks_ref, q_ref, kvc_ref, kinit_ref, vinit_ref, gq_ref, gk_ref, gm512_ref, gm256_ref, sd_ref, o_ref, kn_ref = refs
        blocks = []
        for bi in range(group):
            rows = pl.ds(bi * tq, tq)
            blocks.append(_attn_block(
                sinks_ref, q_ref.at[:, rows, :], kvc_ref.at[rows, :], kinit_ref.at[bi], vinit_ref.at[bi],
                gq_ref, gk_ref, gm512_ref, gm256_ref, sd_ref, o_ref.at[rows, :], kn_ref.at[rows, :],
                layer=layer, tq=tq, has_init=True))
    while blocks:
        blocks = [b for b in blocks if next(b, _DONE) is not _DONE]


_DONE = object()


def _attn_block(*refs, layer, tq, has_init):
    if has_init:
        sinks_ref, q_ref, kvc_ref, kinit_ref, vinit_ref, gq_ref, gk_ref, gm512_ref, gm256_ref, sd_ref, o_ref, kn_ref = refs
    else:
        sinks_ref, q_ref, kvc_ref, kvp_ref, gq_ref, gk_ref, gm512_ref, gm256_ref, sd_ref, o_ref, kn_ref = refs
    n = pl.program_id(1)

    def qk_norm(x, gmat, g):
        x2 = x * x
        hi = x2.astype(BF16)
        lo = (x2 - hi.astype(F32)).astype(BF16)
        ss = _dot(hi, gmat) + _dot(lo, gmat)
        return x * lax.rsqrt(ss * (1.0 / HEAD_DIM) + EPS) * g

    gm256, gk = gm256_ref[...], gk_ref[...]
    kvc = kvc_ref[...]
    kc = qk_norm(kvc[:, :KV_W], gm256, gk)
    kn_ref[...] = kc
    vc = kvc[:, KV_W:]
    if has_init:
        kp, vp = kinit_ref[...], vinit_ref[...]
    else:
        kvp = kvp_ref[...]
        kp, vp = qk_norm(kvp[:, :KV_W], gm256, gk), kvp[:, KV_W:]
    if tq < WINDOW:
        pad = jnp.zeros((WINDOW - tq, KV_W), F32)
        kc = jnp.concatenate([kc, pad], axis=0)
        vc = jnp.concatenate([vc, pad], axis=0)
    kall = jnp.concatenate([kp, kc], axis=0)
    vall = jnp.concatenate([vp, vc], axis=0)

    nk = 2 * WINDOW
    ii = lax.broadcasted_iota(jnp.int32, (tq, nk), 0)
    jj = lax.broadcasted_iota(jnp.int32, (tq, nk), 1)
    dist = WINDOW + ii - jj
    valid = (dist >= 0) & (dist <= WINDOW)
    if not has_init:
        valid = valid & (jj >= jnp.where(n > 0, 0, WINDOW))
    distm = jnp.where(valid, dist.astype(F32), -NEG_BIG)

    gm512, gq = gm512_ref[...], gq_ref[...] * (HEAD_DIM ** -0.5)
    qn = [qk_norm(q_ref[s], gm512, gq) for s in range(2)]
    lane_k = lax.broadcasted_iota(jnp.int32, (nk, LANES), 1) < HEAD_DIM
    lane_q = lax.broadcasted_iota(jnp.int32, (tq, LANES), 1) < HEAD_DIM
    sd = sd_ref[...]

    def two_copies(a, upper):
        if upper:
            bot = jnp.where(lane_k, 0.0, a)
            top = pltpu.roll(bot, HEAD_DIM, 1)
        else:
            top = jnp.where(lane_k, a, 0.0)
            bot = pltpu.roll(top, HEAD_DIM, 1)
        return jnp.concatenate([top, bot], axis=0).astype(BF16)

    kds, vds = [], []
    for kv in range(N_KV):
        cs = slice(LANES * (kv // 2), LANES * (kv // 2) + LANES)
        kds.append(two_copies(kall[:, cs], bool(kv % 2)))
        vds.append(two_copies(vall[:, cs], bool(kv % 2)))
    npair = N_HEADS // 2
    yield
    scores = []
    for p in range(npair):
        off = LANES * (p % 4)
        qp = qn[p // 4][:, off:off + LANES].astype(BF16)
        scores.append(_nt_dot(qp, kds[p // 2]))
    yield
    probs, sinkw = [], []
    for p in range(npair):
        es, sk = [], []
        for hh in range(2):
            h = 2 * p + hh
            slope = 2.0 ** (-8.0 * (h + 1) / N_HEADS)
            sink = sinks_ref[layer, h]
            sh = scores[p][:, nk * hh:nk * hh + nk] - slope * distm
            mx = jnp.maximum(jnp.max(sh, axis=-1, keepdims=True), sink)
            es.append(jnp.exp(sh - mx))
            sk.append(jnp.exp(sink - mx))
        probs.append(jnp.concatenate(es, axis=1).astype(BF16))
        sinkw.append(jnp.where(lane_q, sk[0], sk[1]))
    yield
    for p in range(npair):
        num = _dot(probs[p], vds[p // 2])
        den = _dot(probs[p], sd) + sinkw[p]
        o_ref[:, LANES * p:LANES * p + LANES] = (num / den).astype(o_ref.dtype)


def _attn_call(z3, sinks, gq_t, gk_t, layer, nb_batch, t_len, cache=None):
    m = z3.shape[1]
    has_init = cache is not None
    tq = WINDOW if t_len % WINDOW == 0 else t_len
    nb = t_len // tq
    assert nb == 1 or not has_init
    depth = gq_t.shape[0]
    gm512 = _group_sum_matrix(SLAB, HEAD_DIM)
    gm256 = _group_sum_matrix(KV_W, HEAD_DIM)
    sd_np = np.zeros((4 * WINDOW, LANES), np.float32)
    sd_np[:2 * WINDOW, :HEAD_DIM] = 1.0
    sd_np[2 * WINDOW:, HEAD_DIM:] = 1.0
    sd = jnp.asarray(sd_np, dtype=BF16)

    group = _group_size(nb_batch, nb, has_init)
    rows = group * tq
    in_specs = [
        pl.BlockSpec(memory_space=pltpu.SMEM),
        pl.BlockSpec((2, rows, SLAB), lambda b, n: (0, b * nb + n, 0)),
        pl.BlockSpec((None, rows, SLAB), lambda b, n: (SLAB_KV, b * nb + n, 0)),
    ]
    args = [sinks, z3, z3]
    if has_init:
        cache_spec = pl.BlockSpec((None, group, WINDOW, KV_W), lambda b, n: (layer, b, 0, 0))
        if group == 1:
            cache_spec = pl.BlockSpec((None, None, WINDOW, KV_W), lambda b, n: (layer, b, 0, 0))
        in_specs += [cache_spec] * 2
        args += [cache[0], cache[1]]
    else:
        in_specs += [pl.BlockSpec((None, WINDOW, SLAB), lambda b, n: (SLAB_KV, jnp.maximum(b * nb + n - 1, 0), 0))]
        args += [z3]
    in_specs += [
        pl.BlockSpec((None, 1, SLAB), lambda b, n: (layer, 0, 0)),
        pl.BlockSpec((None, 1, KV_W), lambda b, n: (layer, 0, 0)),
        pl.BlockSpec((SLAB, SLAB), lambda b, n: (0, 0)),
        pl.BlockSpec((KV_W, KV_W), lambda b, n: (0, 0)),
        pl.BlockSpec((4 * WINDOW, LANES), lambda b, n: (0, 0)),
    ]
    args += [gq_t, gk_t, gm512, gm256, sd]
    out_dtype = F32 if has_init else BF16
    return pl.pallas_call(
        functools.partial(_attn_kernel, layer=layer, tq=tq, has_init=has_init, group=group),
        grid=(nb_batch // group, nb),
        in_specs=in_specs,
        out_specs=[
            pl.BlockSpec((rows, BR_W), lambda b, n: (b * nb + n, 0)),
            pl.BlockSpec((rows, KV_W), lambda b, n: (b * nb + n, 0)),
        ],
        out_shape=[jax.ShapeDtypeStruct((m, BR_W), out_dtype), jax.ShapeDtypeStruct((m, KV_W), F32)],
        compiler_params=_params("arbitrary", "arbitrary"),
        name="attn",
    )(*args)


def _conv_kernel(*refs, tt, nt, has_init):
    if has_init:
        l0, l1, g0, g1, init_ref, w_ref, b_ref, gl_ref, bl_ref, o_ref, new_ref, ext, ybuf, shifted = refs
    else:
        l0, l1, g0, g1, w_ref, b_ref, gl_ref, bl_ref, o_ref, new_ref, ext, ybuf, shifted = refs
    t = pl.program_id(1)

    @pl.when(t == 0)
    def _():
        ext[0:CONV_HALO, :] = init_ref[...] if has_init else jnp.zeros((CONV_HALO, BR_W), F32)

    if nt > 1:
        @pl.when(t > 0)
        def _():
            ext[0:CONV_HALO, :] = ext[tt:tt + CONV_HALO, :]

    for cb, (lr, gr) in enumerate(((l0, g0), (l1, g1))):
        ext[CONV_HALO:CONV_HALO + tt, SLAB * cb:SLAB * cb + SLAB] = lr[...] * jax.nn.sigmoid(gr[...])
    new_ref[...] = ext[tt:tt + CONV_HALO, :]

    rs = min(tt, 32)
    base = CONV_HALO - (CONV_K - 1)
    nsh = shifted.shape[1]
    for c in range(BR_W // LANES):
        cs = slice(LANES * c, LANES * c + LANES)
        for r in range(1, SUBLANES):
            shifted[r - 1] = ext[r:r + nsh, cs]

        def taps(i, carry, cs=cs):
            r0 = pl.multiple_of(i * rs, rs)
            acc = jnp.broadcast_to(b_ref[:, cs], (rs, LANES))
            for r in range(SUBLANES):
                offs = [(base + k) // SUBLANES for k in range(CONV_K) if (base + k) % SUBLANES == r]
                rows = pl.ds(r0 + SUBLANES * offs[0], rs + SUBLANES * (offs[-1] - offs[0]))
                win = ext[rows, cs] if r == 0 else shifted[r - 1, rows, :]
                for a in offs:
                    k = SUBLANES * a + r - base
                    d = SUBLANES * (a - offs[0])
                    acc = acc + w_ref[k:k + 1, cs] * win[d:d + rs]
            ybuf[pl.ds(r0, rs), cs] = acc
            return carry

        lax.fori_loop(0, tt // rs, taps, 0)

    def norm_act(i, carry):
        rows = pl.ds(pl.multiple_of(i * rs, rs), rs)
        y = ybuf[rows, :]
        yc = y - jnp.mean(y, axis=-1, keepdims=True)
        var = jnp.mean(yc * yc, axis=-1, keepdims=True)
        yn = yc * lax.rsqrt(var + EPS) * gl_ref[...] + bl_ref[...]
        o_ref[rows, :] = _silu(yn).astype(o_ref.dtype)
        return carry

    lax.fori_loop(0, tt // rs, norm_act, 0, unroll=min(4, tt // rs))


def _conv_short_kernel(l0, l1, g0, g1, init_ref, w_ref, b_ref, gl_ref, bl_ref, o_ref, new_ref, ext, ybuf, *,
                       nbat, t_len):
    ext[:, 0:SUBLANES, :] = jnp.zeros((nbat, SUBLANES, BR_W), F32)
    ext[:, CONV_HALO - (CONV_K - 1):CONV_HALO, :] = init_ref[...]
    for cb, (lr, gr) in enumerate(((l0, g0), (l1, g1))):
        u = lr[...] * jax.nn.sigmoid(gr[...])
        ext[:, CONV_HALO:CONV_HALO + t_len, SLAB * cb:SLAB * cb + SLAB] = u.reshape(nbat, t_len, SLAB)
    new_ref[...] = ext[:, t_len:t_len + CONV_HALO, :]
    base = CONV_HALO - (CONV_K - 1)
    for c in range(BR_W // LANES):
        cs = slice(LANES * c, LANES * c + LANES)
        acc = jnp.broadcast_to(b_ref[:, cs], (nbat, t_len, LANES))
        for k in range(CONV_K):
            acc = acc + w_ref[k:k + 1, cs] * ext[:, base + k:base + k + t_len, cs]
        ybuf[:, cs] = acc.reshape(nbat * t_len, LANES)
    rs = 32

    def norm_act(i, carry):
        rows = pl.ds(pl.multiple_of(i * rs, rs), rs)
        y = ybuf[rows, :]
        yc = y - jnp.mean(y, axis=-1, keepdims=True)
        var = jnp.mean(yc * yc, axis=-1, keepdims=True)
        yn = yc * lax.rsqrt(var + EPS) * gl_ref[...] + bl_ref[...]
        o_ref[rows, :] = _silu(yn).astype(o_ref.dtype)
        return carry

    lax.fori_loop(0, nbat * t_len // rs, norm_act, 0, unroll=min(4, nbat * t_len // rs))


def _short_batched(nb_batch, t_len, has_init):
    return has_init and t_len == SUBLANES and (nb_batch * t_len) % 32 == 0


def _conv_short_call(z3, init, w_dw, b_dw, g_ln, b_ln, layer, nb_batch, t_len):
    m = z3.shape[1]
    depth = w_dw.shape[0]
    slab = lambda j: pl.BlockSpec((None, m, SLAB), lambda i: (j, 0, 0))
    vec = lambda: pl.BlockSpec((None, 1, BR_W), lambda i: (layer, 0, 0))
    return pl.pallas_call(
        functools.partial(_conv_short_kernel, nbat=nb_batch, t_len=t_len),
        grid=(1,),
        in_specs=[slab(SLAB_CONV), slab(SLAB_CONV + 1), slab(SLAB_CONV + 2), slab(SLAB_CONV + 3),
                  pl.BlockSpec((None, nb_batch, CONV_K - 1, BR_W), lambda i: (layer, 0, 0, 0)),
                  pl.BlockSpec((None, CONV_K, BR_W), lambda i: (layer, 0, 0)), vec(), vec(), vec()],
        out_specs=[pl.BlockSpec((m, BR_W), lambda i: (0, 0)),
                   pl.BlockSpec((nb_batch, CONV_HALO, BR_W), lambda i: (0, 0, 0))],
        out_shape=[jax.ShapeDtypeStruct((m, BR_W), F32), jax.ShapeDtypeStruct((nb_batch, CONV_HALO, BR_W), F32)],
        scratch_shapes=[pltpu.VMEM((nb_batch, CONV_HALO + t_len, BR_W), F32), pltpu.VMEM((m, BR_W), F32)],
        compiler_params=_params("arbitrary"),
        name="conv_short",
    )(z3, z3, z3, z3, init, w_dw, b_dw.reshape(depth, 1, BR_W), g_ln.reshape(depth, 1, BR_W),
      b_ln.reshape(depth, 1, BR_W))


def _conv_call(z3, w_dw, b_dw, g_ln, b_ln, layer, nb_batch, t_len, init=None):
    m = z3.shape[1]
    has_init = init is not None
    if _short_batched(nb_batch, t_len, has_init):
        return _conv_short_call(z3, init, w_dw, b_dw, g_ln, b_ln, layer, nb_batch, t_len)
    if has_init:
        init = jnp.pad(init, ((0, 0), (0, 0), (CONV_HALO - (CONV_K - 1), 0), (0, 0)))
    tt = 512 if t_len % 512 == 0 else (256 if t_len % 256 == 0 else t_len)
    nt = t_len // tt
    assert nt == 1 or tt >= CONV_HALO
    depth = w_dw.shape[0]

    def slab(j):
        return pl.BlockSpec((None, tt, SLAB), lambda b, t: (j, b * nt + t, 0))

    def vec():
        return pl.BlockSpec((None, 1, BR_W), lambda b, t: (layer, 0, 0))

    in_specs = [slab(SLAB_CONV), slab(SLAB_CONV + 1), slab(SLAB_CONV + 2), slab(SLAB_CONV + 3)]
    args = [z3, z3, z3, z3]
    if has_init:
        in_specs.append(pl.BlockSpec((None, None, CONV_HALO, BR_W), lambda b, t: (layer, b, 0, 0)))
        args.append(init)
    in_specs += [pl.BlockSpec((None, CONV_K, BR_W), lambda b, t: (layer, 0, 0)), vec(), vec(), vec()]
    args += [w_dw, b_dw.reshape(depth, 1, BR_W), g_ln.reshape(depth, 1, BR_W), b_ln.reshape(depth, 1, BR_W)]
    return pl.pallas_call(
        functools.partial(_conv_kernel, tt=tt, nt=nt, has_init=has_init),
        grid=(nb_batch, nt),
        in_specs=in_specs,
        out_specs=[
            pl.BlockSpec((tt, BR_W), lambda b, t: (b * nt + t, 0)),
            pl.BlockSpec((None, CONV_HALO, BR_W), lambda b, t: (b, 0, 0)),
        ],
        out_shape=[
            jax.ShapeDtypeStruct((m, BR_W), F32 if has_init else BF16),
            jax.ShapeDtypeStruct((nb_batch, CONV_HALO, BR_W), F32),
        ],
        scratch_shapes=[pltpu.VMEM((CONV_HALO + tt, BR_W), F32), pltpu.VMEM((tt, BR_W), F32),
                        pltpu.VMEM((SUBLANES - 1, CONV_HALO + tt - SUBLANES, LANES), F32)],
        compiler_params=_params("arbitrary", "arbitrary"),
        name="conv",
    )(*args)


def _pool_kernel(*refs, tt, nt, has_init, pos0):
    if has_init:
        u0, u1, init_ref, w_ref, s_ref, o_ref, ext = refs
    else:
        u0, u1, w_ref, s_ref, o_ref, ext = refs
    t = pl.program_id(1)

    @pl.when(t == 0)
    def _():
        ext[0:POOL_HALO, :] = init_ref[...] if has_init else jnp.zeros((POOL_HALO, BR_W), F32)

    if nt > 1:
        @pl.when(t > 0)
        def _():
            ext[0:POOL_HALO, :] = ext[tt:tt + POOL_HALO, :]

    ext[POOL_HALO:POOL_HALO + tt, 0:SLAB] = u0[...]
    ext[POOL_HALO:POOL_HALO + tt, SLAB:2 * SLAB] = u1[...]
    pos = pos0 + t * tt + lax.broadcasted_iota(jnp.int32, (tt, 1), 0)
    for g, w in enumerate(POOL_WINDOWS):
        cs = slice(POOL_G * g, POOL_G * g + POOL_G)
        cur = ext[POOL_HALO:POOL_HALO + tt, cs]
        wsum = cur
        for s in range(1, w):
            wsum = wsum + ext[POOL_HALO - s:POOL_HALO - s + tt, cs]
        cnt = jnp.minimum(pos + 1, w).astype(F32)
        zg = wsum / cnt - cur
        y = _dot(zg.astype(BF16), w_ref[g].astype(BF16)) * s_ref[:, cs]
        o_ref[:, cs] = y.astype(o_ref.dtype)


def _pool_short_kernel(u0, u1, init_ref, w_ref, s_ref, o_ref, ext, *, nbat, t_len, pos0):
    ext[:, 0:SUBLANES, :] = jnp.zeros((nbat, SUBLANES, BR_W), F32)
    ext[:, POOL_HALO - POOL_PAD:POOL_HALO, :] = init_ref[...]
    ext[:, POOL_HALO:POOL_HALO + t_len, 0:SLAB] = u0[...].reshape(nbat, t_len, SLAB)
    ext[:, POOL_HALO:POOL_HALO + t_len, SLAB:2 * SLAB] = u1[...].reshape(nbat, t_len, SLAB)
    pos = pos0 + lax.broadcasted_iota(jnp.int32, (nbat, t_len, POOL_G), 1)
    for g, w in enumerate(POOL_WINDOWS):
        cs = slice(POOL_G * g, POOL_G * g + POOL_G)
        cur = ext[:, POOL_HALO:POOL_HALO + t_len, cs]
        wsum = cur
        for s in range(1, w):
            wsum = wsum + ext[:, POOL_HALO - s:POOL_HALO - s + t_len, cs]
        cnt = jnp.minimum(pos + 1, w).astype(F32)
        zg = (wsum / cnt - cur).reshape(nbat * t_len, POOL_G)
        y = _dot(zg.astype(BF16), w_ref[g].astype(BF16)) * s_ref[:, cs]
        o_ref[:, cs] = y.astype(o_ref.dtype)


def _pool_short_call(z3, init, w_pool, s_pool, layer, nb_batch, t_len, pos0):
    m = z3.shape[1]
    depth = w_pool.shape[0]
    slab = lambda j: pl.BlockSpec((None, m, SLAB), lambda i: (j, 0, 0))
    return pl.pallas_call(
        functools.partial(_pool_short_kernel, nbat=nb_batch, t_len=t_len, pos0=pos0),
        grid=(1,),
        in_specs=[slab(SLAB_POOL), slab(SLAB_POOL + 1),
                  pl.BlockSpec((None, nb_batch, POOL_PAD, BR_W), lambda i: (layer, 0, 0, 0)),
                  pl.BlockSpec((None, len(POOL_WINDOWS), POOL_G, POOL_G), lambda i: (layer, 0, 0, 0)),
                  pl.BlockSpec((None, 1, BR_W), lambda i: (layer, 0, 0))],
        out_specs=pl.BlockSpec((m, BR_W), lambda i: (0, 0)),
        out_shape=jax.ShapeDtypeStruct((m, BR_W), F32),
        scratch_shapes=[pltpu.VMEM((nb_batch, POOL_HALO + t_len, BR_W), F32)],
        compiler_params=_params("arbitrary"),
        name="pool_short",
    )(z3, z3, init, w_pool, s_pool.reshape(depth, 1, BR_W))


def _pool_call(z3, w_pool, s_pool, layer, nb_batch, t_len, pos0, init=None):
    m = z3.shape[1]
    has_init = init is not None
    if _short_batched(nb_batch, t_len, has_init):
        return _pool_short_call(z3, init, w_pool, s_pool, layer, nb_batch, t_len, pos0)
    if has_init:
        init = jnp.pad(init, ((0, 0), (0, 0), (POOL_HALO - POOL_PAD, 0), (0, 0)))
    tt = 256 if t_len % 256 == 0 else t_len
    nt = t_len // tt
    assert nt == 1 or tt >= POOL_HALO
    depth = w_pool.shape[0]

    def slab(j):
        return pl.BlockSpec((None, tt, SLAB), lambda b, t: (j, b * nt + t, 0))

    in_specs = [slab(SLAB_POOL), slab(SLAB_POOL + 1)]
    args = [z3, z3]
    if has_init:
        in_specs.append(pl.BlockSpec((None, None, POOL_HALO, BR_W), lambda b, t: (layer, b, 0, 0)))
        args.append(init)
    in_specs += [
        pl.BlockSpec((None, len(POOL_WINDOWS), POOL_G, POOL_G), lambda b, t: (layer, 0, 0, 0)),
        pl.BlockSpec((None, 1, BR_W), lambda b, t: (layer, 0, 0)),
    ]
    args += [w_pool, s_pool.reshape(depth, 1, BR_W)]
    return pl.pallas_call(
        functools.partial(_pool_kernel, tt=tt, nt=nt, has_init=has_init, pos0=pos0),
        grid=(nb_batch, nt),
        in_specs=in_specs,
        out_specs=pl.BlockSpec((tt, BR_W), lambda b, t: (b * nt + t, 0)),
        out_shape=jax.ShapeDtypeStruct((m, BR_W), F32 if has_init else BF16),
        scratch_shapes=[pltpu.VMEM((POOL_HALO + tt, BR_W), F32)],
        compiler_params=_params("arbitrary", "arbitrary"),
        name="pool",
    )(*args)


def _ret_constants(c, cp):
    lg = np.log1p(-np.exp2(-5.0 - np.arange(RET_HEADS, dtype=np.float64)))
    i = np.arange(c, dtype=np.float64)
    diff = i[:, None] - i[None, :]
    decay = np.where(diff >= 0, np.exp(lg[:, None, None] * np.maximum(diff, 0.0)), 0.0)
    dec = np.zeros((RET_HEADS // 2, c, 2 * cp))
    for h in range(RET_HEADS):
        dec[h // 2, :, (h % 2) * cp:(h % 2) * cp + c] = decay[h]
    kfac = np.repeat(np.exp(lg[None, :] * (c - 1 - i)[:, None]), RET_DK, axis=1) * RET_DK ** -0.5
    cfac = np.repeat(np.exp(lg[None, :] * (i + 1)[:, None]), RET_DV, axis=1)
    gch = np.repeat(np.exp(lg * c), RET_DK).reshape(RET_HEADS // 2, 2 * RET_DK, 1)
    gch = np.broadcast_to(gch, (RET_HEADS // 2, 2 * RET_DK, RET_DV))
    f = lambda a: jnp.asarray(np.ascontiguousarray(a), dtype=F32)
    return f(dec), f(kfac), f(cfac), f(gch)


def _ret_kernel(*refs, c, cp, has_init, group, cps):
    if group == 1 and cps == 1:
        return _ret_chunk(*refs, c=c, cp=cp, has_init=has_init)
    if group == 1:
        ins, rest = refs[:6], refs[6:-2]
        o_ref, s_ref = refs[-2:]
        for ci in range(cps):
            rows = pl.ds(ci * c, c)
            _ret_chunk(*[r.at[rows, :] for r in ins], *rest, o_ref.at[rows, :], s_ref,
                       c=c, cp=cp, has_init=has_init, first=ci == 0)
        return
    rq, rk, rv0, rv1, rg0, rg1, s0_ref, gn_ref, dec_ref, kf_ref, cf_ref, gch_ref, o_ref, s_ref = refs
    for bi in range(group):
        rows = pl.ds(bi * c, c)
        _ret_chunk(rq.at[rows, :], rk.at[rows, :], rv0.at[rows, :], rv1.at[rows, :], rg0.at[rows, :], rg1.at[rows, :],
                   s0_ref.at[bi], gn_ref, dec_ref, kf_ref, cf_ref, gch_ref, o_ref.at[rows, :], s_ref.at[bi],
                   c=c, cp=cp, has_init=True)


def _ret_chunk(*refs, c, cp, has_init, first=True):
    if has_init:
        rq, rk, rv0, rv1, rg0, rg1, s0_ref, gn_ref, dec_ref, kf_ref, cf_ref, gch_ref, o_ref, s_ref = refs
    else:
        rq, rk, rv0, rv1, rg0, rg1, gn_ref, dec_ref, kf_ref, cf_ref, gch_ref, o_ref, s_ref = refs
    n = pl.program_id(1)

    if first:
        @pl.when(n == 0)
        def _():
            s_ref[...] = s0_ref[...] if has_init else jnp.zeros(s_ref.shape, F32)

    lo = lax.broadcasted_iota(jnp.int32, (c, LANES), 1) < RET_DK
    q = rq[...]
    kraw = rk[...]
    k = kraw * (RET_DK ** -0.5)
    kdec = kraw * kf_ref[...]
    rvs, rgs = (rv0, rv1), (rg0, rg1)

    def stack_heads(a):
        a0, a1 = jnp.where(lo, a, 0.0), jnp.where(lo, 0.0, a)
        if cp > c:
            z = jnp.zeros((cp - c, LANES), F32)
            return jnp.concatenate([a0, z, a1, z], axis=0)
        return jnp.concatenate([a0, a1], axis=0)

    npair = RET_HEADS // 2
    zc = jnp.zeros((c, LANES), F32)
    scores, vbds, vsts, inners, crosses = [], [], [], [], []
    for p in range(npair):
        cs = slice(LANES * p, LANES * p + LANES)
        scores.append(_nt_dot(q[:, cs].astype(BF16), stack_heads(k[:, cs]).astype(BF16)) * dec_ref[p])
        vs = [rvs[h // 4][:, LANES * (h % 4):LANES * (h % 4) + LANES] for h in (2 * p, 2 * p + 1)]
        rows0 = jnp.concatenate([vs[0], zc], axis=1)
        rows1 = jnp.concatenate([zc, vs[1]], axis=1)
        if cp > c:
            zp = jnp.zeros((cp - c, 2 * LANES), F32)
            zq = jnp.zeros((cp - c, LANES), F32)
            vbds.append(jnp.concatenate([rows0, zp, rows1, zp], axis=0).astype(BF16))
            vsts.append(jnp.concatenate([vs[0], zq, vs[1], zq], axis=0).astype(BF16))
        else:
            vbds.append(jnp.concatenate([rows0, rows1], axis=0).astype(BF16))
            vsts.append(jnp.concatenate([vs[0], vs[1]], axis=0).astype(BF16))
    for p in range(npair):
        cs = slice(LANES * p, LANES * p + LANES)
        qp = q[:, cs]
        inners.append(_dot(scores[p].astype(BF16), vbds[p]))
        sprev = s_ref[p]
        qst = jnp.concatenate([jnp.where(lo, qp, 0.0), jnp.where(lo, 0.0, qp)], axis=0)
        crosses.append(_dot(qst.astype(BF16), sprev.astype(BF16)))
        upd = _tn_dot(stack_heads(kdec[:, cs]).astype(BF16), vsts[p])
        s_ref[p] = gch_ref[p] * sprev + upd
    for p in range(npair):
        inner, cross = inners[p], crosses[p]
        for hh, h in enumerate((2 * p, 2 * p + 1)):
            hc = slice(LANES * h, LANES * h + LANES)
            o = inner[:, LANES * hh:LANES * hh + LANES] + cross[c * hh:c * hh + c, :] * cf_ref[:, hc]
            oc = o - jnp.mean(o, axis=-1, keepdims=True)
            var = jnp.mean(oc * oc, axis=-1, keepdims=True)
            gate = rgs[h // 4][:, LANES * (h % 4):LANES * (h % 4) + LANES]
            y = oc * lax.rsqrt(var + EPS) * gn_ref[:, hc] * _silu(gate)
            o_ref[:, hc] = y.astype(o_ref.dtype)


def _ret_call(z3, g_ret, layer, nb_batch, t_len, init=None):
    m = z3.shape[1]
    has_init = init is not None
    c = RET_CHUNK if t_len % RET_CHUNK == 0 else t_len
    cp = max(c, 64)
    nc = t_len // c
    depth = g_ret.shape[0]
    dec, kfac, cfac, gch = _ret_constants(c, cp)
    npair = RET_HEADS // 2

    group = _group_size(nb_batch, nc, has_init)
    cps = RET_CHUNKS_PER_STEP if group == 1 and nc % RET_CHUNKS_PER_STEP == 0 else 1
    ns = nc // cps
    rows = group * cps * c
    state_block = (npair, LANES, RET_DV) if group == 1 else (group, npair, LANES, RET_DV)
    lead = (None,) if group == 1 else ()

    def slab(j):
        return pl.BlockSpec((None, rows, SLAB), lambda b, n: (j, b * ns + n, 0))

    def const(shape):
        nd = len(shape)
        return pl.BlockSpec(shape, lambda b, n: (0,) * nd)

    in_specs = [slab(SLAB_RQ), slab(SLAB_RK), slab(SLAB_RV), slab(SLAB_RV + 1), slab(SLAB_RG), slab(SLAB_RG + 1)]
    args = [z3] * 6
    if has_init:
        in_specs.append(pl.BlockSpec((None,) + lead + state_block, lambda b, n: (layer, b, 0, 0, 0)))
        args.append(init)
    in_specs += [pl.BlockSpec((None, 1, BR_W), lambda b, n: (layer, 0, 0)),
                 const(dec.shape), const(kfac.shape), const(cfac.shape), const(gch.shape)]
    args += [g_ret.reshape(depth, 1, BR_W), dec, kfac, cfac, gch]
    return pl.pallas_call(
        functools.partial(_ret_kernel, c=c, cp=cp, has_init=has_init, group=group, cps=cps),
        grid=(nb_batch // group, ns),
        in_specs=in_specs,
        out_specs=[
            pl.BlockSpec((rows, BR_W), lambda b, n: (b * ns + n, 0)),
            pl.BlockSpec(lead + state_block, lambda b, n: (b, 0, 0, 0)),
        ],
        out_shape=[
            jax.ShapeDtypeStruct((m, BR_W), F32 if has_init else BF16),
            jax.ShapeDtypeStruct((nb_batch, npair, LANES, RET_DV), F32),
        ],
        compiler_params=_params("arbitrary", "arbitrary"),
        name="retention",
    )(*args)


def _merge_kernel(y0, y1, y2, y3, gate_ref, w_ref, o_ref, *rest):
    wb_ref, acc_ref = rest if len(rest) == 2 else (None, rest[0])
    r, c = pl.program_id(1), pl.program_id(2)
    if wb_ref is not None:
        wb_ref[...] = w_ref[...].astype(BF16)
        w_ref = wb_ref
    for k, y_ref in enumerate((y0, y1, y2, y3)):
        @pl.when(r == k)
        def _(k=k, y_ref=y_ref):
            val = gate_ref[...].astype(F32) * _dot(y_ref[...].astype(BF16), w_ref[...])
            if k == 0:
                acc_ref[c] = val
            elif k < N_BR - 1:
                acc_ref[c] += val
            else:
                o_ref[...] = (acc_ref[c] + val).astype(o_ref.dtype)


def _merge_call(ys, gates, w, tm):
    m = gates.shape[1]
    _single_tile(m, tm, w)
    tn = MERGE_TN
    nc = D_MODEL // tn
    y_spec = pl.BlockSpec((tm, BR_W), lambda i, r, c: (i, 0))
    wblock, windex = (None, BR_W, tn), lambda i, r, c: (r, 0, c)
    out_specs = [pl.BlockSpec((tm, tn), lambda i, r, c: (i, jnp.where(r == N_BR - 1, c, 0)))]
    out_shape = [jax.ShapeDtypeStruct((m, D_MODEL), BF16)]
    if w.master:
        spec, shape = w.emit(wblock, windex)
        out_specs.append(spec)
        out_shape.append(shape)
    return pl.pallas_call(
        _merge_kernel,
        grid=(m // tm, N_BR, nc),
        in_specs=[y_spec] * N_BR + [
            pl.BlockSpec((None, tm, tn), lambda i, r, c: (nc * r + c, i, 0)),
            w.spec(wblock, windex),
        ],
        out_specs=out_specs,
        out_shape=out_shape,
        scratch_shapes=[pltpu.VMEM((nc, tm, tn), F32)],
        compiler_params=_params("arbitrary", "arbitrary", "arbitrary"),
        name="merge",
    )(*ys, gates, w.arr)


def _outproj_kernel(m_ref, w_ref, x_ref, gt_ref, o_ref, wb_ref=None):
    o_ref[...] = x_ref[...] + gt_ref[...] * _dot(m_ref[...], _load_weight(w_ref, wb_ref))


def _outproj_call(merged, x, mod, w, tm):
    m, d = x.shape
    _single_tile(m, tm, w)
    wblock, windex = (d, SLAB), lambda i, c: (0, c)
    out_specs = [pl.BlockSpec((tm, SLAB), lambda i, c: (i, c))]
    out_shape = [jax.ShapeDtypeStruct((m, d), F32)]
    if w.master:
        spec, shape = w.emit(wblock, windex)
        out_specs.append(spec)
        out_shape.append(shape)
    return pl.pallas_call(
        _outproj_kernel,
        grid=(m // tm, d // SLAB),
        in_specs=[
            pl.BlockSpec((tm, d), lambda i, c: (i, 0)),
            w.spec(wblock, windex),
            pl.BlockSpec((tm, SLAB), lambda i, c: (i, c)),
            mod.spec(2, tm, width=SLAB, col=1),
        ],
        out_specs=out_specs,
        out_shape=out_shape,
        compiler_params=_params("arbitrary", "arbitrary"),
        name="outproj",
    )(merged, w.arr, x, mod.arr)


def _mlp_kernel(x_ref, sc_ref, sh_ref, gt_ref, g_ref, w1_ref, w2_ref, o_ref, *rest, rc, nf):
    (w1b_ref, w2b_ref, h_ref) = rest if len(rest) == 3 else (None, None, rest[0])
    f = pl.program_id(1)

    @pl.when(f == 0)
    def _():
        _modnorm_to(h_ref, x_ref, sc_ref, sh_ref, g_ref, rc)
        o_ref[...] = jnp.zeros(o_ref.shape, F32)

    a = _dot(h_ref[...], _load_weight(w1_ref, w1b_ref))
    a = jnp.square(jnp.maximum(a, 0.0)).astype(BF16)
    if w2b_ref is not None:
        w2b_ref[...] = w2_ref[...].astype(BF16)
        w2_ref = w2b_ref
    for c0 in range(0, o_ref.shape[1], SLAB):
        o_ref[:, c0:c0 + SLAB] += _dot(a, w2_ref[:, c0:c0 + SLAB])

    @pl.when(f == nf - 1)
    def _():
        def body(cidx, carry):
            r = pl.multiple_of(cidx * rc, rc)
            rows = pl.ds(r, rc)
            o_ref[rows, :] = x_ref[rows, :] + _mod_rows(gt_ref, r, rc) * o_ref[rows, :]
            return carry

        lax.fori_loop(0, x_ref.shape[0] // rc, body, 0)


def _mlp_call(x, mod, g_norm, w1, w2, layer, tm, tf):
    m, d = x.shape
    _single_tile(m, tm, w1)
    assert w1.master == w2.master
    depth = g_norm.shape[0]
    nf = w1.shape[1] // tf
    w1block, w1index = (d, tf), lambda i, f: (0, f)
    w2block, w2index = (tf, d), lambda i, f: (f, 0)
    out_specs = [pl.BlockSpec((tm, d), lambda i, f: (i, 0))]
    out_shape = [jax.ShapeDtypeStruct((m, d), F32)]
    if w1.master:
        for spec, shape in (w1.emit(w1block, w1index), w2.emit(w2block, w2index)):
            out_specs.append(spec)
            out_shape.append(shape)
    return pl.pallas_call(
        functools.partial(_mlp_kernel, rc=_row_chunk(tm), nf=nf),
        grid=(m // tm, nf),
        in_specs=[
            pl.BlockSpec((tm, d), lambda i, f: (i, 0), pipeline_mode=pl.Buffered(1)),
            mod.spec(4, tm), mod.spec(3, tm), mod.spec(5, tm),
            pl.BlockSpec((None, 1, d), lambda i, f: (layer, 0, 0)),
            w1.spec(w1block, w1index),
            w2.spec(w2block, w2index),
        ],
        out_specs=out_specs,
        out_shape=out_shape,
        scratch_shapes=[pltpu.VMEM((tm, d), BF16)],
        compiler_params=_params("arbitrary", "arbitrary"),
        name="mlp",
    )(x, mod.arr, mod.arr, mod.arr, g_norm.reshape(depth, 1, d), w1.arr, w2.arr)


def _layer(x, mod, p, w, layer, nb_batch, t_len, tm_in, tm, tf, cache):
    copies = {}

    def split(outs, *names):
        outs = list(outs)
        for name in reversed(names):
            if w[name].master:
                copies[name] = outs.pop()
        return outs if len(outs) > 1 else outs[0]

    z3, h = split(_inproj_call(x, mod, p["g_norm1"], w["in_mix"], layer, tm_in), "in_mix")
    gates = split(_gate_call(h, w["in_gate"], tm_in), "in_gate")
    if cache is None:
        attn_cache = conv_init = pool_init = ret_init = None
        pos0 = 0
    else:
        attn_cache, conv_init, pool_init, ret_init = cache
        pos0 = PAST_LEN
    y_att, k_norm = _attn_call(z3, p["attn_sinks"], p["gq_t"], p["gk_t"], layer, nb_batch, t_len, attn_cache)
    y_conv, conv_tail = _conv_call(z3, p["w_dw"], p["b_dw"], p["g_conv_ln"], p["b_conv_ln"], layer, nb_batch, t_len,
                                   conv_init)
    y_pool = _pool_call(z3, p["w_pool"], p["s_pool"], layer, nb_batch, t_len, pos0, pool_init)
    y_ret, s_new = _ret_call(z3, p["g_ret_norm"], layer, nb_batch, t_len, ret_init)
    merged = split(_merge_call((y_att, y_conv, y_pool, y_ret), gates, w["br"], tm), "br")
    x = split(_outproj_call(merged, x, mod, w["out"], tm_in), "out")
    x = split(_mlp_call(x, mod, p["g_norm2"], w["mlp1"], w["mlp2"], layer, tm, tf), "mlp1", "mlp2")
    return x, (z3, k_norm, conv_tail, s_new), copies


def kernel(x_prompt, x_sample, c_prompt, c_sample, cache_attn_k, cache_attn_v, state_conv, state_pool, state_ret,
           w_ada, b_ada, g_norm1, g_norm2, w_in, g_qnorm, g_knorm, attn_sinks, w_dw, b_dw, g_conv_ln, b_conv_ln,
           w_pool, s_pool, g_ret_norm, w_br, w_out, w_mlp1, w_mlp2):
    nb, t_len, d = x_prompt.shape
    nsb, st_len, _ = x_sample.shape
    depth = w_ada.shape[0]
    mp, ms = nb * t_len, nsb * st_len

    n_c = nb + nsb
    r_pad = -(-n_c // 16) * 16
    c_all = jnp.concatenate([c_prompt, c_sample, jnp.zeros((r_pad - n_c, d), F32)], axis=0)
    ada = _ada_call(c_all, w_ada, b_ada)
    ada_p = ada[:, :nb].reshape(depth, nb, 1, 6 * d)
    ada_s = jnp.repeat(ada[:, nb:n_c], st_len, axis=1)

    p = dict(g_norm1=g_norm1, g_norm2=g_norm2, attn_sinks=attn_sinks, w_dw=w_dw, b_dw=b_dw,
             g_conv_ln=g_conv_ln, b_conv_ln=b_conv_ln, w_pool=w_pool, s_pool=s_pool, g_ret_norm=g_ret_norm,
             gq_t=jnp.tile(g_qnorm, (1, SLAB // HEAD_DIM)).reshape(depth, 1, SLAB),
             gk_t=jnp.tile(g_knorm, (1, N_KV)).reshape(depth, 1, KV_W))

    tm_p = 1024 if t_len % 1024 == 0 else t_len
    tm_p_in = 2048 if t_len % 2048 == 0 else tm_p
    tm_s = ms
    n_mix = SLAB_GATE * SLAB
    cache_k2 = cache_attn_k.reshape(depth, nsb, WINDOW, KV_W)
    cache_v2 = cache_attn_v.reshape(depth, nsb, WINDOW, KV_W)
    ret_init = state_ret.reshape(depth, nsb, RET_HEADS // 2, 2 * RET_DK, RET_DV)

    xp = x_prompt.reshape(mp, d)
    xs = x_sample.reshape(ms, d)
    st_p = [[] for _ in range(5)]
    st_s = [[] for _ in range(5)]
    for l in range(depth):
        masters = dict(in_mix=_Weight(w_in, l, shape=(d, n_mix)),
                       in_gate=_Weight(w_in, l, shape=(d, N_IN - n_mix), col0=SLAB_GATE),
                       br=_Weight(w_br, l), out=_Weight(w_out, l), mlp1=_Weight(w_mlp1, l), mlp2=_Weight(w_mlp2, l))
        mod_s = _Mod(ada_s, l, True, st_len)
        cache = ((cache_k2, cache_v2), state_conv, state_pool, ret_init)
        xs, (z3, k_norm, conv_tail, s_new), copies = _layer(xs, mod_s, p, masters, l, nsb, st_len, tm_s, tm_s,
                                                            MLP_TF_MASTER, cache)
        st_s[0].append(k_norm.reshape(nsb, st_len, N_KV, HEAD_DIM))
        st_s[1].append(z3[SLAB_KV].reshape(nsb, st_len, SLAB)[:, :, KV_W:].reshape(nsb, st_len, N_KV, HEAD_DIM))
        st_s[2].append(conv_tail)
        pool_u = jnp.moveaxis(z3[SLAB_POOL:SLAB_POOL + 2].reshape(2, nsb, st_len, SLAB), 0, 2)
        st_s[3].append(pool_u.reshape(nsb, st_len, BR_W))
        st_s[4].append(s_new.reshape(nsb, RET_HEADS, RET_DK, RET_DV))

        mod_p = _Mod(ada_p, l, False, t_len)
        wcopy = {name: _Weight(arr) for name, arr in copies.items()}
        xp, (z3, k_norm, conv_tail, s_new), _ = _layer(xp, mod_p, p, wcopy, l, nb, t_len, tm_p_in, tm_p, MLP_TF, None)
        z4 = z3.reshape(SLAB_GATE, nb, t_len, SLAB)
        st_p[0].append(k_norm.reshape(nb, t_len, KV_W)[:, -WINDOW:].reshape(nb, WINDOW, N_KV, HEAD_DIM))
        st_p[1].append(z4[SLAB_KV, :, -WINDOW:, KV_W:].reshape(nb, WINDOW, N_KV, HEAD_DIM))
        st_p[2].append(conv_tail[:, CONV_HALO - (CONV_K - 1):])
        pool_u = z4[SLAB_POOL:SLAB_POOL + 2, :, -POOL_PAD:]
        st_p[3].append(jnp.moveaxis(pool_u, 0, 2).reshape(nb, POOL_PAD, BR_W))
        st_p[4].append(s_new.reshape(nb, RET_HEADS, RET_DK, RET_DV))

    def rolled(old, new_rows, keep):
        return jnp.concatenate([old, jnp.stack(new_rows)], axis=2)[:, :, -keep:]

    sample_states = (rolled(cache_attn_k, st_s[0], WINDOW), rolled(cache_attn_v, st_s[1], WINDOW),
                     jnp.stack(st_s[2])[:, :, CONV_HALO - (CONV_K - 1):], rolled(state_pool, st_s[3], POOL_PAD),
                     jnp.stack(st_s[4]))
    return (xp.reshape(nb, t_len, d), xs.reshape(nsb, st_len, d), *[jnp.stack(a) for a in st_p], *sample_states)
```

```python
import functools

import numpy as np
import jax
import jax.numpy as jnp
from jax import lax
from jax.experimental import pallas as pl
from jax.experimental.pallas import tpu as pltpu

F32 = jnp.float32
BF16 = jnp.bfloat16

D_MODEL = 2048
PAST_LEN = 16384
N_HEADS = 16
HEAD_DIM = 64
N_KV = 4
WINDOW = 128
BR_W = 1024
CONV_K = 31
POOL_WINDOWS = (2, 4, 8, 16)
POOL_G = 256
POOL_PAD = 15
RET_HEADS = 8
RET_DK = 64
RET_DV = 128
RET_CHUNK = 128
N_BR = 4
D_FF = 4 * D_MODEL
MLP_TF = 1024
MLP_TF_MASTER = 512
MERGE_TN = 1024
EPS = 1e-6
KV_W = N_KV * HEAD_DIM
N_IN = 15872

SLAB = 512
N_SLAB = N_IN // SLAB
SLAB_Q, SLAB_KV, SLAB_CONV, SLAB_POOL, SLAB_RQ, SLAB_RK, SLAB_RV, SLAB_RG, SLAB_GATE = 0, 2, 3, 7, 9, 10, 11, 13, 15

VMEM_LIMIT_BYTES = 60 * 1024 * 1024
LANES = 128
SUBLANES = 8
NEG_BIG = -1e30
CONV_HALO = 32
POOL_HALO = 16


def _params(*sem):
    return pltpu.CompilerParams(dimension_semantics=sem, vmem_limit_bytes=VMEM_LIMIT_BYTES)


def _nt_dot(a, b):
    return lax.dot_general(a, b, (((1,), (1,)), ((), ())), preferred_element_type=F32)


def _tn_dot(a, b):
    return lax.dot_general(a, b, (((0,), (0,)), ((), ())), preferred_element_type=F32)


def _dot(a, b):
    return jnp.dot(a, b, preferred_element_type=F32)


def _silu(x):
    return x * jax.nn.sigmoid(x)


def _ada_kernel(c_ref, w_ref, b_ref, o_ref):
    s = _silu(c_ref[...]).astype(BF16)
    o_ref[...] = _dot(s, w_ref[...].astype(BF16)) + b_ref[...]


def _ada_call(c_all, w_ada, b_ada):
    depth, d, n = w_ada.shape
    r = c_all.shape[0]
    tn = 1024
    return pl.pallas_call(
        _ada_kernel,
        grid=(depth, n // tn),
        in_specs=[
            pl.BlockSpec((r, d), lambda l, j: (0, 0)),
            pl.BlockSpec((None, d, tn), lambda l, j: (l, 0, j)),
            pl.BlockSpec((None, 1, tn), lambda l, j: (l, 0, j)),
        ],
        out_specs=pl.BlockSpec((None, r, tn), lambda l, j: (l, 0, j)),
        out_shape=jax.ShapeDtypeStruct((depth, r, n), F32),
        compiler_params=_params("arbitrary", "arbitrary"),
        name="ada",
    )(c_all, w_ada, b_ada.reshape(depth, 1, n))


class _Mod:
    def __init__(self, arr, layer, per_row, t_len):
        self.arr, self.l, self.per_row, self.t_len = arr, layer, per_row, t_len

    def spec(self, k, tm, width=D_MODEL, col=None):
        l, tpb, nb = self.l, max(self.t_len // tm, 1), D_MODEL // width

        def cidx(idx):
            return k * nb + (idx[col] if col is not None else 0)

        if self.per_row:
            return pl.BlockSpec((None, tm, width), lambda *idx: (l, idx[0], cidx(idx)))
        return pl.BlockSpec((None, None, 1, width), lambda *idx: (l, idx[0] // tpb, 0, cidx(idx)))


def _mod_rows(ref, r, rc):
    return ref[...] if ref.shape[0] == 1 else ref[pl.ds(r, rc), :]


def _modnorm_to(h_ref, x_ref, sc_ref, sh_ref, g_ref, rc):
    g = g_ref[...]

    def body(c, carry):
        r = pl.multiple_of(c * rc, rc)
        x = x_ref[pl.ds(r, rc), :]
        ms = jnp.mean(x * x, axis=-1, keepdims=True)
        y = x * lax.rsqrt(ms + EPS) * g
        h = y * (1.0 + _mod_rows(sc_ref, r, rc)) + _mod_rows(sh_ref, r, rc)
        h_ref[pl.ds(r, rc), :] = h.astype(h_ref.dtype)
        return carry

    lax.fori_loop(0, x_ref.shape[0] // rc, body, 0)


def _row_chunk(tm):
    return 128 if tm % 128 == 0 else tm


class _Weight:
    def __init__(self, arr, layer=None, shape=None, col0=0):
        self.arr, self.layer, self.col0 = arr, layer, col0
        self.shape = tuple(shape if shape is not None else (arr.shape[1:] if layer is not None else arr.shape))

    @property
    def master(self):
        return self.layer is not None

    def spec(self, block, index):
        if not self.master:
            return pl.BlockSpec(block, index)
        layer, col0 = self.layer, self.col0

        def master_index(*g):
            idx = tuple(index(*g))
            return (layer,) + idx[:-1] + (idx[-1] + col0,)

        return pl.BlockSpec((None,) + tuple(block), master_index)

    def emit(self, block, index):
        return pl.BlockSpec(block, index), jax.ShapeDtypeStruct(self.shape, BF16)


def _load_weight(w_ref, copy_ref):
    if copy_ref is None:
        return w_ref[...]
    w = w_ref[...].astype(BF16)
    copy_ref[...] = w
    return w


def _single_tile(m, tm, w):
    assert not w.master or m == tm, "a master weight must be streamed by a single row tile"


def _inproj_kernel(x_ref, sc_ref, sh_ref, g_ref, w_ref, o_ref, h_ref, wb_ref=None, *, rc):
    @pl.when(pl.program_id(1) == 0)
    def _():
        _modnorm_to(h_ref, x_ref, sc_ref, sh_ref, g_ref, rc)

    o_ref[...] = _dot(h_ref[...], _load_weight(w_ref, wb_ref))


def _inproj_call(x, mod, g_norm, w, layer, tm):
    m, d = x.shape
    _single_tile(m, tm, w)
    depth = g_norm.shape[0]
    wblock, windex = (d, SLAB), lambda i, j: (0, j)
    out_specs = [pl.BlockSpec((None, tm, SLAB), lambda i, j: (j, i, 0)),
                 pl.BlockSpec((tm, d), lambda i, j: (i, 0), pipeline_mode=pl.Buffered(1))]
    out_shape = [jax.ShapeDtypeStruct((SLAB_GATE, m, SLAB), F32), jax.ShapeDtypeStruct((m, d), BF16)]
    if w.master:
        spec, shape = w.emit(wblock, windex)
        out_specs.append(spec)
        out_shape.append(shape)
    return pl.pallas_call(
        functools.partial(_inproj_kernel, rc=_row_chunk(tm)),
        grid=(m // tm, SLAB_GATE),
        in_specs=[
            pl.BlockSpec((tm, d), lambda i, j: (i, 0)),
            mod.spec(1, tm), mod.spec(0, tm),
            pl.BlockSpec((None, 1, d), lambda i, j: (layer, 0, 0)),
            w.spec(wblock, windex),
        ],
        out_specs=out_specs,
        out_shape=out_shape,
        compiler_params=_params("arbitrary", "arbitrary"),
        name="inproj",
    )(x, mod.arr, mod.arr, g_norm.reshape(depth, 1, d), w.arr)


def _gate_kernel(h_ref, w_ref, o_ref, wb_ref=None):
    z = _dot(h_ref[...], _load_weight(w_ref, wb_ref))
    o_ref[...] = (0.5 * jnp.tanh(0.5 * z) + 0.5).astype(o_ref.dtype)


def _gate_call(h, w, tm):
    m, d = h.shape
    _single_tile(m, tm, w)
    per = MERGE_TN // SLAB
    wblock, windex = (d, SLAB), lambda i, j: (0, j)
    out_specs = [pl.BlockSpec((None, tm, SLAB), lambda i, j: (j // per, i, j % per))]
    out_shape = [jax.ShapeDtypeStruct(((N_SLAB - SLAB_GATE) // per, m, MERGE_TN), BF16)]
    if w.master:
        spec, shape = w.emit(wblock, windex)
        out_specs.append(spec)
        out_shape.append(shape)
    return pl.pallas_call(
        _gate_kernel,
        grid=(m // tm, N_SLAB - SLAB_GATE),
        in_specs=[pl.BlockSpec((tm, d), lambda i, j: (i, 0)), w.spec(wblock, windex)],
        out_specs=out_specs,
        out_shape=out_shape,
        compiler_params=_params("arbitrary", "arbitrary"),
        name="ingate",
    )(h, w.arr)


def _group_sum_matrix(width, group):
    idx = np.arange(width) // group
    return jnp.asarray((idx[:, None] == idx[None, :]).astype(np.float32), dtype=BF16)


SHORT_GROUP = 4
RET_CHUNKS_PER_STEP = 4


def _group_size(nb_batch, nblocks, has_init):
    return SHORT_GROUP if has_init and nblocks == 1 and nb_batch % SHORT_GROUP == 0 else 1


def _attn_kernel(*refs, layer, tq, has_init, group):
    if group == 1:
        blocks = [_attn_block(*refs, layer=layer, tq=tq, has_init=has_init)]
    else:
        sinks_ref, q_ref, kvc_ref, kinit_ref, vinit_ref, gq_ref, gk_ref, gm512_ref, gm256_ref, sd_ref, o_ref, kn_ref = refs
        blocks = []
        for bi in range(group):
            rows = pl.ds(bi * tq, tq)
            blocks.append(_attn_block(
                sinks_ref, q_ref.at[:, rows, :], kvc_ref.at[rows, :], kinit_ref.at[bi], vinit_ref.at[bi],
                gq_ref, gk_ref, gm512_ref, gm256_ref, sd_ref, o_ref.at[rows, :], kn_ref.at[rows, :],
                layer=layer, tq=tq, has_init=True))
    while blocks:
        blocks = [b for b in blocks if next(b, _DONE) is not _DONE]


_DONE = object()


def _attn_block(*refs, layer, tq, has_init):
    if has_init:
        sinks_ref, q_ref, kvc_ref, kinit_ref, vinit_ref, gq_ref, gk_ref, gm512_ref, gm256_ref, sd_ref, o_ref, kn_ref = refs
    else:
        sinks_ref, q_ref, kvc_ref, kvp_ref, gq_ref, gk_ref, gm512_ref, gm256_ref, sd_ref, o_ref, kn_ref = refs
    n = pl.program_id(1)

    def qk_norm(x, gmat, g):
        x2 = x * x
        hi = x2.astype(BF16)
        lo = (x2 - hi.astype(F32)).astype(BF16)
        ss = _dot(hi, gmat) + _dot(lo, gmat)
        return x * lax.rsqrt(ss * (1.0 / HEAD_DIM) + EPS) * g

    gm256, gk = gm256_ref[...], gk_ref[...]
    kvc = kvc_ref[...]
    kc = qk_norm(kvc[:, :KV_W], gm256, gk)
    kn_ref[...] = kc
    vc = kvc[:, KV_W:]
    if has_init:
        kp, vp = kinit_ref[...], vinit_ref[...]
    else:
        kvp = kvp_ref[...]
        kp, vp = qk_norm(kvp[:, :KV_W], gm256, gk), kvp[:, KV_W:]
    if tq < WINDOW:
        pad = jnp.zeros((WINDOW - tq, KV_W), F32)
        kc = jnp.concatenate([kc, pad], axis=0)
        vc = jnp.concatenate([vc, pad], axis=0)
    kall = jnp.concatenate([kp, kc], axis=0)
    vall = jnp.concatenate([vp, vc], axis=0)

    nk = 2 * WINDOW
    ii = lax.broadcasted_iota(jnp.int32, (tq, nk), 0)
    jj = lax.broadcasted_iota(jnp.int32, (tq, nk), 1)
    dist = WINDOW + ii - jj
    valid = (dist >= 0) & (dist <= WINDOW)
    if not has_init:
        valid = valid & (jj >= jnp.where(n > 0, 0, WINDOW))
    distm = jnp.where(valid, dist.astype(F32), -NEG_BIG)

    gm512, gq = gm512_ref[...], gq_ref[...] * (HEAD_DIM ** -0.5)
    qn = [qk_norm(q_ref[s], gm512, gq) for s in range(2)]
    lane_k = lax.broadcasted_iota(jnp.int32, (nk, LANES), 1) < HEAD_DIM
    lane_q = lax.broadcasted_iota(jnp.int32, (tq, LANES), 1) < HEAD_DIM
    sd = sd_ref[...]

    def two_copies(a, upper):
        if upper:
            bot = jnp.where(lane_k, 0.0, a)
            top = pltpu.roll(bot, HEAD_DIM, 1)
        else:
            top = jnp.where(lane_k, a, 0.0)
            bot = pltpu.roll(top, HEAD_DIM, 1)
        return jnp.concatenate([top, bot], axis=0).astype(BF16)

    kds, vds = [], []
    for kv in range(N_KV):
        cs = slice(LANES * (kv // 2), LANES * (kv // 2) + LANES)
        kds.append(two_copies(kall[:, cs], bool(kv % 2)))
        vds.append(two_copies(vall[:, cs], bool(kv % 2)))
    npair = N_HEADS // 2
    yield
    scores = []
    for p in range(npair):
        off = LANES * (p % 4)
        qp = qn[p // 4][:, off:off + LANES].astype(BF16)
        scores.append(_nt_dot(qp, kds[p // 2]))
    yield
    probs, sinkw = [], []
    for p in range(npair):
        es, sk = [], []
        for hh in range(2):
            h = 2 * p + hh
            slope = 2.0 ** (-8.0 * (h + 1) / N_HEADS)
            sink = sinks_ref[layer, h]
            sh = scores[p][:, nk * hh:nk * hh + nk] - slope * distm
            mx = jnp.maximum(jnp.max(sh, axis=-1, keepdims=True), sink)
            es.append(jnp.exp(sh - mx))
            sk.append(jnp.exp(sink - mx))
        probs.append(jnp.concatenate(es, axis=1).astype(BF16))
        sinkw.append(jnp.where(lane_q, sk[0], sk[1]))
    yield
    for p in range(npair):
        num = _dot(probs[p], vds[p // 2])
        den = _dot(probs[p], sd) + sinkw[p]
        o_ref[:, LANES * p:LANES * p + LANES] = (num / den).astype(o_ref.dtype)


def _attn_call(z3, sinks, gq_t, gk_t, layer, nb_batch, t_len, cache=None):
    m = z3.shape[1]
    has_init = cache is not None
    tq = WINDOW if t_len % WINDOW == 0 else t_len
    nb = t_len // tq
    assert nb == 1 or not has_init
    depth = gq_t.shape[0]
    gm512 = _group_sum_matrix(SLAB, HEAD_DIM)
    gm256 = _group_sum_matrix(KV_W, HEAD_DIM)
    sd_np = np.zeros((4 * WINDOW, LANES), np.float32)
    sd_np[:2 * WINDOW, :HEAD_DIM] = 1.0
    sd_np[2 * WINDOW:, HEAD_DIM:] = 1.0
    sd = jnp.asarray(sd_np, dtype=BF16)

    group = _group_size(nb_batch, nb, has_init)
    rows = group * tq
    in_specs = [
        pl.BlockSpec(memory_space=pltpu.SMEM),
        pl.BlockSpec((2, rows, SLAB), lambda b, n: (0, b * nb + n, 0)),
        pl.BlockSpec((None, rows, SLAB), lambda b, n: (SLAB_KV, b * nb + n, 0)),
    ]
    args = [sinks, z3, z3]
    if has_init:
        cache_spec = pl.BlockSpec((None, group, WINDOW, KV_W), lambda b, n: (layer, b, 0, 0))
        if group == 1:
            cache_spec = pl.BlockSpec((None, None, WINDOW, KV_W), lambda b, n: (layer, b, 0, 0))
        in_specs += [cache_spec] * 2
        args += [cache[0], cache[1]]
    else:
        in_specs += [pl.BlockSpec((None, WINDOW, SLAB), lambda b, n: (SLAB_KV, jnp.maximum(b * nb + n - 1, 0), 0))]
        args += [z3]
    in_specs += [
        pl.BlockSpec((None, 1, SLAB), lambda b, n: (layer, 0, 0)),
        pl.BlockSpec((None, 1, KV_W), lambda b, n: (layer, 0, 0)),
        pl.BlockSpec((SLAB, SLAB), lambda b, n: (0, 0)),
        pl.BlockSpec((KV_W, KV_W), lambda b, n: (0, 0)),
        pl.BlockSpec((4 * WINDOW, LANES), lambda b, n: (0, 0)),
    ]
    args += [gq_t, gk_t, gm512, gm256, sd]
    out_dtype = F32 if has_init else BF16
    return pl.pallas_call(
        functools.partial(_attn_kernel, layer=layer, tq=tq, has_init=has_init, group=group),
        grid=(nb_batch // group, nb),
        in_specs=in_specs,
        out_specs=[
            pl.BlockSpec((rows, BR_W), lambda b, n: (b * nb + n, 0)),
            pl.BlockSpec((rows, KV_W), lambda b, n: (b * nb + n, 0)),
        ],
        out_shape=[jax.ShapeDtypeStruct((m, BR_W), out_dtype), jax.ShapeDtypeStruct((m, KV_W), F32)],
        compiler_params=_params("arbitrary", "arbitrary"),
        name="attn",
    )(*args)


def _conv_kernel(*refs, tt, nt, has_init):
    if has_init:
        l0, l1, g0, g1, init_ref, w_ref, b_ref, gl_ref, bl_ref, o_ref, new_ref, ext, ybuf, shifted = refs
    else:
        l0, l1, g0, g1, w_ref, b_ref, gl_ref, bl_ref, o_ref, new_ref, ext, ybuf, shifted = refs
    t = pl.program_id(1)

    @pl.when(t == 0)
    def _():
        ext[0:CONV_HALO, :] = init_ref[...] if has_init else jnp.zeros((CONV_HALO, BR_W), F32)

    if nt > 1:
        @pl.when(t > 0)
        def _():
            ext[0:CONV_HALO, :] = ext[tt:tt + CONV_HALO, :]

    for cb, (lr, gr) in enumerate(((l0, g0), (l1, g1))):
        ext[CONV_HALO:CONV_HALO + tt, SLAB * cb:SLAB * cb + SLAB] = lr[...] * jax.nn.sigmoid(gr[...])
    new_ref[...] = ext[tt:tt + CONV_HALO, :]

    rs = min(tt, 32)
    base = CONV_HALO - (CONV_K - 1)
    nsh = shifted.shape[1]
    for c in range(BR_W // LANES):
        cs = slice(LANES * c, LANES * c + LANES)
        for r in range(1, SUBLANES):
            shifted[r - 1] = ext[r:r + nsh, cs]

        def taps(i, carry, cs=cs):
            r0 = pl.multiple_of(i * rs, rs)
            acc = jnp.broadcast_to(b_ref[:, cs], (rs, LANES))
            for r in range(SUBLANES):
                offs = [(base + k) // SUBLANES for k in range(CONV_K) if (base + k) % SUBLANES == r]
                rows = pl.ds(r0 + SUBLANES * offs[0], rs + SUBLANES * (offs[-1] - offs[0]))
                win = ext[rows, cs] if r == 0 else shifted[r - 1, rows, :]
                for a in offs:
                    k = SUBLANES * a + r - base
                    d = SUBLANES * (a - offs[0])
                    acc = acc + w_ref[k:k + 1, cs] * win[d:d + rs]
            ybuf[pl.ds(r0, rs), cs] = acc
            return carry

        lax.fori_loop(0, tt // rs, taps, 0)

    def norm_act(i, carry):
        rows = pl.ds(pl.multiple_of(i * rs, rs), rs)
        y = ybuf[rows, :]
        yc = y - jnp.mean(y, axis=-1, keepdims=True)
        var = jnp.mean(yc * yc, axis=-1, keepdims=True)
        yn = yc * lax.rsqrt(var + EPS) * gl_ref[...] + bl_ref[...]
        o_ref[rows, :] = _silu(yn).astype(o_ref.dtype)
        return carry

    lax.fori_loop(0, tt // rs, norm_act, 0, unroll=min(4, tt // rs))


def _conv_short_kernel(l0, l1, g0, g1, init_ref, w_ref, b_ref, gl_ref, bl_ref, o_ref, new_ref, ext, ybuf, *,
                       nbat, t_len):
    ext[:, 0:SUBLANES, :] = jnp.zeros((nbat, SUBLANES, BR_W), F32)
    ext[:, CONV_HALO - (CONV_K - 1):CONV_HALO, :] = init_ref[...]
    for cb, (lr, gr) in enumerate(((l0, g0), (l1, g1))):
        u = lr[...] * jax.nn.sigmoid(gr[...])
        ext[:, CONV_HALO:CONV_HALO + t_len, SLAB * cb:SLAB * cb + SLAB] = u.reshape(nbat, t_len, SLAB)
    new_ref[...] = ext[:, t_len:t_len + CONV_HALO, :]
    base = CONV_HALO - (CONV_K - 1)
    for c in range(BR_W // LANES):
        cs = slice(LANES * c, LANES * c + LANES)
        acc = jnp.broadcast_to(b_ref[:, cs], (nbat, t_len, LANES))
        for k in range(CONV_K):
            acc = acc + w_ref[k:k + 1, cs] * ext[:, base + k:base + k + t_len, cs]
        ybuf[:, cs] = acc.reshape(nbat * t_len, LANES)
    rs = 32

    def norm_act(i, carry):
        rows = pl.ds(pl.multiple_of(i * rs, rs), rs)
        y = ybuf[rows, :]
        yc = y - jnp.mean(y, axis=-1, keepdims=True)
        var = jnp.mean(yc * yc, axis=-1, keepdims=True)
        yn = yc * lax.rsqrt(var + EPS) * gl_ref[...] + bl_ref[...]
        o_ref[rows, :] = _silu(yn).astype(o_ref.dtype)
        return carry

    lax.fori_loop(0, nbat * t_len // rs, norm_act, 0, unroll=min(4, nbat * t_len // rs))


def _short_batched(nb_batch, t_len, has_init):
    return has_init and t_len == SUBLANES and (nb_batch * t_len) % 32 == 0


def _conv_short_call(z3, init, w_dw, b_dw, g_ln, b_ln, layer, nb_batch, t_len):
    m = z3.shape[1]
    depth = w_dw.shape[0]
    slab = lambda j: pl.BlockSpec((None, m, SLAB), lambda i: (j, 0, 0))
    vec = lambda: pl.BlockSpec((None, 1, BR_W), lambda i: (layer, 0, 0))
    return pl.pallas_call(
        functools.partial(_conv_short_kernel, nbat=nb_batch, t_len=t_len),
        grid=(1,),
        in_specs=[slab(SLAB_CONV), slab(SLAB_CONV + 1), slab(SLAB_CONV + 2), slab(SLAB_CONV + 3),
                  pl.BlockSpec((None, nb_batch, CONV_K - 1, BR_W), lambda i: (layer, 0, 0, 0)),
                  pl.BlockSpec((None, CONV_K, BR_W), lambda i: (layer, 0, 0)), vec(), vec(), vec()],
        out_specs=[pl.BlockSpec((m, BR_W), lambda i: (0, 0)),
                   pl.BlockSpec((nb_batch, CONV_HALO, BR_W), lambda i: (0, 0, 0))],
        out_shape=[jax.ShapeDtypeStruct((m, BR_W), F32), jax.ShapeDtypeStruct((nb_batch, CONV_HALO, BR_W), F32)],
        scratch_shapes=[pltpu.VMEM((nb_batch, CONV_HALO + t_len, BR_W), F32), pltpu.VMEM((m, BR_W), F32)],
        compiler_params=_params("arbitrary"),
        name="conv_short",
    )(z3, z3, z3, z3, init, w_dw, b_dw.reshape(depth, 1, BR_W), g_ln.reshape(depth, 1, BR_W),
      b_ln.reshape(depth, 1, BR_W))


def _conv_call(z3, w_dw, b_dw, g_ln, b_ln, layer, nb_batch, t_len, init=None):
    m = z3.shape[1]
    has_init = init is not None
    if _short_batched(nb_batch, t_len, has_init):
        return _conv_short_call(z3, init, w_dw, b_dw, g_ln, b_ln, layer, nb_batch, t_len)
    if has_init:
        init = jnp.pad(init, ((0, 0), (0, 0), (CONV_HALO - (CONV_K - 1), 0), (0, 0)))
    tt = 512 if t_len % 512 == 0 else (256 if t_len % 256 == 0 else t_len)
    nt = t_len // tt
    assert nt == 1 or tt >= CONV_HALO
    depth = w_dw.shape[0]

    def slab(j):
        return pl.BlockSpec((None, tt, SLAB), lambda b, t: (j, b * nt + t, 0))

    def vec():
        return pl.BlockSpec((None, 1, BR_W), lambda b, t: (layer, 0, 0))

    in_specs = [slab(SLAB_CONV), slab(SLAB_CONV + 1), slab(SLAB_CONV + 2), slab(SLAB_CONV + 3)]
    args = [z3, z3, z3, z3]
    if has_init:
        in_specs.append(pl.BlockSpec((None, None, CONV_HALO, BR_W), lambda b, t: (layer, b, 0, 0)))
        args.append(init)
    in_specs += [pl.BlockSpec((None, CONV_K, BR_W), lambda b, t: (layer, 0, 0)), vec(), vec(), vec()]
    args += [w_dw, b_dw.reshape(depth, 1, BR_W), g_ln.reshape(depth, 1, BR_W), b_ln.reshape(depth, 1, BR_W)]
    return pl.pallas_call(
        functools.partial(_conv_kernel, tt=tt, nt=nt, has_init=has_init),
        grid=(nb_batch, nt),
        in_specs=in_specs,
        out_specs=[
            pl.BlockSpec((tt, BR_W), lambda b, t: (b * nt + t, 0)),
            pl.BlockSpec((None, CONV_HALO, BR_W), lambda b, t: (b, 0, 0)),
        ],
        out_shape=[
            jax.ShapeDtypeStruct((m, BR_W), F32 if has_init else BF16),
            jax.ShapeDtypeStruct((nb_batch, CONV_HALO, BR_W), F32),
        ],
        scratch_shapes=[pltpu.VMEM((CONV_HALO + tt, BR_W), F32), pltpu.VMEM((tt, BR_W), F32),
                        pltpu.VMEM((SUBLANES - 1, CONV_HALO + tt - SUBLANES, LANES), F32)],
        compiler_params=_params("arbitrary", "arbitrary"),
        name="conv",
    )(*args)


def _pool_kernel(*refs, tt, nt, has_init, pos0):
    if has_init:
        u0, u1, init_ref, w_ref, s_ref, o_ref, ext = refs
    else:
        u0, u1, w_ref, s_ref, o_ref, ext = refs
    t = pl.program_id(1)

    @pl.when(t == 0)
    def _():
        ext[0:POOL_HALO, :] = init_ref[...] if has_init else jnp.zeros((POOL_HALO, BR_W), F32)

    if nt > 1:
        @pl.when(t > 0)
        def _():
            ext[0:POOL_HALO, :] = ext[tt:tt + POOL_HALO, :]

    ext[POOL_HALO:POOL_HALO + tt, 0:SLAB] = u0[...]
    ext[POOL_HALO:POOL_HALO + tt, SLAB:2 * SLAB] = u1[...]
    pos = pos0 + t * tt + lax.broadcasted_iota(jnp.int32, (tt, 1), 0)
    for g, w in enumerate(POOL_WINDOWS):
        cs = slice(POOL_G * g, POOL_G * g + POOL_G)
        cur = ext[POOL_HALO:POOL_HALO + tt, cs]
        wsum = cur
        for s in range(1, w):
            wsum = wsum + ext[POOL_HALO - s:POOL_HALO - s + tt, cs]
        cnt = jnp.minimum(pos + 1, w).astype(F32)
        zg = wsum / cnt - cur
        y = _dot(zg.astype(BF16), w_ref[g].astype(BF16)) * s_ref[:, cs]
        o_ref[:, cs] = y.astype(o_ref.dtype)


def _pool_short_kernel(u0, u1, init_ref, w_ref, s_ref, o_ref, ext, *, nbat, t_len, pos0):
    ext[:, 0:SUBLANES, :] = jnp.zeros((nbat, SUBLANES, BR_W), F32)
    ext[:, POOL_HALO - POOL_PAD:POOL_HALO, :] = init_ref[...]
    ext[:, POOL_HALO:POOL_HALO + t_len, 0:SLAB] = u0[...].reshape(nbat, t_len, SLAB)
    ext[:, POOL_HALO:POOL_HALO + t_len, SLAB:2 * SLAB] = u1[...].reshape(nbat, t_len, SLAB)
    pos = pos0 + lax.broadcasted_iota(jnp.int32, (nbat, t_len, POOL_G), 1)
    for g, w in enumerate(POOL_WINDOWS):
        cs = slice(POOL_G * g, POOL_G * g + POOL_G)
        cur = ext[:, POOL_HALO:POOL_HALO + t_len, cs]
        wsum = cur
        for s in range(1, w):
            wsum = wsum + ext[:, POOL_HALO - s:POOL_HALO - s + t_len, cs]
        cnt = jnp.minimum(pos + 1, w).astype(F32)
        zg = (wsum / cnt - cur).reshape(nbat * t_len, POOL_G)
        y = _dot(zg.astype(BF16), w_ref[g].astype(BF16)) * s_ref[:, cs]
        o_ref[:, cs] = y.astype(o_ref.dtype)


def _pool_short_call(z3, init, w_pool, s_pool, layer, nb_batch, t_len, pos0):
    m = z3.shape[1]
    depth = w_pool.shape[0]
    slab = lambda j: pl.BlockSpec((None, m, SLAB), lambda i: (j, 0, 0))
    return pl.pallas_call(
        functools.partial(_pool_short_kernel, nbat=nb_batch, t_len=t_len, pos0=pos0),
        grid=(1,),
        in_specs=[slab(SLAB_POOL), slab(SLAB_POOL + 1),
                  pl.BlockSpec((None, nb_batch, POOL_PAD, BR_W), lambda i: (layer, 0, 0, 0)),
                  pl.BlockSpec((None, len(POOL_WINDOWS), POOL_G, POOL_G), lambda i: (layer, 0, 0, 0)),
                  pl.BlockSpec((None, 1, BR_W), lambda i: (layer, 0, 0))],
        out_specs=pl.BlockSpec((m, BR_W), lambda i: (0, 0)),
        out_shape=jax.ShapeDtypeStruct((m, BR_W), F32),
        scratch_shapes=[pltpu.VMEM((nb_batch, POOL_HALO + t_len, BR_W), F32)],
        compiler_params=_params("arbitrary"),
        name="pool_short",
    )(z3, z3, init, w_pool, s_pool.reshape(depth, 1, BR_W))


def _pool_call(z3, w_pool, s_pool, layer, nb_batch, t_len, pos0, init=None):
    m = z3.shape[1]
    has_init = init is not None
    if _short_batched(nb_batch, t_len, has_init):
        return _pool_short_call(z3, init, w_pool, s_pool, layer, nb_batch, t_len, pos0)
    if has_init:
        init = jnp.pad(init, ((0, 0), (0, 0), (POOL_HALO - POOL_PAD, 0), (0, 0)))
    tt = 256 if t_len % 256 == 0 else t_len
    nt = t_len // tt
    assert nt == 1 or tt >= POOL_HALO
    depth = w_pool.shape[0]

    def slab(j):
        return pl.BlockSpec((None, tt, SLAB), lambda b, t: (j, b * nt + t, 0))

    in_specs = [slab(SLAB_POOL), slab(SLAB_POOL + 1)]
    args = [z3, z3]
    if has_init:
        in_specs.append(pl.BlockSpec((None, None, POOL_HALO, BR_W), lambda b, t: (layer, b, 0, 0)))
        args.append(init)
    in_specs += [
        pl.BlockSpec((None, len(POOL_WINDOWS), POOL_G, POOL_G), lambda b, t: (layer, 0, 0, 0)),
        pl.BlockSpec((None, 1, BR_W), lambda b, t: (layer, 0, 0)),
    ]
    args += [w_pool, s_pool.reshape(depth, 1, BR_W)]
    return pl.pallas_call(
        functools.partial(_pool_kernel, tt=tt, nt=nt, has_init=has_init, pos0=pos0),
        grid=(nb_batch, nt),
        in_specs=in_specs,
        out_specs=pl.BlockSpec((tt, BR_W), lambda b, t: (b * nt + t, 0)),
        out_shape=jax.ShapeDtypeStruct((m, BR_W), F32 if has_init else BF16),
        scratch_shapes=[pltpu.VMEM((POOL_HALO + tt, BR_W), F32)],
        compiler_params=_params("arbitrary", "arbitrary"),
        name="pool",
    )(*args)


def _ret_constants(c, cp):
    lg = np.log1p(-np.exp2(-5.0 - np.arange(RET_HEADS, dtype=np.float64)))
    i = np.arange(c, dtype=np.float64)
    diff = i[:, None] - i[None, :]
    decay = np.where(diff >= 0, np.exp(lg[:, None, None] * np.maximum(diff, 0.0)), 0.0)
    dec = np.zeros((RET_HEADS // 2, c, 2 * cp))
    for h in range(RET_HEADS):
        dec[h // 2, :, (h % 2) * cp:(h % 2) * cp + c] = decay[h]
    kfac = np.repeat(np.exp(lg[None, :] * (c - 1 - i)[:, None]), RET_DK, axis=1) * RET_DK ** -0.5
    cfac = np.repeat(np.exp(lg[None, :] * (i + 1)[:, None]), RET_DV, axis=1)
    gch = np.repeat(np.exp(lg * c), RET_DK).reshape(RET_HEADS // 2, 2 * RET_DK, 1)
    gch = np.broadcast_to(gch, (RET_HEADS // 2, 2 * RET_DK, RET_DV))
    f = lambda a: jnp.asarray(np.ascontiguousarray(a), dtype=F32)
    return f(dec), f(kfac), f(cfac), f(gch)


def _ret_kernel(*refs, c, cp, has_init, group, cps):
    if group == 1 and cps == 1:
        return _ret_chunk(*refs, c=c, cp=cp, has_init=has_init)
    if group == 1:
        ins, rest = refs[:6], refs[6:-2]
        o_ref, s_ref = refs[-2:]
        for ci in range(cps):
            rows = pl.ds(ci * c, c)
            _ret_chunk(*[r.at[rows, :] for r in ins], *rest, o_ref.at[rows, :], s_ref,
                       c=c, cp=cp, has_init=has_init, first=ci == 0)
        return
    rq, rk, rv0, rv1, rg0, rg1, s0_ref, gn_ref, dec_ref, kf_ref, cf_ref, gch_ref, o_ref, s_ref = refs
    for bi in range(group):
        rows = pl.ds(bi * c, c)
        _ret_chunk(rq.at[rows, :], rk.at[rows, :], rv0.at[rows, :], rv1.at[rows, :], rg0.at[rows, :], rg1.at[rows, :],
                   s0_ref.at[bi], gn_ref, dec_ref, kf_ref, cf_ref, gch_ref, o_ref.at[rows, :], s_ref.at[bi],
                   c=c, cp=cp, has_init=True)


def _ret_chunk(*refs, c, cp, has_init, first=True):
    if has_init:
        rq, rk, rv0, rv1, rg0, rg1, s0_ref, gn_ref, dec_ref, kf_ref, cf_ref, gch_ref, o_ref, s_ref = refs
    else:
        rq, rk, rv0, rv1, rg0, rg1, gn_ref, dec_ref, kf_ref, cf_ref, gch_ref, o_ref, s_ref = refs
    n = pl.program_id(1)

    if first:
        @pl.when(n == 0)
        def _():
            s_ref[...] = s0_ref[...] if has_init else jnp.zeros(s_ref.shape, F32)

    lo = lax.broadcasted_iota(jnp.int32, (c, LANES), 1) < RET_DK
    q = rq[...]
    kraw = rk[...]
    k = kraw * (RET_DK ** -0.5)
    kdec = kraw * kf_ref[...]
    rvs, rgs = (rv0, rv1), (rg0, rg1)

    def stack_heads(a):
        a0, a1 = jnp.where(lo, a, 0.0), jnp.where(lo, 0.0, a)
        if cp > c:
            z = jnp.zeros((cp - c, LANES), F32)
            return jnp.concatenate([a0, z, a1, z], axis=0)
        return jnp.concatenate([a0, a1], axis=0)

    npair = RET_HEADS // 2
    zc = jnp.zeros((c, LANES), F32)
    scores, vbds, vsts, inners, crosses = [], [], [], [], []
    for p in range(npair):
        cs = slice(LANES * p, LANES * p + LANES)
        scores.append(_nt_dot(q[:, cs].astype(BF16), stack_heads(k[:, cs]).astype(BF16)) * dec_ref[p])
        vs = [rvs[h // 4][:, LANES * (h % 4):LANES * (h % 4) + LANES] for h in (2 * p, 2 * p + 1)]
        rows0 = jnp.concatenate([vs[0], zc], axis=1)
        rows1 = jnp.concatenate([zc, vs[1]], axis=1)
        if cp > c:
            zp = jnp.zeros((cp - c, 2 * LANES), F32)
            zq = jnp.zeros((cp - c, LANES), F32)
            vbds.append(jnp.concatenate([rows0, zp, rows1, zp], axis=0).astype(BF16))
            vsts.append(jnp.concatenate([vs[0], zq, vs[1], zq], axis=0).astype(BF16))
        else:
            vbds.append(jnp.concatenate([rows0, rows1], axis=0).astype(BF16))
            vsts.append(jnp.concatenate([vs[0], vs[1]], axis=0).astype(BF16))
    for p in range(npair):
        cs = slice(LANES * p, LANES * p + LANES)
        qp = q[:, cs]
        inners.append(_dot(scores[p].astype(BF16), vbds[p]))
        sprev = s_ref[p]
        qst = jnp.concatenate([jnp.where(lo, qp, 0.0), jnp.where(lo, 0.0, qp)], axis=0)
        crosses.append(_dot(qst.astype(BF16), sprev.astype(BF16)))
        upd = _tn_dot(stack_heads(kdec[:, cs]).astype(BF16), vsts[p])
        s_ref[p] = gch_ref[p] * sprev + upd
    for p in range(npair):
        inner, cross = inners[p], crosses[p]
        for hh, h in enumerate((2 * p, 2 * p + 1)):
            hc = slice(LANES * h, LANES * h + LANES)
            o = inner[:, LANES * hh:LANES * hh + LANES] + cross[c * hh:c * hh + c, :] * cf_ref[:, hc]
            oc = o - jnp.mean(o, axis=-1, keepdims=True)
            var = jnp.mean(oc * oc, axis=-1, keepdims=True)
            gate = rgs[h // 4][:, LANES * (h % 4):LANES * (h % 4) + LANES]
            y = oc * lax.rsqrt(var + EPS) * gn_ref[:, hc] * _silu(gate)
            o_ref[:, hc] = y.astype(o_ref.dtype)


def _ret_call(z3, g_ret, layer, nb_batch, t_len, init=None):
    m = z3.shape[1]
    has_init = init is not None
    c = RET_CHUNK if t_len % RET_CHUNK == 0 else t_len
    cp = max(c, 64)
    nc = t_len // c
    depth = g_ret.shape[0]
    dec, kfac, cfac, gch = _ret_constants(c, cp)
    npair = RET_HEADS // 2

    group = _group_size(nb_batch, nc, has_init)
    cps = RET_CHUNKS_PER_STEP if group == 1 and nc % RET_CHUNKS_PER_STEP == 0 else 1
    ns = nc // cps
    rows = group * cps * c
    state_block = (npair, LANES, RET_DV) if group == 1 else (group, npair, LANES, RET_DV)
    lead = (None,) if group == 1 else ()

    def slab(j):
        return pl.BlockSpec((None, rows, SLAB), lambda b, n: (j, b * ns + n, 0))

    def const(shape):
        nd = len(shape)
        return pl.BlockSpec(shape, lambda b, n: (0,) * nd)

    in_specs = [slab(SLAB_RQ), slab(SLAB_RK), slab(SLAB_RV), slab(SLAB_RV + 1), slab(SLAB_RG), slab(SLAB_RG + 1)]
    args = [z3] * 6
    if has_init:
        in_specs.append(pl.BlockSpec((None,) + lead + state_block, lambda b, n: (layer, b, 0, 0, 0)))
        args.append(init)
    in_specs += [pl.BlockSpec((None, 1, BR_W), lambda b, n: (layer, 0, 0)),
                 const(dec.shape), const(kfac.shape), const(cfac.shape), const(gch.shape)]
    args += [g_ret.reshape(depth, 1, BR_W), dec, kfac, cfac, gch]
    return pl.pallas_call(
        functools.partial(_ret_kernel, c=c, cp=cp, has_init=has_init, group=group, cps=cps),
        grid=(nb_batch // group, ns),
        in_specs=in_specs,
        out_specs=[
            pl.BlockSpec((rows, BR_W), lambda b, n: (b * ns + n, 0)),
            pl.BlockSpec(lead + state_block, lambda b, n: (b, 0, 0, 0)),
        ],
        out_shape=[
            jax.ShapeDtypeStruct((m, BR_W), F32 if has_init else BF16),
            jax.ShapeDtypeStruct((nb_batch, npair, LANES, RET_DV), F32),
        ],
        compiler_params=_params("arbitrary", "arbitrary"),
        name="retention",
    )(*args)


def _merge_kernel(y0, y1, y2, y3, gate_ref, w_ref, o_ref, *rest):
    wb_ref, acc_ref = rest if len(rest) == 2 else (None, rest[0])
    r, c = pl.program_id(1), pl.program_id(2)
    if wb_ref is not None:
        wb_ref[...] = w_ref[...].astype(BF16)
        w_ref = wb_ref
    for k, y_ref in enumerate((y0, y1, y2, y3)):
        @pl.when(r == k)
        def _(k=k, y_ref=y_ref):
            val = gate_ref[...].astype(F32) * _dot(y_ref[...].astype(BF16), w_ref[...])
            if k == 0:
                acc_ref[c] = val
            elif k < N_BR - 1:
                acc_ref[c] += val
            else:
                o_ref[...] = (acc_ref[c] + val).astype(o_ref.dtype)


def _merge_call(ys, gates, w, tm):
    m = gates.shape[1]
    _single_tile(m, tm, w)
    tn = MERGE_TN
    nc = D_MODEL // tn
    y_spec = pl.BlockSpec((tm, BR_W), lambda i, r, c: (i, 0))
    wblock, windex = (None, BR_W, tn), lambda i, r, c: (r, 0, c)
    out_specs = [pl.BlockSpec((tm, tn), lambda i, r, c: (i, jnp.where(r == N_BR - 1, c, 0)))]
    out_shape = [jax.ShapeDtypeStruct((m, D_MODEL), BF16)]
    if w.master:
        spec, shape = w.emit(wblock, windex)
        out_specs.append(spec)
        out_shape.append(shape)
    return pl.pallas_call(
        _merge_kernel,
        grid=(m // tm, N_BR, nc),
        in_specs=[y_spec] * N_BR + [
            pl.BlockSpec((None, tm, tn), lambda i, r, c: (nc * r + c, i, 0)),
            w.spec(wblock, windex),
        ],
        out_specs=out_specs,
        out_shape=out_shape,
        scratch_shapes=[pltpu.VMEM((nc, tm, tn), F32)],
        compiler_params=_params("arbitrary", "arbitrary", "arbitrary"),
        name="merge",
    )(*ys, gates, w.arr)


def _outproj_kernel(m_ref, w_ref, x_ref, gt_ref, o_ref, wb_ref=None):
    o_ref[...] = x_ref[...] + gt_ref[...] * _dot(m_ref[...], _load_weight(w_ref, wb_ref))


def _outproj_call(merged, x, mod, w, tm):
    m, d = x.shape
    _single_tile(m, tm, w)
    wblock, windex = (d, SLAB), lambda i, c: (0, c)
    out_specs = [pl.BlockSpec((tm, SLAB), lambda i, c: (i, c))]
    out_shape = [jax.ShapeDtypeStruct((m, d), F32)]
    if w.master:
        spec, shape = w.emit(wblock, windex)
        out_specs.append(spec)
        out_shape.append(shape)
    return pl.pallas_call(
        _outproj_kernel,
        grid=(m // tm, d // SLAB),
        in_specs=[
            pl.BlockSpec((tm, d), lambda i, c: (i, 0)),
            w.spec(wblock, windex),
            pl.BlockSpec((tm, SLAB), lambda i, c: (i, c)),
            mod.spec(2, tm, width=SLAB, col=1),
        ],
        out_specs=out_specs,
        out_shape=out_shape,
        compiler_params=_params("arbitrary", "arbitrary"),
        name="outproj",
    )(merged, w.arr, x, mod.arr)


def _mlp_kernel(x_ref, sc_ref, sh_ref, gt_ref, g_ref, w1_ref, w2_ref, o_ref, *rest, rc, nf):
    (w1b_ref, w2b_ref, h_ref) = rest if len(rest) == 3 else (None, None, rest[0])
    f = pl.program_id(1)

    @pl.when(f == 0)
    def _():
        _modnorm_to(h_ref, x_ref, sc_ref, sh_ref, g_ref, rc)
        o_ref[...] = jnp.zeros(o_ref.shape, F32)

    a = _dot(h_ref[...], _load_weight(w1_ref, w1b_ref))
    a = jnp.square(jnp.maximum(a, 0.0)).astype(BF16)
    if w2b_ref is not None:
        w2b_ref[...] = w2_ref[...].astype(BF16)
        w2_ref = w2b_ref
    for c0 in range(0, o_ref.shape[1], SLAB):
        o_ref[:, c0:c0 + SLAB] += _dot(a, w2_ref[:, c0:c0 + SLAB])

    @pl.when(f == nf - 1)
    def _():
        def body(cidx, carry):
            r = pl.multiple_of(cidx * rc, rc)
            rows = pl.ds(r, rc)
            o_ref[rows, :] = x_ref[rows, :] + _mod_rows(gt_ref, r, rc) * o_ref[rows, :]
            return carry

        lax.fori_loop(0, x_ref.shape[0] // rc, body, 0)


def _mlp_call(x, mod, g_norm, w1, w2, layer, tm, tf):
    m, d = x.shape
    _single_tile(m, tm, w1)
    assert w1.master == w2.master
    depth = g_norm.shape[0]
    nf = w1.shape[1] // tf
    w1block, w1index = (d, tf), lambda i, f: (0, f)
    w2block, w2index = (tf, d), lambda i, f: (f, 0)
    out_specs = [pl.BlockSpec((tm, d), lambda i, f: (i, 0))]
    out_shape = [jax.ShapeDtypeStruct((m, d), F32)]
    if w1.master:
        for spec, shape in (w1.emit(w1block, w1index), w2.emit(w2block, w2index)):
            out_specs.append(spec)
            out_shape.append(shape)
    return pl.pallas_call(
        functools.partial(_mlp_kernel, rc=_row_chunk(tm), nf=nf),
        grid=(m // tm, nf),
        in_specs=[
            pl.BlockSpec((tm, d), lambda i, f: (i, 0)),
            mod.spec(4, tm), mod.spec(3, tm), mod.spec(5, tm),
            pl.BlockSpec((None, 1, d), lambda i, f: (layer, 0, 0)),
            w1.spec(w1block, w1index),
            w2.spec(w2block, w2index),
        ],
        out_specs=out_specs,
        out_shape=out_shape,
        scratch_shapes=[pltpu.VMEM((tm, d), BF16)],
        compiler_params=_params("arbitrary", "arbitrary"),
        name="mlp",
    )(x, mod.arr, mod.arr, mod.arr, g_norm.reshape(depth, 1, d), w1.arr, w2.arr)


def _layer(x, mod, p, w, layer, nb_batch, t_len, tm_in, tm, tf, cache):
    copies = {}

    def split(outs, *names):
        outs = list(outs)
        for name in reversed(names):
            if w[name].master:
                copies[name] = outs.pop()
        return outs if len(outs) > 1 else outs[0]

    z3, h = split(_inproj_call(x, mod, p["g_norm1"], w["in_mix"], layer, tm_in), "in_mix")
    gates = split(_gate_call(h, w["in_gate"], tm_in), "in_gate")
    if cache is None:
        attn_cache = conv_init = pool_init = ret_init = None
        pos0 = 0
    else:
        attn_cache, conv_init, pool_init, ret_init = cache
        pos0 = PAST_LEN
    y_att, k_norm = _attn_call(z3, p["attn_sinks"], p["gq_t"], p["gk_t"], layer, nb_batch, t_len, attn_cache)
    y_conv, conv_tail = _conv_call(z3, p["w_dw"], p["b_dw"], p["g_conv_ln"], p["b_conv_ln"], layer, nb_batch, t_len,
                                   conv_init)
    y_pool = _pool_call(z3, p["w_pool"], p["s_pool"], layer, nb_batch, t_len, pos0, pool_init)
    y_ret, s_new = _ret_call(z3, p["g_ret_norm"], layer, nb_batch, t_len, ret_init)
    merged = split(_merge_call((y_att, y_conv, y_pool, y_ret), gates, w["br"], tm), "br")
    x = split(_outproj_call(merged, x, mod, w["out"], tm_in), "out")
    x = split(_mlp_call(x, mod, p["g_norm2"], w["mlp1"], w["mlp2"], layer, tm, tf), "mlp1", "mlp2")
    return x, (z3, k_norm, conv_tail, s_new), copies


def kernel(x_prompt, x_sample, c_prompt, c_sample, cache_attn_k, cache_attn_v, state_conv, state_pool, state_ret,
           w_ada, b_ada, g_norm1, g_norm2, w_in, g_qnorm, g_knorm, attn_sinks, w_dw, b_dw, g_conv_ln, b_conv_ln,
           w_pool, s_pool, g_ret_norm, w_br, w_out, w_mlp1, w_mlp2):
    nb, t_len, d = x_prompt.shape
    nsb, st_len, _ = x_sample.shape
    depth = w_ada.shape[0]
    mp, ms = nb * t_len, nsb * st_len

    n_c = nb + nsb
    r_pad = -(-n_c // 16) * 16
    c_all = jnp.concatenate([c_prompt, c_sample, jnp.zeros((r_pad - n_c, d), F32)], axis=0)
    ada = _ada_call(c_all, w_ada, b_ada)
    ada_p = ada[:, :nb].reshape(depth, nb, 1, 6 * d)
    ada_s = jnp.repeat(ada[:, nb:n_c], st_len, axis=1)

    p = dict(g_norm1=g_norm1, g_norm2=g_norm2, attn_sinks=attn_sinks, w_dw=w_dw, b_dw=b_dw,
             g_conv_ln=g_conv_ln, b_conv_ln=b_conv_ln, w_pool=w_pool, s_pool=s_pool, g_ret_norm=g_ret_norm,
             gq_t=jnp.tile(g_qnorm, (1, SLAB // HEAD_DIM)).reshape(depth, 1, SLAB),
             gk_t=jnp.tile(g_knorm, (1, N_KV)).reshape(depth, 1, KV_W))

    tm_p = 1024 if t_len % 1024 == 0 else t_len
    tm_p_in = 2048 if t_len % 2048 == 0 else tm_p
    tm_s = ms
    n_mix = SLAB_GATE * SLAB
    cache_k2 = cache_attn_k.reshape(depth, nsb, WINDOW, KV_W)
    cache_v2 = cache_attn_v.reshape(depth, nsb, WINDOW, KV_W)
    ret_init = state_ret.reshape(depth, nsb, RET_HEADS // 2, 2 * RET_DK, RET_DV)

    xp = x_prompt.reshape(mp, d)
    xs = x_sample.reshape(ms, d)
    st_p = [[] for _ in range(5)]
    st_s = [[] for _ in range(5)]
    for l in range(depth):
        masters = dict(in_mix=_Weight(w_in, l, shape=(d, n_mix)),
                       in_gate=_Weight(w_in, l, shape=(d, N_IN - n_mix), col0=SLAB_GATE),
                       br=_Weight(w_br, l), out=_Weight(w_out, l), mlp1=_Weight(w_mlp1, l), mlp2=_Weight(w_mlp2, l))
        mod_s = _Mod(ada_s, l, True, st_len)
        cache = ((cache_k2, cache_v2), state_conv, state_pool, ret_init)
        xs, (z3, k_norm, conv_tail, s_new), copies = _layer(xs, mod_s, p, masters, l, nsb, st_len, tm_s, tm_s,
                                                            MLP_TF_MASTER, cache)
        st_s[0].append(k_norm.reshape(nsb, st_len, N_KV, HEAD_DIM))
        st_s[1].append(z3[SLAB_KV].reshape(nsb, st_len, SLAB)[:, :, KV_W:].reshape(nsb, st_len, N_KV, HEAD_DIM))
        st_s[2].append(conv_tail)
        pool_u = jnp.moveaxis(z3[SLAB_POOL:SLAB_POOL + 2].reshape(2, nsb, st_len, SLAB), 0, 2)
        st_s[3].append(pool_u.reshape(nsb, st_len, BR_W))
        st_s[4].append(s_new.reshape(nsb, RET_HEADS, RET_DK, RET_DV))

        mod_p = _Mod(ada_p, l, False, t_len)
        wcopy = {name: _Weight(arr) for name, arr in copies.items()}
        xp, (z3, k_norm, conv_tail, s_new), _ = _layer(xp, mod_p, p, wcopy, l, nb, t_len, tm_p_in, tm_p, MLP_TF, None)
        z4 = z3.reshape(SLAB_GATE, nb, t_len, SLAB)
        st_p[0].append(k_norm.reshape(nb, t_len, KV_W)[:, -WINDOW:].reshape(nb, WINDOW, N_KV, HEAD_DIM))
        st_p[1].append(z4[SLAB_KV, :, -WINDOW:, KV_W:].reshape(nb, WINDOW, N_KV, HEAD_DIM))
        st_p[2].append(conv_tail[:, CONV_HALO - (CONV_K - 1):])
        pool_u = z4[SLAB_POOL:SLAB_POOL + 2, :, -POOL_PAD:]
        st_p[3].append(jnp.moveaxis(pool_u, 0, 2).reshape(nb, POOL_PAD, BR_W))
        st_p[4].append(s_new.reshape(nb, RET_HEADS, RET_DK, RET_DV))

    def rolled(old, new_rows, keep):
        return jnp.concatenate([old, jnp.stack(new_rows)], axis=2)[:, :, -keep:]

    sample_states = (rolled(cache_attn_k, st_s[0], WINDOW), rolled(cache_attn_v, st_s[1], WINDOW),
                     jnp.stack(st_s[2])[:, :, CONV_HALO - (CONV_K - 1):], rolled(state_pool, st_s[3], POOL_PAD),
                     jnp.stack(st_s[4]))
    return (xp.reshape(nb, t_len, d), xs.reshape(nsb, st_len, d), *[jnp.stack(a) for a in st_p], *sample_states)
```

```python
import functools

import numpy as np
import jax
import jax.numpy as jnp
from jax import lax
from jax.experimental import pallas as pl
from jax.experimental.pallas import tpu as pltpu

F32 = jnp.float32
BF16 = jnp.bfloat16

D_MODEL = 2048
PAST_LEN = 16384
N_HEADS = 16
HEAD_DIM = 64
N_KV = 4
WINDOW = 128
BR_W = 1024
CONV_K = 31
POOL_WINDOWS = (2, 4, 8, 16)
POOL_G = 256
POOL_PAD = 15
RET_HEADS = 8
RET_DK = 64
RET_DV = 128
RET_CHUNK = 128
N_BR = 4
D_FF = 4 * D_MODEL
MLP_TF = 1024
MLP_TF_MASTER = 512
MERGE_TN = 2048
EPS = 1e-6
KV_W = N_KV * HEAD_DIM
N_IN = 15872

SLAB = 512
N_SLAB = N_IN // SLAB
SLAB_Q, SLAB_KV, SLAB_CONV, SLAB_POOL, SLAB_RQ, SLAB_RK, SLAB_RV, SLAB_RG, SLAB_GATE = 0, 2, 3, 7, 9, 10, 11, 13, 15

VMEM_LIMIT_BYTES = 60 * 1024 * 1024
LANES = 128
SUBLANES = 8
NEG_BIG = -1e30
CONV_HALO = 32
POOL_HALO = 16


def _params(*sem):
    return pltpu.CompilerParams(dimension_semantics=sem, vmem_limit_bytes=VMEM_LIMIT_BYTES)


def _nt_dot(a, b):
    return lax.dot_general(a, b, (((1,), (1,)), ((), ())), preferred_element_type=F32)


def _tn_dot(a, b):
    return lax.dot_general(a, b, (((0,), (0,)), ((), ())), preferred_element_type=F32)


def _dot(a, b):
    return jnp.dot(a, b, preferred_element_type=F32)


def _silu(x):
    return x * jax.nn.sigmoid(x)


def _ada_kernel(c_ref, w_ref, b_ref, o_ref):
    s = _silu(c_ref[...]).astype(BF16)
    o_ref[...] = _dot(s, w_ref[...].astype(BF16)) + b_ref[...]


def _ada_call(c_all, w_ada, b_ada):
    depth, d, n = w_ada.shape
    r = c_all.shape[0]
    tn = 1024
    return pl.pallas_call(
        _ada_kernel,
        grid=(depth, n // tn),
        in_specs=[
            pl.BlockSpec((r, d), lambda l, j: (0, 0)),
            pl.BlockSpec((None, d, tn), lambda l, j: (l, 0, j)),
            pl.BlockSpec((None, 1, tn), lambda l, j: (l, 0, j)),
        ],
        out_specs=pl.BlockSpec((None, r, tn), lambda l, j: (l, 0, j)),
        out_shape=jax.ShapeDtypeStruct((depth, r, n), F32),
        compiler_params=_params("arbitrary", "arbitrary"),
        name="ada",
    )(c_all, w_ada, b_ada.reshape(depth, 1, n))


class _Mod:
    def __init__(self, arr, layer, per_row, t_len):
        self.arr, self.l, self.per_row, self.t_len = arr, layer, per_row, t_len

    def spec(self, k, tm, width=D_MODEL, col=None):
        l, tpb, nb = self.l, max(self.t_len // tm, 1), D_MODEL // width

        def cidx(idx):
            return k * nb + (idx[col] if col is not None else 0)

        if self.per_row:
            return pl.BlockSpec((None, tm, width), lambda *idx: (l, idx[0], cidx(idx)))
        return pl.BlockSpec((None, None, 1, width), lambda *idx: (l, idx[0] // tpb, 0, cidx(idx)))


def _mod_rows(ref, r, rc):
    return ref[...] if ref.shape[0] == 1 else ref[pl.ds(r, rc), :]


def _modnorm_to(h_ref, x_ref, sc_ref, sh_ref, g_ref, rc):
    g = g_ref[...]

    def body(c, carry):
        r = pl.multiple_of(c * rc, rc)
        x = x_ref[pl.ds(r, rc), :]
        ms = jnp.mean(x * x, axis=-1, keepdims=True)
        y = x * lax.rsqrt(ms + EPS) * g
        h = y * (1.0 + _mod_rows(sc_ref, r, rc)) + _mod_rows(sh_ref, r, rc)
        h_ref[pl.ds(r, rc), :] = h.astype(h_ref.dtype)
        return carry

    lax.fori_loop(0, x_ref.shape[0] // rc, body, 0)


def _row_chunk(tm):
    return 128 if tm % 128 == 0 else tm


class _Weight:
    def __init__(self, arr, layer=None, shape=None, col0=0):
        self.arr, self.layer, self.col0 = arr, layer, col0
        self.shape = tuple(shape if shape is not None else (arr.shape[1:] if layer is not None else arr.shape))

    @property
    def master(self):
        return self.layer is not None

    def spec(self, block, index):
        if not self.master:
            return pl.BlockSpec(block, index)
        layer, col0 = self.layer, self.col0

        def master_index(*g):
            idx = tuple(index(*g))
            return (layer,) + idx[:-1] + (idx[-1] + col0,)

        return pl.BlockSpec((None,) + tuple(block), master_index)

    def emit(self, block, index):
        return pl.BlockSpec(block, index), jax.ShapeDtypeStruct(self.shape, BF16)


def _load_weight(w_ref, copy_ref):
    if copy_ref is None:
        return w_ref[...]
    w = w_ref[...].astype(BF16)
    copy_ref[...] = w
    return w


def _single_tile(m, tm, w):
    assert not w.master or m == tm, "a master weight must be streamed by a single row tile"


def _inproj_kernel(x_ref, sc_ref, sh_ref, g_ref, w_ref, o_ref, h_ref, wb_ref=None, *, rc):
    @pl.when(pl.program_id(1) == 0)
    def _():
        _modnorm_to(h_ref, x_ref, sc_ref, sh_ref, g_ref, rc)

    o_ref[...] = _dot(h_ref[...], _load_weight(w_ref, wb_ref))


def _inproj_call(x, mod, g_norm, w, layer, tm):
    m, d = x.shape
    _single_tile(m, tm, w)
    depth = g_norm.shape[0]
    wblock, windex = (d, SLAB), lambda i, j: (0, j)
    out_specs = [pl.BlockSpec((None, tm, SLAB), lambda i, j: (j, i, 0)),
                 pl.BlockSpec((tm, d), lambda i, j: (i, 0), pipeline_mode=pl.Buffered(1))]
    out_shape = [jax.ShapeDtypeStruct((SLAB_GATE, m, SLAB), F32), jax.ShapeDtypeStruct((m, d), BF16)]
    if w.master:
        spec, shape = w.emit(wblock, windex)
        out_specs.append(spec)
        out_shape.append(shape)
    return pl.pallas_call(
        functools.partial(_inproj_kernel, rc=_row_chunk(tm)),
        grid=(m // tm, SLAB_GATE),
        in_specs=[
            pl.BlockSpec((tm, d), lambda i, j: (i, 0)),
            mod.spec(1, tm), mod.spec(0, tm),
            pl.BlockSpec((None, 1, d), lambda i, j: (layer, 0, 0)),
            w.spec(wblock, windex),
        ],
        out_specs=out_specs,
        out_shape=out_shape,
        compiler_params=_params("arbitrary", "arbitrary"),
        name="inproj",
    )(x, mod.arr, mod.arr, g_norm.reshape(depth, 1, d), w.arr)


def _gate_kernel(h_ref, w_ref, o_ref, wb_ref=None):
    z = _dot(h_ref[...], _load_weight(w_ref, wb_ref))
    o_ref[...] = (0.5 * jnp.tanh(0.5 * z) + 0.5).astype(o_ref.dtype)


def _gate_call(h, w, tm):
    m, d = h.shape
    _single_tile(m, tm, w)
    per = MERGE_TN // SLAB
    wblock, windex = (d, SLAB), lambda i, j: (0, j)
    out_specs = [pl.BlockSpec((None, tm, SLAB), lambda i, j: (j // per, i, j % per))]
    out_shape = [jax.ShapeDtypeStruct(((N_SLAB - SLAB_GATE) // per, m, MERGE_TN), BF16)]
    if w.master:
        spec, shape = w.emit(wblock, windex)
        out_specs.append(spec)
        out_shape.append(shape)
    return pl.pallas_call(
        _gate_kernel,
        grid=(m // tm, N_SLAB - SLAB_GATE),
        in_specs=[pl.BlockSpec((tm, d), lambda i, j: (i, 0)), w.spec(wblock, windex)],
        out_specs=out_specs,
        out_shape=out_shape,
        compiler_params=_params("arbitrary", "arbitrary"),
        name="ingate",
    )(h, w.arr)


def _group_sum_matrix(width, group):
    idx = np.arange(width) // group
    return jnp.asarray((idx[:, None] == idx[None, :]).astype(np.float32), dtype=BF16)


SHORT_GROUP = 4
RET_CHUNKS_PER_STEP = 4


def _group_size(nb_batch, nblocks, has_init):
    return SHORT_GROUP if has_init and nblocks == 1 and nb_batch % SHORT_GROUP == 0 else 1


def _attn_kernel(*refs, layer, tq, has_init, group):
    if group == 1:
        blocks = [_attn_block(*refs, layer=layer, tq=tq, has_init=has_init)]
    else:
        sinks_ref, q_ref, kvc_ref, kinit_ref, vinit_ref, gq_ref, gk_ref, gm512_ref, gm256_ref, sd_ref, o_ref, kn_ref = refs
        blocks = []
        for bi in range(group):
            rows = pl.ds(bi * tq, tq)
            blocks.append(_attn_block(
                sinks_ref, q_ref.at[:, rows, :], kvc_ref.at[rows, :], kinit_ref.at[bi], vinit_ref.at[bi],
                gq_ref, gk_ref, gm512_ref, gm256_ref, sd_ref, o_ref.at[rows, :], kn_ref.at[rows, :],
                layer=layer, tq=tq, has_init=True))
    while blocks:
        blocks = [b for b in blocks if next(b, _DONE) is not _DONE]


_DONE = object()


def _attn_block(*refs, layer, tq, has_init):
    if has_init:
        sinks_ref, q_ref, kvc_ref, kinit_ref, vinit_ref, gq_ref, gk_ref, gm512_ref, gm256_ref, sd_ref, o_ref, kn_ref = refs
    else:
        sinks_ref, q_ref, kvc_ref, kvp_ref, gq_ref, gk_ref, gm512_ref, gm256_ref, sd_ref, o_ref, kn_ref = refs
    n = pl.program_id(1)

    def qk_norm(x, gmat, g):
        x2 = x * x
        hi = x2.astype(BF16)
        lo = (x2 - hi.astype(F32)).astype(BF16)
        ss = _dot(hi, gmat) + _dot(lo, gmat)
        return x * lax.rsqrt(ss * (1.0 / HEAD_DIM) + EPS) * g

    gm256, gk = gm256_ref[...], gk_ref[...]
    kvc = kvc_ref[...]
    kc = qk_norm(kvc[:, :KV_W], gm256, gk)
    kn_ref[...] = kc
    vc = kvc[:, KV_W:]
    if has_init:
        kp, vp = kinit_ref[...], vinit_ref[...]
    else:
        kvp = kvp_ref[...]
        kp, vp = qk_norm(kvp[:, :KV_W], gm256, gk), kvp[:, KV_W:]
    if tq < WINDOW:
        pad = jnp.zeros((WINDOW - tq, KV_W), F32)
        kc = jnp.concatenate([kc, pad], axis=0)
        vc = jnp.concatenate([vc, pad], axis=0)
    kall = jnp.concatenate([kp, kc], axis=0)
    vall = jnp.concatenate([vp, vc], axis=0)

    nk = 2 * WINDOW
    ii = lax.broadcasted_iota(jnp.int32, (tq, nk), 0)
    jj = lax.broadcasted_iota(jnp.int32, (tq, nk), 1)
    dist = WINDOW + ii - jj
    valid = (dist >= 0) & (dist <= WINDOW)
    if not has_init:
        valid = valid & (jj >= jnp.where(n > 0, 0, WINDOW))
    distm = jnp.where(valid, dist.astype(F32), -NEG_BIG)

    gm512, gq = gm512_ref[...], gq_ref[...] * (HEAD_DIM ** -0.5)
    qn = [qk_norm(q_ref[s], gm512, gq) for s in range(2)]
    lane_k = lax.broadcasted_iota(jnp.int32, (nk, LANES), 1) < HEAD_DIM
    lane_q = lax.broadcasted_iota(jnp.int32, (tq, LANES), 1) < HEAD_DIM
    sd = sd_ref[...]

    def two_copies(a, upper):
        if upper:
            bot = jnp.where(lane_k, 0.0, a)
            top = pltpu.roll(bot, HEAD_DIM, 1)
        else:
            top = jnp.where(lane_k, a, 0.0)
            bot = pltpu.roll(top, HEAD_DIM, 1)
        return jnp.concatenate([top, bot], axis=0).astype(BF16)

    kds, vds = [], []
    for kv in range(N_KV):
        cs = slice(LANES * (kv // 2), LANES * (kv // 2) + LANES)
        kds.append(two_copies(kall[:, cs], bool(kv % 2)))
        vds.append(two_copies(vall[:, cs], bool(kv % 2)))
    npair = N_HEADS // 2
    yield
    scores = []
    for p in range(npair):
        off = LANES * (p % 4)
        qp = qn[p // 4][:, off:off + LANES].astype(BF16)
        scores.append(_nt_dot(qp, kds[p // 2]))
    yield
    probs, sinkw = [], []
    for p in range(npair):
        es, sk = [], []
        for hh in range(2):
            h = 2 * p + hh
            slope = 2.0 ** (-8.0 * (h + 1) / N_HEADS)
            sink = sinks_ref[layer, h]
            sh = scores[p][:, nk * hh:nk * hh + nk] - slope * distm
            mx = jnp.maximum(jnp.max(sh, axis=-1, keepdims=True), sink)
            es.append(jnp.exp(sh - mx))
            sk.append(jnp.exp(sink - mx))
        probs.append(jnp.concatenate(es, axis=1).astype(BF16))
        sinkw.append(jnp.where(lane_q, sk[0], sk[1]))
    yield
    for p in range(npair):
        num = _dot(probs[p], vds[p // 2])
        den = _dot(probs[p], sd) + sinkw[p]
        o_ref[:, LANES * p:LANES * p + LANES] = (num / den).astype(o_ref.dtype)


def _attn_call(z3, sinks, gq_t, gk_t, layer, nb_batch, t_len, cache=None):
    m = z3.shape[1]
    has_init = cache is not None
    tq = WINDOW if t_len % WINDOW == 0 else t_len
    nb = t_len // tq
    assert nb == 1 or not has_init
    depth = gq_t.shape[0]
    gm512 = _group_sum_matrix(SLAB, HEAD_DIM)
    gm256 = _group_sum_matrix(KV_W, HEAD_DIM)
    sd_np = np.zeros((4 * WINDOW, LANES), np.float32)
    sd_np[:2 * WINDOW, :HEAD_DIM] = 1.0
    sd_np[2 * WINDOW:, HEAD_DIM:] = 1.0
    sd = jnp.asarray(sd_np, dtype=BF16)

    group = _group_size(nb_batch, nb, has_init)
    rows = group * tq
    in_specs = [
        pl.BlockSpec(memory_space=pltpu.SMEM),
        pl.BlockSpec((2, rows, SLAB), lambda b, n: (0, b * nb + n, 0)),
        pl.BlockSpec((None, rows, SLAB), lambda b, n: (SLAB_KV, b * nb + n, 0)),
    ]
    args = [sinks, z3, z3]
    if has_init:
        cache_spec = pl.BlockSpec((None, group, WINDOW, KV_W), lambda b, n: (layer, b, 0, 0))
        if group == 1:
            cache_spec = pl.BlockSpec((None, None, WINDOW, KV_W), lambda b, n: (layer, b, 0, 0))
        in_specs += [cache_spec] * 2
        args += [cache[0], cache[1]]
    else:
        in_specs += [pl.BlockSpec((None, WINDOW, SLAB), lambda b, n: (SLAB_KV, jnp.maximum(b * nb + n - 1, 0), 0))]
        args += [z3]
    in_specs += [
        pl.BlockSpec((None, 1, SLAB), lambda b, n: (layer, 0, 0)),
        pl.BlockSpec((None, 1, KV_W), lambda b, n: (layer, 0, 0)),
        pl.BlockSpec((SLAB, SLAB), lambda b, n: (0, 0)),
        pl.BlockSpec((KV_W, KV_W), lambda b, n: (0, 0)),
        pl.BlockSpec((4 * WINDOW, LANES), lambda b, n: (0, 0)),
    ]
    args += [gq_t, gk_t, gm512, gm256, sd]
    out_dtype = F32 if has_init else BF16
    return pl.pallas_call(
        functools.partial(_attn_kernel, layer=layer, tq=tq, has_init=has_init, group=group),
        grid=(nb_batch // group, nb),
        in_specs=in_specs,
        out_specs=[
            pl.BlockSpec((rows, BR_W), lambda b, n: (b * nb + n, 0)),
            pl.BlockSpec((rows, KV_W), lambda b, n: (b * nb + n, 0)),
        ],
        out_shape=[jax.ShapeDtypeStruct((m, BR_W), out_dtype), jax.ShapeDtypeStruct((m, KV_W), F32)],
        compiler_params=_params("arbitrary", "arbitrary"),
        name="attn",
    )(*args)


def _conv_kernel(*refs, tt, nt, has_init):
    if has_init:
        l0, l1, g0, g1, init_ref, w_ref, b_ref, gl_ref, bl_ref, o_ref, new_ref, ext, ybuf, shifted = refs
    else:
        l0, l1, g0, g1, w_ref, b_ref, gl_ref, bl_ref, o_ref, new_ref, ext, ybuf, shifted = refs
    t = pl.program_id(1)

    @pl.when(t == 0)
    def _():
        ext[0:CONV_HALO, :] = init_ref[...] if has_init else jnp.zeros((CONV_HALO, BR_W), F32)

    if nt > 1:
        @pl.when(t > 0)
        def _():
            ext[0:CONV_HALO, :] = ext[tt:tt + CONV_HALO, :]

    for cb, (lr, gr) in enumerate(((l0, g0), (l1, g1))):
        ext[CONV_HALO:CONV_HALO + tt, SLAB * cb:SLAB * cb + SLAB] = lr[...] * jax.nn.sigmoid(gr[...])
    new_ref[...] = ext[tt:tt + CONV_HALO, :]

    rs = min(tt, 32)
    base = CONV_HALO - (CONV_K - 1)
    nsh = shifted.shape[1]
    for c in range(BR_W // LANES):
        cs = slice(LANES * c, LANES * c + LANES)
        for r in range(1, SUBLANES):
            shifted[r - 1] = ext[r:r + nsh, cs]

        def taps(i, carry, cs=cs):
            r0 = pl.multiple_of(i * rs, rs)
            acc = jnp.broadcast_to(b_ref[:, cs], (rs, LANES))
            for r in range(SUBLANES):
                offs = [(base + k) // SUBLANES for k in range(CONV_K) if (base + k) % SUBLANES == r]
                rows = pl.ds(r0 + SUBLANES * offs[0], rs + SUBLANES * (offs[-1] - offs[0]))
                win = ext[rows, cs] if r == 0 else shifted[r - 1, rows, :]
                for a in offs:
                    k = SUBLANES * a + r - base
                    d = SUBLANES * (a - offs[0])
                    acc = acc + w_ref[k:k + 1, cs] * win[d:d + rs]
            ybuf[pl.ds(r0, rs), cs] = acc
            return carry

        lax.fori_loop(0, tt // rs, taps, 0)

    def norm_act(i, carry):
        rows = pl.ds(pl.multiple_of(i * rs, rs), rs)
        y = ybuf[rows, :]
        yc = y - jnp.mean(y, axis=-1, keepdims=True)
        var = jnp.mean(yc * yc, axis=-1, keepdims=True)
        yn = yc * lax.rsqrt(var + EPS) * gl_ref[...] + bl_ref[...]
        o_ref[rows, :] = _silu(yn).astype(o_ref.dtype)
        return carry

    lax.fori_loop(0, tt // rs, norm_act, 0, unroll=min(4, tt // rs))


def _conv_short_kernel(l0, l1, g0, g1, init_ref, w_ref, b_ref, gl_ref, bl_ref, o_ref, new_ref, ext, ybuf, *,
                       nbat, t_len):
    ext[:, 0:SUBLANES, :] = jnp.zeros((nbat, SUBLANES, BR_W), F32)
    ext[:, CONV_HALO - (CONV_K - 1):CONV_HALO, :] = init_ref[...]
    for cb, (lr, gr) in enumerate(((l0, g0), (l1, g1))):
        u = lr[...] * jax.nn.sigmoid(gr[...])
        ext[:, CONV_HALO:CONV_HALO + t_len, SLAB * cb:SLAB * cb + SLAB] = u.reshape(nbat, t_len, SLAB)
    new_ref[...] = ext[:, t_len:t_len + CONV_HALO, :]
    base = CONV_HALO - (CONV_K - 1)
    for c in range(BR_W // LANES):
        cs = slice(LANES * c, LANES * c + LANES)
        acc = jnp.broadcast_to(b_ref[:, cs], (nbat, t_len, LANES))
        for k in range(CONV_K):
            acc = acc + w_ref[k:k + 1, cs] * ext[:, base + k:base + k + t_len, cs]
        ybuf[:, cs] = acc.reshape(nbat * t_len, LANES)
    rs = 32

    def norm_act(i, carry):
        rows = pl.ds(pl.multiple_of(i * rs, rs), rs)
        y = ybuf[rows, :]
        yc = y - jnp.mean(y, axis=-1, keepdims=True)
        var = jnp.mean(yc * yc, axis=-1, keepdims=True)
        yn = yc * lax.rsqrt(var + EPS) * gl_ref[...] + bl_ref[...]
        o_ref[rows, :] = _silu(yn).astype(o_ref.dtype)
        return carry

    lax.fori_loop(0, nbat * t_len // rs, norm_act, 0, unroll=min(4, nbat * t_len // rs))


def _short_batched(nb_batch, t_len, has_init):
    return has_init and t_len == SUBLANES and (nb_batch * t_len) % 32 == 0


def _conv_short_call(z3, init, w_dw, b_dw, g_ln, b_ln, layer, nb_batch, t_len):
    m = z3.shape[1]
    depth = w_dw.shape[0]
    slab = lambda j: pl.BlockSpec((None, m, SLAB), lambda i: (j, 0, 0))
    vec = lambda: pl.BlockSpec((None, 1, BR_W), lambda i: (layer, 0, 0))
    return pl.pallas_call(
        functools.partial(_conv_short_kernel, nbat=nb_batch, t_len=t_len),
        grid=(1,),
        in_specs=[slab(SLAB_CONV), slab(SLAB_CONV + 1), slab(SLAB_CONV + 2), slab(SLAB_CONV + 3),
                  pl.BlockSpec((None, nb_batch, CONV_K - 1, BR_W), lambda i: (layer, 0, 0, 0)),
                  pl.BlockSpec((None, CONV_K, BR_W), lambda i: (layer, 0, 0)), vec(), vec(), vec()],
        out_specs=[pl.BlockSpec((m, BR_W), lambda i: (0, 0)),
                   pl.BlockSpec((nb_batch, CONV_HALO, BR_W), lambda i: (0, 0, 0))],
        out_shape=[jax.ShapeDtypeStruct((m, BR_W), F32), jax.ShapeDtypeStruct((nb_batch, CONV_HALO, BR_W), F32)],
        scratch_shapes=[pltpu.VMEM((nb_batch, CONV_HALO + t_len, BR_W), F32), pltpu.VMEM((m, BR_W), F32)],
        compiler_params=_params("arbitrary"),
        name="conv_short",
    )(z3, z3, z3, z3, init, w_dw, b_dw.reshape(depth, 1, BR_W), g_ln.reshape(depth, 1, BR_W),
      b_ln.reshape(depth, 1, BR_W))


def _conv_call(z3, w_dw, b_dw, g_ln, b_ln, layer, nb_batch, t_len, init=None):
    m = z3.shape[1]
    has_init = init is not None
    if _short_batched(nb_batch, t_len, has_init):
        return _conv_short_call(z3, init, w_dw, b_dw, g_ln, b_ln, layer, nb_batch, t_len)
    if has_init:
        init = jnp.pad(init, ((0, 0), (0, 0), (CONV_HALO - (CONV_K - 1), 0), (0, 0)))
    tt = 512 if t_len % 512 == 0 else (256 if t_len % 256 == 0 else t_len)
    nt = t_len // tt
    assert nt == 1 or tt >= CONV_HALO
    depth = w_dw.shape[0]

    def slab(j):
        return pl.BlockSpec((None, tt, SLAB), lambda b, t: (j, b * nt + t, 0))

    def vec():
        return pl.BlockSpec((None, 1, BR_W), lambda b, t: (layer, 0, 0))

    in_specs = [slab(SLAB_CONV), slab(SLAB_CONV + 1), slab(SLAB_CONV + 2), slab(SLAB_CONV + 3)]
    args = [z3, z3, z3, z3]
    if has_init:
        in_specs.append(pl.BlockSpec((None, None, CONV_HALO, BR_W), lambda b, t: (layer, b, 0, 0)))
        args.append(init)
    in_specs += [pl.BlockSpec((None, CONV_K, BR_W), lambda b, t: (layer, 0, 0)), vec(), vec(), vec()]
    args += [w_dw, b_dw.reshape(depth, 1, BR_W), g_ln.reshape(depth, 1, BR_W), b_ln.reshape(depth, 1, BR_W)]
    return pl.pallas_call(
        functools.partial(_conv_kernel, tt=tt, nt=nt, has_init=has_init),
        grid=(nb_batch, nt),
        in_specs=in_specs,
        out_specs=[
            pl.BlockSpec((tt, BR_W), lambda b, t: (b * nt + t, 0)),
            pl.BlockSpec((None, CONV_HALO, BR_W), lambda b, t: (b, 0, 0)),
        ],
        out_shape=[
            jax.ShapeDtypeStruct((m, BR_W), F32 if has_init else BF16),
            jax.ShapeDtypeStruct((nb_batch, CONV_HALO, BR_W), F32),
        ],
        scratch_shapes=[pltpu.VMEM((CONV_HALO + tt, BR_W), F32), pltpu.VMEM((tt, BR_W), F32),
                        pltpu.VMEM((SUBLANES - 1, CONV_HALO + tt - SUBLANES, LANES), F32)],
        compiler_params=_params("arbitrary", "arbitrary"),
        name="conv",
    )(*args)


def _pool_kernel(*refs, tt, nt, has_init, pos0):
    if has_init:
        u0, u1, init_ref, w_ref, s_ref, o_ref, ext = refs
    else:
        u0, u1, w_ref, s_ref, o_ref, ext = refs
    t = pl.program_id(1)

    @pl.when(t == 0)
    def _():
        ext[0:POOL_HALO, :] = init_ref[...] if has_init else jnp.zeros((POOL_HALO, BR_W), F32)

    if nt > 1:
        @pl.when(t > 0)
        def _():
            ext[0:POOL_HALO, :] = ext[tt:tt + POOL_HALO, :]

    ext[POOL_HALO:POOL_HALO + tt, 0:SLAB] = u0[...]
    ext[POOL_HALO:POOL_HALO + tt, SLAB:2 * SLAB] = u1[...]
    pos = pos0 + t * tt + lax.broadcasted_iota(jnp.int32, (tt, 1), 0)
    for g, w in enumerate(POOL_WINDOWS):
        cs = slice(POOL_G * g, POOL_G * g + POOL_G)
        cur = ext[POOL_HALO:POOL_HALO + tt, cs]
        wsum = cur
        for s in range(1, w):
            wsum = wsum + ext[POOL_HALO - s:POOL_HALO - s + tt, cs]
        cnt = jnp.minimum(pos + 1, w).astype(F32)
        zg = wsum / cnt - cur
        y = _dot(zg.astype(BF16), w_ref[g].astype(BF16)) * s_ref[:, cs]
        o_ref[:, cs] = y.astype(o_ref.dtype)


def _pool_short_kernel(u0, u1, init_ref, w_ref, s_ref, o_ref, ext, *, nbat, t_len, pos0):
    ext[:, 0:SUBLANES, :] = jnp.zeros((nbat, SUBLANES, BR_W), F32)
    ext[:, POOL_HALO - POOL_PAD:POOL_HALO, :] = init_ref[...]
    ext[:, POOL_HALO:POOL_HALO + t_len, 0:SLAB] = u0[...].reshape(nbat, t_len, SLAB)
    ext[:, POOL_HALO:POOL_HALO + t_len, SLAB:2 * SLAB] = u1[...].reshape(nbat, t_len, SLAB)
    pos = pos0 + lax.broadcasted_iota(jnp.int32, (nbat, t_len, POOL_G), 1)
    for g, w in enumerate(POOL_WINDOWS):
        cs = slice(POOL_G * g, POOL_G * g + POOL_G)
        cur = ext[:, POOL_HALO:POOL_HALO + t_len, cs]
        wsum = cur
        for s in range(1, w):
            wsum = wsum + ext[:, POOL_HALO - s:POOL_HALO - s + t_len, cs]
        cnt = jnp.minimum(pos + 1, w).astype(F32)
        zg = (wsum / cnt - cur).reshape(nbat * t_len, POOL_G)
        y = _dot(zg.astype(BF16), w_ref[g].astype(BF16)) * s_ref[:, cs]
        o_ref[:, cs] = y.astype(o_ref.dtype)


def _pool_short_call(z3, init, w_pool, s_pool, layer, nb_batch, t_len, pos0):
    m = z3.shape[1]
    depth = w_pool.shape[0]
    slab = lambda j: pl.BlockSpec((None, m, SLAB), lambda i: (j, 0, 0))
    return pl.pallas_call(
        functools.partial(_pool_short_kernel, nbat=nb_batch, t_len=t_len, pos0=pos0),
        grid=(1,),
        in_specs=[slab(SLAB_POOL), slab(SLAB_POOL + 1),
                  pl.BlockSpec((None, nb_batch, POOL_PAD, BR_W), lambda i: (layer, 0, 0, 0)),
                  pl.BlockSpec((None, len(POOL_WINDOWS), POOL_G, POOL_G), lambda i: (layer, 0, 0, 0)),
                  pl.BlockSpec((None, 1, BR_W), lambda i: (layer, 0, 0))],
        out_specs=pl.BlockSpec((m, BR_W), lambda i: (0, 0)),
        out_shape=jax.ShapeDtypeStruct((m, BR_W), F32),
        scratch_shapes=[pltpu.VMEM((nb_batch, POOL_HALO + t_len, BR_W), F32)],
        compiler_params=_params("arbitrary"),
        name="pool_short",
    )(z3, z3, init, w_pool, s_pool.reshape(depth, 1, BR_W))


def _pool_call(z3, w_pool, s_pool, layer, nb_batch, t_len, pos0, init=None):
    m = z3.shape[1]
    has_init = init is not None
    if _short_batched(nb_batch, t_len, has_init):
        return _pool_short_call(z3, init, w_pool, s_pool, layer, nb_batch, t_len, pos0)
    if has_init:
        init = jnp.pad(init, ((0, 0), (0, 0), (POOL_HALO - POOL_PAD, 0), (0, 0)))
    tt = 256 if t_len % 256 == 0 else t_len
    nt = t_len // tt
    assert nt == 1 or tt >= POOL_HALO
    depth = w_pool.shape[0]

    def slab(j):
        return pl.BlockSpec((None, tt, SLAB), lambda b, t: (j, b * nt + t, 0))

    in_specs = [slab(SLAB_POOL), slab(SLAB_POOL + 1)]
    args = [z3, z3]
    if has_init:
        in_specs.append(pl.BlockSpec((None, None, POOL_HALO, BR_W), lambda b, t: (layer, b, 0, 0)))
        args.append(init)
    in_specs += [
        pl.BlockSpec((None, len(POOL_WINDOWS), POOL_G, POOL_G), lambda b, t: (layer, 0, 0, 0)),
        pl.BlockSpec((None, 1, BR_W), lambda b, t: (layer, 0, 0)),
    ]
    args += [w_pool, s_pool.reshape(depth, 1, BR_W)]
    return pl.pallas_call(
        functools.partial(_pool_kernel, tt=tt, nt=nt, has_init=has_init, pos0=pos0),
        grid=(nb_batch, nt),
        in_specs=in_specs,
        out_specs=pl.BlockSpec((tt, BR_W), lambda b, t: (b * nt + t, 0)),
        out_shape=jax.ShapeDtypeStruct((m, BR_W), F32 if has_init else BF16),
        scratch_shapes=[pltpu.VMEM((POOL_HALO + tt, BR_W), F32)],
        compiler_params=_params("arbitrary", "arbitrary"),
        name="pool",
    )(*args)


def _ret_constants(c, cp):
    lg = np.log1p(-np.exp2(-5.0 - np.arange(RET_HEADS, dtype=np.float64)))
    i = np.arange(c, dtype=np.float64)
    diff = i[:, None] - i[None, :]
    decay = np.where(diff >= 0, np.exp(lg[:, None, None] * np.maximum(diff, 0.0)), 0.0)
    dec = np.zeros((RET_HEADS // 2, c, 2 * cp))
    for h in range(RET_HEADS):
        dec[h // 2, :, (h % 2) * cp:(h % 2) * cp + c] = decay[h]
    kfac = np.repeat(np.exp(lg[None, :] * (c - 1 - i)[:, None]), RET_DK, axis=1) * RET_DK ** -0.5
    cfac = np.repeat(np.exp(lg[None, :] * (i + 1)[:, None]), RET_DV, axis=1)
    gch = np.repeat(np.exp(lg * c), RET_DK).reshape(RET_HEADS // 2, 2 * RET_DK, 1)
    gch = np.broadcast_to(gch, (RET_HEADS // 2, 2 * RET_DK, RET_DV))
    f = lambda a: jnp.asarray(np.ascontiguousarray(a), dtype=F32)
    return f(dec), f(kfac), f(cfac), f(gch)


def _ret_kernel(*refs, c, cp, has_init, group, cps):
    if group == 1 and cps == 1:
        return _ret_chunk(*refs, c=c, cp=cp, has_init=has_init)
    if group == 1:
        ins, rest = refs[:6], refs[6:-2]
        o_ref, s_ref = refs[-2:]
        for ci in range(cps):
            rows = pl.ds(ci * c, c)
            _ret_chunk(*[r.at[rows, :] for r in ins], *rest, o_ref.at[rows, :], s_ref,
                       c=c, cp=cp, has_init=has_init, first=ci == 0)
        return
    rq, rk, rv0, rv1, rg0, rg1, s0_ref, gn_ref, dec_ref, kf_ref, cf_ref, gch_ref, o_ref, s_ref = refs
    for bi in range(group):
        rows = pl.ds(bi * c, c)
        _ret_chunk(rq.at[rows, :], rk.at[rows, :], rv0.at[rows, :], rv1.at[rows, :], rg0.at[rows, :], rg1.at[rows, :],
                   s0_ref.at[bi], gn_ref, dec_ref, kf_ref, cf_ref, gch_ref, o_ref.at[rows, :], s_ref.at[bi],
                   c=c, cp=cp, has_init=True)


def _ret_chunk(*refs, c, cp, has_init, first=True):
    if has_init:
        rq, rk, rv0, rv1, rg0, rg1, s0_ref, gn_ref, dec_ref, kf_ref, cf_ref, gch_ref, o_ref, s_ref = refs
    else:
        rq, rk, rv0, rv1, rg0, rg1, gn_ref, dec_ref, kf_ref, cf_ref, gch_ref, o_ref, s_ref = refs
    n = pl.program_id(1)

    if first:
        @pl.when(n == 0)
        def _():
            s_ref[...] = s0_ref[...] if has_init else jnp.zeros(s_ref.shape, F32)

    lo = lax.broadcasted_iota(jnp.int32, (c, LANES), 1) < RET_DK
    q = rq[...]
    kraw = rk[...]
    k = kraw * (RET_DK ** -0.5)
    kdec = kraw * kf_ref[...]
    rvs, rgs = (rv0, rv1), (rg0, rg1)

    def stack_heads(a):
        a0, a1 = jnp.where(lo, a, 0.0), jnp.where(lo, 0.0, a)
        if cp > c:
            z = jnp.zeros((cp - c, LANES), F32)
            return jnp.concatenate([a0, z, a1, z], axis=0)
        return jnp.concatenate([a0, a1], axis=0)

    npair = RET_HEADS // 2
    zc = jnp.zeros((c, LANES), F32)
    scores, vbds, vsts, inners, crosses = [], [], [], [], []
    for p in range(npair):
        cs = slice(LANES * p, LANES * p + LANES)
        scores.append(_nt_dot(q[:, cs].astype(BF16), stack_heads(k[:, cs]).astype(BF16)) * dec_ref[p])
        vs = [rvs[h // 4][:, LANES * (h % 4):LANES * (h % 4) + LANES] for h in (2 * p, 2 * p + 1)]
        rows0 = jnp.concatenate([vs[0], zc], axis=1)
        rows1 = jnp.concatenate([zc, vs[1]], axis=1)
        if cp > c:
            zp = jnp.zeros((cp - c, 2 * LANES), F32)
            zq = jnp.zeros((cp - c, LANES), F32)
            vbds.append(jnp.concatenate([rows0, zp, rows1, zp], axis=0).astype(BF16))
            vsts.append(jnp.concatenate([vs[0], zq, vs[1], zq], axis=0).astype(BF16))
        else:
            vbds.append(jnp.concatenate([rows0, rows1], axis=0).astype(BF16))
            vsts.append(jnp.concatenate([vs[0], vs[1]], axis=0).astype(BF16))
    for p in range(npair):
        cs = slice(LANES * p, LANES * p + LANES)
        qp = q[:, cs]
        inners.append(_dot(scores[p].astype(BF16), vbds[p]))
        sprev = s_ref[p]
        qst = jnp.concatenate([jnp.where(lo, qp, 0.0), jnp.where(lo, 0.0, qp)], axis=0)
        crosses.append(_dot(qst.astype(BF16), sprev.astype(BF16)))
        upd = _tn_dot(stack_heads(kdec[:, cs]).astype(BF16), vsts[p])
        s_ref[p] = gch_ref[p] * sprev + upd
    for p in range(npair):
        inner, cross = inners[p], crosses[p]
        for hh, h in enumerate((2 * p, 2 * p + 1)):
            hc = slice(LANES * h, LANES * h + LANES)
            o = inner[:, LANES * hh:LANES * hh + LANES] + cross[c * hh:c * hh + c, :] * cf_ref[:, hc]
            oc = o - jnp.mean(o, axis=-1, keepdims=True)
            var = jnp.mean(oc * oc, axis=-1, keepdims=True)
            gate = rgs[h // 4][:, LANES * (h % 4):LANES * (h % 4) + LANES]
            y = oc * lax.rsqrt(var + EPS) * gn_ref[:, hc] * _silu(gate)
            o_ref[:, hc] = y.astype(o_ref.dtype)


def _ret_call(z3, g_ret, layer, nb_batch, t_len, init=None):
    m = z3.shape[1]
    has_init = init is not None
    c = RET_CHUNK if t_len % RET_CHUNK == 0 else t_len
    cp = max(c, 64)
    nc = t_len // c
    depth = g_ret.shape[0]
    dec, kfac, cfac, gch = _ret_constants(c, cp)
    npair = RET_HEADS // 2

    group = _group_size(nb_batch, nc, has_init)
    cps = RET_CHUNKS_PER_STEP if group == 1 and nc % RET_CHUNKS_PER_STEP == 0 else 1
    ns = nc // cps
    rows = group * cps * c
    state_block = (npair, LANES, RET_DV) if group == 1 else (group, npair, LANES, RET_DV)
    lead = (None,) if group == 1 else ()

    def slab(j):
        return pl.BlockSpec((None, rows, SLAB), lambda b, n: (j, b * ns + n, 0))

    def const(shape):
        nd = len(shape)
        return pl.BlockSpec(shape, lambda b, n: (0,) * nd)

    in_specs = [slab(SLAB_RQ), slab(SLAB_RK), slab(SLAB_RV), slab(SLAB_RV + 1), slab(SLAB_RG), slab(SLAB_RG + 1)]
    args = [z3] * 6
    if has_init:
        in_specs.append(pl.BlockSpec((None,) + lead + state_block, lambda b, n: (layer, b, 0, 0, 0)))
        args.append(init)
    in_specs += [pl.BlockSpec((None, 1, BR_W), lambda b, n: (layer, 0, 0)),
                 const(dec.shape), const(kfac.shape), const(cfac.shape), const(gch.shape)]
    args += [g_ret.reshape(depth, 1, BR_W), dec, kfac, cfac, gch]
    return pl.pallas_call(
        functools.partial(_ret_kernel, c=c, cp=cp, has_init=has_init, group=group, cps=cps),
        grid=(nb_batch // group, ns),
        in_specs=in_specs,
        out_specs=[
            pl.BlockSpec((rows, BR_W), lambda b, n: (b * ns + n, 0)),
            pl.BlockSpec(lead + state_block, lambda b, n: (b, 0, 0, 0)),
        ],
        out_shape=[
            jax.ShapeDtypeStruct((m, BR_W), F32 if has_init else BF16),
            jax.ShapeDtypeStruct((nb_batch, npair, LANES, RET_DV), F32),
        ],
        compiler_params=_params("arbitrary", "arbitrary"),
        name="retention",
    )(*args)


def _merge_kernel(y0, y1, y2, y3, gate_ref, w_ref, o_ref, *rest):
    wb_ref, acc_ref = rest if len(rest) == 2 else (None, rest[0])
    r, c = pl.program_id(1), pl.program_id(2)
    if wb_ref is not None:
        wb_ref[...] = w_ref[...].astype(BF16)
        w_ref = wb_ref
    for k, y_ref in enumerate((y0, y1, y2, y3)):
        @pl.when(r == k)
        def _(k=k, y_ref=y_ref):
            val = gate_ref[...].astype(F32) * _dot(y_ref[...].astype(BF16), w_ref[...])
            if k == 0:
                acc_ref[c] = val
            elif k < N_BR - 1:
                acc_ref[c] += val
            else:
                o_ref[...] = (acc_ref[c] + val).astype(o_ref.dtype)


def _merge_call(ys, gates, w, tm):
    m = gates.shape[1]
    _single_tile(m, tm, w)
    tn = MERGE_TN
    nc = D_MODEL // tn
    y_spec = pl.BlockSpec((tm, BR_W), lambda i, r, c: (i, 0))
    wblock, windex = (None, BR_W, tn), lambda i, r, c: (r, 0, c)
    out_specs = [pl.BlockSpec((tm, tn), lambda i, r, c: (i, jnp.where(r == N_BR - 1, c, 0)))]
    out_shape = [jax.ShapeDtypeStruct((m, D_MODEL), BF16)]
    if w.master:
        spec, shape = w.emit(wblock, windex)
        out_specs.append(spec)
        out_shape.append(shape)
    return pl.pallas_call(
        _merge_kernel,
        grid=(m // tm, N_BR, nc),
        in_specs=[y_spec] * N_BR + [
            pl.BlockSpec((None, tm, tn), lambda i, r, c: (nc * r + c, i, 0)),
            w.spec(wblock, windex),
        ],
        out_specs=out_specs,
        out_shape=out_shape,
        scratch_shapes=[pltpu.VMEM((nc, tm, tn), F32)],
        compiler_params=_params("arbitrary", "arbitrary", "arbitrary"),
        name="merge",
    )(*ys, gates, w.arr)


def _outproj_kernel(m_ref, w_ref, x_ref, gt_ref, o_ref, wb_ref=None):
    o_ref[...] = x_ref[...] + gt_ref[...] * _dot(m_ref[...], _load_weight(w_ref, wb_ref))


def _outproj_call(merged, x, mod, w, tm):
    m, d = x.shape
    _single_tile(m, tm, w)
    wblock, windex = (d, SLAB), lambda i, c: (0, c)
    out_specs = [pl.BlockSpec((tm, SLAB), lambda i, c: (i, c))]
    out_shape = [jax.ShapeDtypeStruct((m, d), F32)]
    if w.master:
        spec, shape = w.emit(wblock, windex)
        out_specs.append(spec)
        out_shape.append(shape)
    return pl.pallas_call(
        _outproj_kernel,
        grid=(m // tm, d // SLAB),
        in_specs=[
            pl.BlockSpec((tm, d), lambda i, c: (i, 0)),
            w.spec(wblock, windex),
            pl.BlockSpec((tm, SLAB), lambda i, c: (i, c)),
            mod.spec(2, tm, width=SLAB, col=1),
        ],
        out_specs=out_specs,
        out_shape=out_shape,
        compiler_params=_params("arbitrary", "arbitrary"),
        name="outproj",
    )(merged, w.arr, x, mod.arr)


def _mlp_kernel(x_ref, sc_ref, sh_ref, gt_ref, g_ref, w1_ref, w2_ref, o_ref, *rest, rc, nf):
    (w1b_ref, w2b_ref, h_ref) = rest if len(rest) == 3 else (None, None, rest[0])
    f = pl.program_id(1)

    @pl.when(f == 0)
    def _():
        _modnorm_to(h_ref, x_ref, sc_ref, sh_ref, g_ref, rc)
        o_ref[...] = jnp.zeros(o_ref.shape, F32)

    a = _dot(h_ref[...], _load_weight(w1_ref, w1b_ref))
    a = jnp.square(jnp.maximum(a, 0.0)).astype(BF16)
    if w2b_ref is not None:
        w2b_ref[...] = w2_ref[...].astype(BF16)
        w2_ref = w2b_ref
    for c0 in range(0, o_ref.shape[1], SLAB):
        o_ref[:, c0:c0 + SLAB] += _dot(a, w2_ref[:, c0:c0 + SLAB])

    @pl.when(f == nf - 1)
    def _():
        def body(cidx, carry):
            r = pl.multiple_of(cidx * rc, rc)
            rows = pl.ds(r, rc)
            o_ref[rows, :] = x_ref[rows, :] + _mod_rows(gt_ref, r, rc) * o_ref[rows, :]
            return carry

        lax.fori_loop(0, x_ref.shape[0] // rc, body, 0)


def _mlp_call(x, mod, g_norm, w1, w2, layer, tm, tf):
    m, d = x.shape
    _single_tile(m, tm, w1)
    assert w1.master == w2.master
    depth = g_norm.shape[0]
    nf = w1.shape[1] // tf
    w1block, w1index = (d, tf), lambda i, f: (0, f)
    w2block, w2index = (tf, d), lambda i, f: (f, 0)
    out_specs = [pl.BlockSpec((tm, d), lambda i, f: (i, 0))]
    out_shape = [jax.ShapeDtypeStruct((m, d), F32)]
    if w1.master:
        for spec, shape in (w1.emit(w1block, w1index), w2.emit(w2block, w2index)):
            out_specs.append(spec)
            out_shape.append(shape)
    return pl.pallas_call(
        functools.partial(_mlp_kernel, rc=_row_chunk(tm), nf=nf),
        grid=(m // tm, nf),
        in_specs=[
            pl.BlockSpec((tm, d), lambda i, f: (i, 0)),
            mod.spec(4, tm), mod.spec(3, tm), mod.spec(5, tm),
            pl.BlockSpec((None, 1, d), lambda i, f: (layer, 0, 0)),
            w1.spec(w1block, w1index),
            w2.spec(w2block, w2index),
        ],
        out_specs=out_specs,
        out_shape=out_shape,
        scratch_shapes=[pltpu.VMEM((tm, d), BF16)],
        compiler_params=_params("arbitrary", "arbitrary"),
        name="mlp",
    )(x, mod.arr, mod.arr, mod.arr, g_norm.reshape(depth, 1, d), w1.arr, w2.arr)


def _layer(x, mod, p, w, layer, nb_batch, t_len, tm_in, tm, tf, cache):
    copies = {}

    def split(outs, *names):
        outs = list(outs)
        for name in reversed(names):
            if w[name].master:
                copies[name] = outs.pop()
        return outs if len(outs) > 1 else outs[0]

    z3, h = split(_inproj_call(x, mod, p["g_norm1"], w["in_mix"], layer, tm_in), "in_mix")
    gates = split(_gate_call(h, w["in_gate"], tm_in), "in_gate")
    if cache is None:
        attn_cache = conv_init = pool_init = ret_init = None
        pos0 = 0
    else:
        attn_cache, conv_init, pool_init, ret_init = cache
        pos0 = PAST_LEN
    y_att, k_norm = _attn_call(z3, p["attn_sinks"], p["gq_t"], p["gk_t"], layer, nb_batch, t_len, attn_cache)
    y_conv, conv_tail = _conv_call(z3, p["w_dw"], p["b_dw"], p["g_conv_ln"], p["b_conv_ln"], layer, nb_batch, t_len,
                                   conv_init)
    y_pool = _pool_call(z3, p["w_pool"], p["s_pool"], layer, nb_batch, t_len, pos0, pool_init)
    y_ret, s_new = _ret_call(z3, p["g_ret_norm"], layer, nb_batch, t_len, ret_init)
    merged = split(_merge_call((y_att, y_conv, y_pool, y_ret), gates, w["br"], tm), "br")
    x = split(_outproj_call(merged, x, mod, w["out"], tm_in), "out")
    x = split(_mlp_call(x, mod, p["g_norm2"], w["mlp1"], w["mlp2"], layer, tm, tf), "mlp1", "mlp2")
    return x, (z3, k_norm, conv_tail, s_new), copies


def kernel(x_prompt, x_sample, c_prompt, c_sample, cache_attn_k, cache_attn_v, state_conv, state_pool, state_ret,
           w_ada, b_ada, g_norm1, g_norm2, w_in, g_qnorm, g_knorm, attn_sinks, w_dw, b_dw, g_conv_ln, b_conv_ln,
           w_pool, s_pool, g_ret_norm, w_br, w_out, w_mlp1, w_mlp2):
    nb, t_len, d = x_prompt.shape
    nsb, st_len, _ = x_sample.shape
    depth = w_ada.shape[0]
    mp, ms = nb * t_len, nsb * st_len

    n_c = nb + nsb
    r_pad = -(-n_c // 16) * 16
    c_all = jnp.concatenate([c_prompt, c_sample, jnp.zeros((r_pad - n_c, d), F32)], axis=0)
    ada = _ada_call(c_all, w_ada, b_ada)
    ada_p = ada[:, :nb].reshape(depth, nb, 1, 6 * d)
    ada_s = jnp.repeat(ada[:, nb:n_c], st_len, axis=1)

    p = dict(g_norm1=g_norm1, g_norm2=g_norm2, attn_sinks=attn_sinks, w_dw=w_dw, b_dw=b_dw,
             g_conv_ln=g_conv_ln, b_conv_ln=b_conv_ln, w_pool=w_pool, s_pool=s_pool, g_ret_norm=g_ret_norm,
             gq_t=jnp.tile(g_qnorm, (1, SLAB // HEAD_DIM)).reshape(depth, 1, SLAB),
             gk_t=jnp.tile(g_knorm, (1, N_KV)).reshape(depth, 1, KV_W))

    tm_p = 1024 if t_len % 1024 == 0 else t_len
    tm_p_in = 2048 if t_len % 2048 == 0 else tm_p
    tm_s = ms
    n_mix = SLAB_GATE * SLAB
    cache_k2 = cache_attn_k.reshape(depth, nsb, WINDOW, KV_W)
    cache_v2 = cache_attn_v.reshape(depth, nsb, WINDOW, KV_W)
    ret_init = state_ret.reshape(depth, nsb, RET_HEADS // 2, 2 * RET_DK, RET_DV)

    xp = x_prompt.reshape(mp, d)
    xs = x_sample.reshape(ms, d)
    st_p = [[] for _ in range(5)]
    st_s = [[] for _ in range(5)]
    for l in range(depth):
        masters = dict(in_mix=_Weight(w_in, l, shape=(d, n_mix)),
                       in_gate=_Weight(w_in, l, shape=(d, N_IN - n_mix), col0=SLAB_GATE),
                       br=_Weight(w_br, l), out=_Weight(w_out, l), mlp1=_Weight(w_mlp1, l), mlp2=_Weight(w_mlp2, l))
        mod_s = _Mod(ada_s, l, True, st_len)
        cache = ((cache_k2, cache_v2), state_conv, state_pool, ret_init)
        xs, (z3, k_norm, conv_tail, s_new), copies = _layer(xs, mod_s, p, masters, l, nsb, st_len, tm_s, tm_s,
                                                            MLP_TF_MASTER, cache)
        st_s[0].append(k_norm.reshape(nsb, st_len, N_KV, HEAD_DIM))
        st_s[1].append(z3[SLAB_KV].reshape(nsb, st_len, SLAB)[:, :, KV_W:].reshape(nsb, st_len, N_KV, HEAD_DIM))
        st_s[2].append(conv_tail)
        pool_u = jnp.moveaxis(z3[SLAB_POOL:SLAB_POOL + 2].reshape(2, nsb, st_len, SLAB), 0, 2)
        st_s[3].append(pool_u.reshape(nsb, st_len, BR_W))
        st_s[4].append(s_new.reshape(nsb, RET_HEADS, RET_DK, RET_DV))

        mod_p = _Mod(ada_p, l, False, t_len)
        wcopy = {name: _Weight(arr) for name, arr in copies.items()}
        xp, (z3, k_norm, conv_tail, s_new), _ = _layer(xp, mod_p, p, wcopy, l, nb, t_len, tm_p_in, tm_p, MLP_TF, None)
        z4 = z3.reshape(SLAB_GATE, nb, t_len, SLAB)
        st_p[0].append(k_norm.reshape(nb, t_len, KV_W)[:, -WINDOW:].reshape(nb, WINDOW, N_KV, HEAD_DIM))
        st_p[1].append(z4[SLAB_KV, :, -WINDOW:, KV_W:].reshape(nb, WINDOW, N_KV, HEAD_DIM))
        st_p[2].append(conv_tail[:, CONV_HALO - (CONV_K - 1):])
        pool_u = z4[SLAB_POOL:SLAB_POOL + 2, :, -POOL_PAD:]
        st_p[3].append(jnp.moveaxis(pool_u, 0, 2).reshape(nb, POOL_PAD, BR_W))
        st_p[4].append(s_new.reshape(nb, RET_HEADS, RET_DK, RET_DV))

    def rolled(old, new_rows, keep):
        return jnp.concatenate([old, jnp.stack(new_rows)], axis=2)[:, :, -keep:]

    sample_states = (rolled(cache_attn_k, st_s[0], WINDOW), rolled(cache_attn_v, st_s[1], WINDOW),
                     jnp.stack(st_s[2])[:, :, CONV_HALO - (CONV_K - 1):], rolled(state_pool, st_s[3], POOL_PAD),
                     jnp.stack(st_s[4]))
    return (xp.reshape(nb, t_len, d), xs.reshape(nsb, st_len, d), *[jnp.stack(a) for a in st_p], *sample_states)
```

```python
import functools

import numpy as np
import jax
import jax.numpy as jnp
from jax import lax
from jax.experimental import pallas as pl
from jax.experimental.pallas import tpu as pltpu

F32 = jnp.float32
BF16 = jnp.bfloat16

D_MODEL = 2048
PAST_LEN = 16384
N_HEADS = 16
HEAD_DIM = 64
N_KV = 4
WINDOW = 128
BR_W = 1024
CONV_K = 31
POOL_WINDOWS = (2, 4, 8, 16)
POOL_G = 256
POOL_PAD = 15
RET_HEADS = 8
RET_DK = 64
RET_DV = 128
RET_CHUNK = 128
N_BR = 4
D_FF = 4 * D_MODEL
MLP_TF = 1024
MLP_TF_MASTER = 512
MERGE_TN = 2048
GATE_TN = 1024
EPS = 1e-6
KV_W = N_KV * HEAD_DIM
N_IN = 15872

SLAB = 512
N_SLAB = N_IN // SLAB
SLAB_Q, SLAB_KV, SLAB_CONV, SLAB_POOL, SLAB_RQ, SLAB_RK, SLAB_RV, SLAB_RG, SLAB_GATE = 0, 2, 3, 7, 9, 10, 11, 13, 15

VMEM_LIMIT_BYTES = 60 * 1024 * 1024
LANES = 128
SUBLANES = 8
NEG_BIG = -1e30
CONV_HALO = 32
POOL_HALO = 16


def _params(*sem):
    return pltpu.CompilerParams(dimension_semantics=sem, vmem_limit_bytes=VMEM_LIMIT_BYTES)


def _nt_dot(a, b):
    return lax.dot_general(a, b, (((1,), (1,)), ((), ())), preferred_element_type=F32)


def _tn_dot(a, b):
    return lax.dot_general(a, b, (((0,), (0,)), ((), ())), preferred_element_type=F32)


def _dot(a, b):
    return jnp.dot(a, b, preferred_element_type=F32)


def _silu(x):
    return x * jax.nn.sigmoid(x)


def _ada_kernel(c_ref, w_ref, b_ref, o_ref):
    s = _silu(c_ref[...]).astype(BF16)
    o_ref[...] = _dot(s, w_ref[...].astype(BF16)) + b_ref[...]


def _ada_call(c_all, w_ada, b_ada):
    depth, d, n = w_ada.shape
    r = c_all.shape[0]
    tn = 1024
    return pl.pallas_call(
        _ada_kernel,
        grid=(depth, n // tn),
        in_specs=[
            pl.BlockSpec((r, d), lambda l, j: (0, 0)),
            pl.BlockSpec((None, d, tn), lambda l, j: (l, 0, j)),
            pl.BlockSpec((None, 1, tn), lambda l, j: (l, 0, j)),
        ],
        out_specs=pl.BlockSpec((None, r, tn), lambda l, j: (l, 0, j)),
        out_shape=jax.ShapeDtypeStruct((depth, r, n), F32),
        compiler_params=_params("arbitrary", "arbitrary"),
        name="ada",
    )(c_all, w_ada, b_ada.reshape(depth, 1, n))


class _Mod:
    def __init__(self, arr, layer, per_row, t_len):
        self.arr, self.l, self.per_row, self.t_len = arr, layer, per_row, t_len

    def spec(self, k, tm, width=D_MODEL, col=None):
        l, tpb, nb = self.l, max(self.t_len // tm, 1), D_MODEL // width

        def cidx(idx):
            return k * nb + (idx[col] if col is not None else 0)

        if self.per_row:
            return pl.BlockSpec((None, tm, width), lambda *idx: (l, idx[0], cidx(idx)))
        return pl.BlockSpec((None, None, 1, width), lambda *idx: (l, idx[0] // tpb, 0, cidx(idx)))


def _mod_rows(ref, r, rc):
    return ref[...] if ref.shape[0] == 1 else ref[pl.ds(r, rc), :]


def _modnorm_to(h_ref, x_ref, sc_ref, sh_ref, g_ref, rc):
    g = g_ref[...]

    def body(c, carry):
        r = pl.multiple_of(c * rc, rc)
        x = x_ref[pl.ds(r, rc), :]
        ms = jnp.mean(x * x, axis=-1, keepdims=True)
        y = x * lax.rsqrt(ms + EPS) * g
        h = y * (1.0 + _mod_rows(sc_ref, r, rc)) + _mod_rows(sh_ref, r, rc)
        h_ref[pl.ds(r, rc), :] = h.astype(h_ref.dtype)
        return carry

    lax.fori_loop(0, x_ref.shape[0] // rc, body, 0)


def _row_chunk(tm):
    return 128 if tm % 128 == 0 else tm


class _Weight:
    def __init__(self, arr, layer=None, shape=None, col0=0):
        self.arr, self.layer, self.col0 = arr, layer, col0
        self.shape = tuple(shape if shape is not None else (arr.shape[1:] if layer is not None else arr.shape))

    @property
    def master(self):
        return self.layer is not None

    def spec(self, block, index):
        if not self.master:
            return pl.BlockSpec(block, index)
        layer, col0 = self.layer, self.col0

        def master_index(*g):
            idx = tuple(index(*g))
            return (layer,) + idx[:-1] + (idx[-1] + col0,)

        return pl.BlockSpec((None,) + tuple(block), master_index)

    def emit(self, block, index):
        return pl.BlockSpec(block, index), jax.ShapeDtypeStruct(self.shape, BF16)


def _load_weight(w_ref, copy_ref):
    if copy_ref is None:
        return w_ref[...]
    w = w_ref[...].astype(BF16)
    copy_ref[...] = w
    return w


def _single_tile(m, tm, w):
    assert not w.master or m == tm, "a master weight must be streamed by a single row tile"


def _inproj_kernel(x_ref, sc_ref, sh_ref, g_ref, w_ref, o_ref, h_ref, wb_ref=None, *, rc):
    @pl.when(pl.program_id(1) == 0)
    def _():
        _modnorm_to(h_ref, x_ref, sc_ref, sh_ref, g_ref, rc)

    o_ref[...] = _dot(h_ref[...], _load_weight(w_ref, wb_ref))


def _inproj_call(x, mod, g_norm, w, layer, tm):
    m, d = x.shape
    _single_tile(m, tm, w)
    depth = g_norm.shape[0]
    wblock, windex = (d, SLAB), lambda i, j: (0, j)
    out_specs = [pl.BlockSpec((None, tm, SLAB), lambda i, j: (j, i, 0)),
                 pl.BlockSpec((tm, d), lambda i, j: (i, 0), pipeline_mode=pl.Buffered(1))]
    out_shape = [jax.ShapeDtypeStruct((SLAB_GATE, m, SLAB), F32), jax.ShapeDtypeStruct((m, d), BF16)]
    if w.master:
        spec, shape = w.emit(wblock, windex)
        out_specs.append(spec)
        out_shape.append(shape)
    return pl.pallas_call(
        functools.partial(_inproj_kernel, rc=_row_chunk(tm)),
        grid=(m // tm, SLAB_GATE),
        in_specs=[
            pl.BlockSpec((tm, d), lambda i, j: (i, 0)),
            mod.spec(1, tm), mod.spec(0, tm),
            pl.BlockSpec((None, 1, d), lambda i, j: (layer, 0, 0)),
            w.spec(wblock, windex),
        ],
        out_specs=out_specs,
        out_shape=out_shape,
        compiler_params=_params("arbitrary", "arbitrary"),
        name="inproj",
    )(x, mod.arr, mod.arr, g_norm.reshape(depth, 1, d), w.arr)


def _gate_kernel(h_ref, w_ref, o_ref, wb_ref=None):
    z = _dot(h_ref[...], _load_weight(w_ref, wb_ref))
    o_ref[...] = (0.5 * jnp.tanh(0.5 * z) + 0.5).astype(o_ref.dtype)


def _gate_call(h, w, tm):
    m, d = h.shape
    _single_tile(m, tm, w)
    tn = SLAB if w.master else GATE_TN
    n_gate = (N_SLAB - SLAB_GATE) * SLAB
    per = MERGE_TN // tn
    wblock, windex = (d, tn), lambda i, j: (0, j)
    out_specs = [pl.BlockSpec((None, tm, tn), lambda i, j: (j // per, i, j % per))]
    out_shape = [jax.ShapeDtypeStruct((n_gate // MERGE_TN, m, MERGE_TN), BF16)]
    if w.master:
        spec, shape = w.emit(wblock, windex)
        out_specs.append(spec)
        out_shape.append(shape)
    return pl.pallas_call(
        _gate_kernel,
        grid=(m // tm, n_gate // tn),
        in_specs=[pl.BlockSpec((tm, d), lambda i, j: (i, 0)), w.spec(wblock, windex)],
        out_specs=out_specs,
        out_shape=out_shape,
        compiler_params=_params("arbitrary", "arbitrary"),
        name="ingate",
    )(h, w.arr)


def _group_sum_matrix(width, group):
    idx = np.arange(width) // group
    return jnp.asarray((idx[:, None] == idx[None, :]).astype(np.float32), dtype=BF16)


SHORT_GROUP = 4
RET_CHUNKS_PER_STEP = 4


def _group_size(nb_batch, nblocks, has_init):
    return SHORT_GROUP if has_init and nblocks == 1 and nb_batch % SHORT_GROUP == 0 else 1


def _attn_kernel(*refs, layer, tq, has_init, group):
    if group == 1:
        blocks = [_attn_block(*refs, layer=layer, tq=tq, has_init=has_init)]
    else:
        sinks_ref, q_ref, kvc_ref, kinit_ref, vinit_ref, gq_ref, gk_ref, gm512_ref, gm256_ref, sd_ref, o_ref, kn_ref = refs
        blocks = []
        for bi in range(group):
            rows = pl.ds(bi * tq, tq)
            blocks.append(_attn_block(
                sinks_ref, q_ref.at[:, rows, :], kvc_ref.at[rows, :], kinit_ref.at[bi], vinit_ref.at[bi],
                gq_ref, gk_ref, gm512_ref, gm256_ref, sd_ref, o_ref.at[rows, :], kn_ref.at[rows, :],
                layer=layer, tq=tq, has_init=True))
    while blocks:
        blocks = [b for b in blocks if next(b, _DONE) is not _DONE]


_DONE = object()


def _attn_block(*refs, layer, tq, has_init):
    if has_init:
        sinks_ref, q_ref, kvc_ref, kinit_ref, vinit_ref, gq_ref, gk_ref, gm512_ref, gm256_ref, sd_ref, o_ref, kn_ref = refs
    else:
        sinks_ref, q_ref, kvc_ref, kvp_ref, gq_ref, gk_ref, gm512_ref, gm256_ref, sd_ref, o_ref, kn_ref = refs
    n = pl.program_id(1)

    def qk_norm(x, gmat, g):
        x2 = x * x
        hi = x2.astype(BF16)
        lo = (x2 - hi.astype(F32)).astype(BF16)
        ss = _dot(hi, gmat) + _dot(lo, gmat)
        return x * lax.rsqrt(ss * (1.0 / HEAD_DIM) + EPS) * g

    gm256, gk = gm256_ref[...], gk_ref[...]
    kvc = kvc_ref[...]
    kc = qk_norm(kvc[:, :KV_W], gm256, gk)
    kn_ref[...] = kc
    vc = kvc[:, KV_W:]
    if has_init:
        kp, vp = kinit_ref[...], vinit_ref[...]
    else:
        kvp = kvp_ref[...]
        kp, vp = qk_norm(kvp[:, :KV_W], gm256, gk), kvp[:, KV_W:]
    if tq < WINDOW:
        pad = jnp.zeros((WINDOW - tq, KV_W), F32)
        kc = jnp.concatenate([kc, pad], axis=0)
        vc = jnp.concatenate([vc, pad], axis=0)
    kall = jnp.concatenate([kp, kc], axis=0)
    vall = jnp.concatenate([vp, vc], axis=0)

    nk = 2 * WINDOW
    ii = lax.broadcasted_iota(jnp.int32, (tq, nk), 0)
    jj = lax.broadcasted_iota(jnp.int32, (tq, nk), 1)
    dist = WINDOW + ii - jj
    valid = (dist >= 0) & (dist <= WINDOW)
    if not has_init:
        valid = valid & (jj >= jnp.where(n > 0, 0, WINDOW))
    distm = jnp.where(valid, dist.astype(F32), -NEG_BIG)

    gm512, gq = gm512_ref[...], gq_ref[...] * (HEAD_DIM ** -0.5)
    qn = [qk_norm(q_ref[s], gm512, gq) for s in range(2)]
    lane_k = lax.broadcasted_iota(jnp.int32, (nk, LANES), 1) < HEAD_DIM
    lane_q = lax.broadcasted_iota(jnp.int32, (tq, LANES), 1) < HEAD_DIM
    sd = sd_ref[...]

    def two_copies(a, upper):
        if upper:
            bot = jnp.where(lane_k, 0.0, a)
            top = pltpu.roll(bot, HEAD_DIM, 1)
        else:
            top = jnp.where(lane_k, a, 0.0)
            bot = pltpu.roll(top, HEAD_DIM, 1)
        return jnp.concatenate([top, bot], axis=0).astype(BF16)

    kds, vds = [], []
    for kv in range(N_KV):
        cs = slice(LANES * (kv // 2), LANES * (kv // 2) + LANES)
        kds.append(two_copies(kall[:, cs], bool(kv % 2)))
        vds.append(two_copies(vall[:, cs], bool(kv % 2)))
    npair = N_HEADS // 2
    yield
    scores = []
    for p in range(npair):
        off = LANES * (p % 4)
        qp = qn[p // 4][:, off:off + LANES].astype(BF16)
        scores.append(_nt_dot(qp, kds[p // 2]))
    yield
    probs, sinkw = [], []
    for p in range(npair):
        es, sk = [], []
        for hh in range(2):
            h = 2 * p + hh
            slope = 2.0 ** (-8.0 * (h + 1) / N_HEADS)
            sink = sinks_ref[layer, h]
            sh = scores[p][:, nk * hh:nk * hh + nk] - slope * distm
            mx = jnp.maximum(jnp.max(sh, axis=-1, keepdims=True), sink)
            es.append(jnp.exp(sh - mx))
            sk.append(jnp.exp(sink - mx))
        probs.append(jnp.concatenate(es, axis=1).astype(BF16))
        sinkw.append(jnp.where(lane_q, sk[0], sk[1]))
    yield
    for p in range(npair):
        num = _dot(probs[p], vds[p // 2])
        den = _dot(probs[p], sd) + sinkw[p]
        o_ref[:, LANES * p:LANES * p + LANES] = (num / den).astype(o_ref.dtype)


def _attn_call(z3, sinks, gq_t, gk_t, layer, nb_batch, t_len, cache=None):
    m = z3.shape[1]
    has_init = cache is not None
    tq = WINDOW if t_len % WINDOW == 0 else t_len
    nb = t_len // tq
    assert nb == 1 or not has_init
    depth = gq_t.shape[0]
    gm512 = _group_sum_matrix(SLAB, HEAD_DIM)
    gm256 = _group_sum_matrix(KV_W, HEAD_DIM)
    sd_np = np.zeros((4 * WINDOW, LANES), np.float32)
    sd_np[:2 * WINDOW, :HEAD_DIM] = 1.0
    sd_np[2 * WINDOW:, HEAD_DIM:] = 1.0
    sd = jnp.asarray(sd_np, dtype=BF16)

    group = _group_size(nb_batch, nb, has_init)
    rows = group * tq
    in_specs = [
        pl.BlockSpec(memory_space=pltpu.SMEM),
        pl.BlockSpec((2, rows, SLAB), lambda b, n: (0, b * nb + n, 0)),
        pl.BlockSpec((None, rows, SLAB), lambda b, n: (SLAB_KV, b * nb + n, 0)),
    ]
    args = [sinks, z3, z3]
    if has_init:
        cache_spec = pl.BlockSpec((None, group, WINDOW, KV_W), lambda b, n: (layer, b, 0, 0))
        if group == 1:
            cache_spec = pl.BlockSpec((None, None, WINDOW, KV_W), lambda b, n: (layer, b, 0, 0))
        in_specs += [cache_spec] * 2
        args += [cache[0], cache[1]]
    else:
        in_specs += [pl.BlockSpec((None, WINDOW, SLAB), lambda b, n: (SLAB_KV, jnp.maximum(b * nb + n - 1, 0), 0))]
        args += [z3]
    in_specs += [
        pl.BlockSpec((None, 1, SLAB), lambda b, n: (layer, 0, 0)),
        pl.BlockSpec((None, 1, KV_W), lambda b, n: (layer, 0, 0)),
        pl.BlockSpec((SLAB, SLAB), lambda b, n: (0, 0)),
        pl.BlockSpec((KV_W, KV_W), lambda b, n: (0, 0)),
        pl.BlockSpec((4 * WINDOW, LANES), lambda b, n: (0, 0)),
    ]
    args += [gq_t, gk_t, gm512, gm256, sd]
    out_dtype = F32 if has_init else BF16
    return pl.pallas_call(
        functools.partial(_attn_kernel, layer=layer, tq=tq, has_init=has_init, group=group),
        grid=(nb_batch // group, nb),
        in_specs=in_specs,
        out_specs=[
            pl.BlockSpec((rows, BR_W), lambda b, n: (b * nb + n, 0)),
            pl.BlockSpec((rows, KV_W), lambda b, n: (b * nb + n, 0)),
        ],
        out_shape=[jax.ShapeDtypeStruct((m, BR_W), out_dtype), jax.ShapeDtypeStruct((m, KV_W), F32)],
        compiler_params=_params("arbitrary", "arbitrary"),
        name="attn",
    )(*args)


def _conv_kernel(*refs, tt, nt, has_init):
    if has_init:
        l0, l1, g0, g1, init_ref, w_ref, b_ref, gl_ref, bl_ref, o_ref, new_ref, ext, ybuf, shifted = refs
    else:
        l0, l1, g0, g1, w_ref, b_ref, gl_ref, bl_ref, o_ref, new_ref, ext, ybuf, shifted = refs
    t = pl.program_id(1)

    @pl.when(t == 0)
    def _():
        ext[0:CONV_HALO, :] = init_ref[...] if has_init else jnp.zeros((CONV_HALO, BR_W), F32)

    if nt > 1:
        @pl.when(t > 0)
        def _():
            ext[0:CONV_HALO, :] = ext[tt:tt + CONV_HALO, :]

    for cb, (lr, gr) in enumerate(((l0, g0), (l1, g1))):
        ext[CONV_HALO:CONV_HALO + tt, SLAB * cb:SLAB * cb + SLAB] = lr[...] * jax.nn.sigmoid(gr[...])
    new_ref[...] = ext[tt:tt + CONV_HALO, :]

    rs = min(tt, 32)
    base = CONV_HALO - (CONV_K - 1)
    nsh = shifted.shape[1]
    for c in range(BR_W // LANES):
        cs = slice(LANES * c, LANES * c + LANES)
        for r in range(1, SUBLANES):
            shifted[r - 1] = ext[r:r + nsh, cs]

        def taps(i, carry, cs=cs):
            r0 = pl.multiple_of(i * rs, rs)
            acc = jnp.broadcast_to(b_ref[:, cs], (rs, LANES))
            for r in range(SUBLANES):
                offs = [(base + k) // SUBLANES for k in range(CONV_K) if (base + k) % SUBLANES == r]
                rows = pl.ds(r0 + SUBLANES * offs[0], rs + SUBLANES * (offs[-1] - offs[0]))
                win = ext[rows, cs] if r == 0 else shifted[r - 1, rows, :]
                for a in offs:
                    k = SUBLANES * a + r - base
                    d = SUBLANES * (a - offs[0])
                    acc = acc + w_ref[k:k + 1, cs] * win[d:d + rs]
            ybuf[pl.ds(r0, rs), cs] = acc
            return carry

        lax.fori_loop(0, tt // rs, taps, 0)

    def norm_act(i, carry):
        rows = pl.ds(pl.multiple_of(i * rs, rs), rs)
        y = ybuf[rows, :]
        yc = y - jnp.mean(y, axis=-1, keepdims=True)
        var = jnp.mean(yc * yc, axis=-1, keepdims=True)
        yn = yc * lax.rsqrt(var + EPS) * gl_ref[...] + bl_ref[...]
        o_ref[rows, :] = _silu(yn).astype(o_ref.dtype)
        return carry

    lax.fori_loop(0, tt // rs, norm_act, 0, unroll=min(4, tt // rs))


def _conv_short_kernel(l0, l1, g0, g1, init_ref, w_ref, b_ref, gl_ref, bl_ref, o_ref, new_ref, ext, ybuf, *,
                       nbat, t_len):
    ext[:, 0:SUBLANES, :] = jnp.zeros((nbat, SUBLANES, BR_W), F32)
    ext[:, CONV_HALO - (CONV_K - 1):CONV_HALO, :] = init_ref[...]
    for cb, (lr, gr) in enumerate(((l0, g0), (l1, g1))):
        u = lr[...] * jax.nn.sigmoid(gr[...])
        ext[:, CONV_HALO:CONV_HALO + t_len, SLAB * cb:SLAB * cb + SLAB] = u.reshape(nbat, t_len, SLAB)
    new_ref[...] = ext[:, t_len:t_len + CONV_HALO, :]
    base = CONV_HALO - (CONV_K - 1)
    for c in range(BR_W // LANES):
        cs = slice(LANES * c, LANES * c + LANES)
        acc = jnp.broadcast_to(b_ref[:, cs], (nbat, t_len, LANES))
        for k in range(CONV_K):
            acc = acc + w_ref[k:k + 1, cs] * ext[:, base + k:base + k + t_len, cs]
        ybuf[:, cs] = acc.reshape(nbat * t_len, LANES)
    rs = 32

    def norm_act(i, carry):
        rows = pl.ds(pl.multiple_of(i * rs, rs), rs)
        y = ybuf[rows, :]
        yc = y - jnp.mean(y, axis=-1, keepdims=True)
        var = jnp.mean(yc * yc, axis=-1, keepdims=True)
        yn = yc * lax.rsqrt(var + EPS) * gl_ref[...] + bl_ref[...]
        o_ref[rows, :] = _silu(yn).astype(o_ref.dtype)
        return carry

    lax.fori_loop(0, nbat * t_len // rs, norm_act, 0, unroll=min(4, nbat * t_len // rs))


def _short_batched(nb_batch, t_len, has_init):
    return has_init and t_len == SUBLANES and (nb_batch * t_len) % 32 == 0


def _conv_short_call(z3, init, w_dw, b_dw, g_ln, b_ln, layer, nb_batch, t_len):
    m = z3.shape[1]
    depth = w_dw.shape[0]
    slab = lambda j: pl.BlockSpec((None, m, SLAB), lambda i: (j, 0, 0))
    vec = lambda: pl.BlockSpec((None, 1, BR_W), lambda i: (layer, 0, 0))
    return pl.pallas_call(
        functools.partial(_conv_short_kernel, nbat=nb_batch, t_len=t_len),
        grid=(1,),
        in_specs=[slab(SLAB_CONV), slab(SLAB_CONV + 1), slab(SLAB_CONV + 2), slab(SLAB_CONV + 3),
                  pl.BlockSpec((None, nb_batch, CONV_K - 1, BR_W), lambda i: (layer, 0, 0, 0)),
                  pl.BlockSpec((None, CONV_K, BR_W), lambda i: (layer, 0, 0)), vec(), vec(), vec()],
        out_specs=[pl.BlockSpec((m, BR_W), lambda i: (0, 0)),
                   pl.BlockSpec((nb_batch, CONV_HALO, BR_W), lambda i: (0, 0, 0))],
        out_shape=[jax.ShapeDtypeStruct((m, BR_W), F32), jax.ShapeDtypeStruct((nb_batch, CONV_HALO, BR_W), F32)],
        scratch_shapes=[pltpu.VMEM((nb_batch, CONV_HALO + t_len, BR_W), F32), pltpu.VMEM((m, BR_W), F32)],
        compiler_params=_params("arbitrary"),
        name="conv_short",
    )(z3, z3, z3, z3, init, w_dw, b_dw.reshape(depth, 1, BR_W), g_ln.reshape(depth, 1, BR_W),
      b_ln.reshape(depth, 1, BR_W))


def _conv_call(z3, w_dw, b_dw, g_ln, b_ln, layer, nb_batch, t_len, init=None):
    m = z3.shape[1]
    has_init = init is not None
    if _short_batched(nb_batch, t_len, has_init):
        return _conv_short_call(z3, init, w_dw, b_dw, g_ln, b_ln, layer, nb_batch, t_len)
    if has_init:
        init = jnp.pad(init, ((0, 0), (0, 0), (CONV_HALO - (CONV_K - 1), 0), (0, 0)))
    tt = 512 if t_len % 512 == 0 else (256 if t_len % 256 == 0 else t_len)
    nt = t_len // tt
    assert nt == 1 or tt >= CONV_HALO
    depth = w_dw.shape[0]

    def slab(j):
        return pl.BlockSpec((None, tt, SLAB), lambda b, t: (j, b * nt + t, 0))

    def vec():
        return pl.BlockSpec((None, 1, BR_W), lambda b, t: (layer, 0, 0))

    in_specs = [slab(SLAB_CONV), slab(SLAB_CONV + 1), slab(SLAB_CONV + 2), slab(SLAB_CONV + 3)]
    args = [z3, z3, z3, z3]
    if has_init:
        in_specs.append(pl.BlockSpec((None, None, CONV_HALO, BR_W), lambda b, t: (layer, b, 0, 0)))
        args.append(init)
    in_specs += [pl.BlockSpec((None, CONV_K, BR_W), lambda b, t: (layer, 0, 0)), vec(), vec(), vec()]
    args += [w_dw, b_dw.reshape(depth, 1, BR_W), g_ln.reshape(depth, 1, BR_W), b_ln.reshape(depth, 1, BR_W)]
    return pl.pallas_call(
        functools.partial(_conv_kernel, tt=tt, nt=nt, has_init=has_init),
        grid=(nb_batch, nt),
        in_specs=in_specs,
        out_specs=[
            pl.BlockSpec((tt, BR_W), lambda b, t: (b * nt + t, 0)),
            pl.BlockSpec((None, CONV_HALO, BR_W), lambda b, t: (b, 0, 0)),
        ],
        out_shape=[
            jax.ShapeDtypeStruct((m, BR_W), F32 if has_init else BF16),
            jax.ShapeDtypeStruct((nb_batch, CONV_HALO, BR_W), F32),
        ],
        scratch_shapes=[pltpu.VMEM((CONV_HALO + tt, BR_W), F32), pltpu.VMEM((tt, BR_W), F32),
                        pltpu.VMEM((SUBLANES - 1, CONV_HALO + tt - SUBLANES, LANES), F32)],
        compiler_params=_params("arbitrary", "arbitrary"),
        name="conv",
    )(*args)


def _pool_kernel(*refs, tt, nt, has_init, pos0):
    if has_init:
        u0, u1, init_ref, w_ref, s_ref, o_ref, ext = refs
    else:
        u0, u1, w_ref, s_ref, o_ref, ext = refs
    t = pl.program_id(1)

    @pl.when(t == 0)
    def _():
        ext[0:POOL_HALO, :] = init_ref[...] if has_init else jnp.zeros((POOL_HALO, BR_W), F32)

    if nt > 1:
        @pl.when(t > 0)
        def _():
            ext[0:POOL_HALO, :] = ext[tt:tt + POOL_HALO, :]

    ext[POOL_HALO:POOL_HALO + tt, 0:SLAB] = u0[...]
    ext[POOL_HALO:POOL_HALO + tt, SLAB:2 * SLAB] = u1[...]
    pos = pos0 + t * tt + lax.broadcasted_iota(jnp.int32, (tt, 1), 0)
    for g, w in enumerate(POOL_WINDOWS):
        cs = slice(POOL_G * g, POOL_G * g + POOL_G)
        cur = ext[POOL_HALO:POOL_HALO + tt, cs]
        wsum = cur
        for s in range(1, w):
            wsum = wsum + ext[POOL_HALO - s:POOL_HALO - s + tt, cs]
        cnt = jnp.minimum(pos + 1, w).astype(F32)
        zg = wsum / cnt - cur
        y = _dot(zg.astype(BF16), w_ref[g].astype(BF16)) * s_ref[:, cs]
        o_ref[:, cs] = y.astype(o_ref.dtype)


def _pool_short_kernel(u0, u1, init_ref, w_ref, s_ref, o_ref, ext, *, nbat, t_len, pos0):
    ext[:, 0:SUBLANES, :] = jnp.zeros((nbat, SUBLANES, BR_W), F32)
    ext[:, POOL_HALO - POOL_PAD:POOL_HALO, :] = init_ref[...]
    ext[:, POOL_HALO:POOL_HALO + t_len, 0:SLAB] = u0[...].reshape(nbat, t_len, SLAB)
    ext[:, POOL_HALO:POOL_HALO + t_len, SLAB:2 * SLAB] = u1[...].reshape(nbat, t_len, SLAB)
    pos = pos0 + lax.broadcasted_iota(jnp.int32, (nbat, t_len, POOL_G), 1)
    for g, w in enumerate(POOL_WINDOWS):
        cs = slice(POOL_G * g, POOL_G * g + POOL_G)
        cur = ext[:, POOL_HALO:POOL_HALO + t_len, cs]
        wsum = cur
        for s in range(1, w):
            wsum = wsum + ext[:, POOL_HALO - s:POOL_HALO - s + t_len, cs]
        cnt = jnp.minimum(pos + 1, w).astype(F32)
        zg = (wsum / cnt - cur).reshape(nbat * t_len, POOL_G)
        y = _dot(zg.astype(BF16), w_ref[g].astype(BF16)) * s_ref[:, cs]
        o_ref[:, cs] = y.astype(o_ref.dtype)


def _pool_short_call(z3, init, w_pool, s_pool, layer, nb_batch, t_len, pos0):
    m = z3.shape[1]
    depth = w_pool.shape[0]
    slab = lambda j: pl.BlockSpec((None, m, SLAB), lambda i: (j, 0, 0))
    return pl.pallas_call(
        functools.partial(_pool_short_kernel, nbat=nb_batch, t_len=t_len, pos0=pos0),
        grid=(1,),
        in_specs=[slab(SLAB_POOL), slab(SLAB_POOL + 1),
                  pl.BlockSpec((None, nb_batch, POOL_PAD, BR_W), lambda i: (layer, 0, 0, 0)),
                  pl.BlockSpec((None, len(POOL_WINDOWS), POOL_G, POOL_G), lambda i: (layer, 0, 0, 0)),
                  pl.BlockSpec((None, 1, BR_W), lambda i: (layer, 0, 0))],
        out_specs=pl.BlockSpec((m, BR_W), lambda i: (0, 0)),
        out_shape=jax.ShapeDtypeStruct((m, BR_W), F32),
        scratch_shapes=[pltpu.VMEM((nb_batch, POOL_HALO + t_len, BR_W), F32)],
        compiler_params=_params("arbitrary"),
        name="pool_short",
    )(z3, z3, init, w_pool, s_pool.reshape(depth, 1, BR_W))


def _pool_call(z3, w_pool, s_pool, layer, nb_batch, t_len, pos0, init=None):
    m = z3.shape[1]
    has_init = init is not None
    if _short_batched(nb_batch, t_len, has_init):
        return _pool_short_call(z3, init, w_pool, s_pool, layer, nb_batch, t_len, pos0)
    if has_init:
        init = jnp.pad(init, ((0, 0), (0, 0), (POOL_HALO - POOL_PAD, 0), (0, 0)))
    tt = 256 if t_len % 256 == 0 else t_len
    nt = t_len // tt
    assert nt == 1 or tt >= POOL_HALO
    depth = w_pool.shape[0]

    def slab(j):
        return pl.BlockSpec((None, tt, SLAB), lambda b, t: (j, b * nt + t, 0))

    in_specs = [slab(SLAB_POOL), slab(SLAB_POOL + 1)]
    args = [z3, z3]
    if has_init:
        in_specs.append(pl.BlockSpec((None, None, POOL_HALO, BR_W), lambda b, t: (layer, b, 0, 0)))
        args.append(init)
    in_specs += [
        pl.BlockSpec((None, len(POOL_WINDOWS), POOL_G, POOL_G), lambda b, t: (layer, 0, 0, 0)),
        pl.BlockSpec((None, 1, BR_W), lambda b, t: (layer, 0, 0)),
    ]
    args += [w_pool, s_pool.reshape(depth, 1, BR_W)]
    return pl.pallas_call(
        functools.partial(_pool_kernel, tt=tt, nt=nt, has_init=has_init, pos0=pos0),
        grid=(nb_batch, nt),
        in_specs=in_specs,
        out_specs=pl.BlockSpec((tt, BR_W), lambda b, t: (b * nt + t, 0)),
        out_shape=jax.ShapeDtypeStruct((m, BR_W), F32 if has_init else BF16),
        scratch_shapes=[pltpu.VMEM((POOL_HALO + tt, BR_W), F32)],
        compiler_params=_params("arbitrary", "arbitrary"),
        name="pool",
    )(*args)


def _ret_constants(c, cp):
    lg = np.log1p(-np.exp2(-5.0 - np.arange(RET_HEADS, dtype=np.float64)))
    i = np.arange(c, dtype=np.float64)
    diff = i[:, None] - i[None, :]
    decay = np.where(diff >= 0, np.exp(lg[:, None, None] * np.maximum(diff, 0.0)), 0.0)
    dec = np.zeros((RET_HEADS // 2, c, 2 * cp))
    for h in range(RET_HEADS):
        dec[h // 2, :, (h % 2) * cp:(h % 2) * cp + c] = decay[h]
    kfac = np.repeat(np.exp(lg[None, :] * (c - 1 - i)[:, None]), RET_DK, axis=1) * RET_DK ** -0.5
    cfac = np.repeat(np.exp(lg[None, :] * (i + 1)[:, None]), RET_DV, axis=1)
    gch = np.repeat(np.exp(lg * c), RET_DK).reshape(RET_HEADS // 2, 2 * RET_DK, 1)
    gch = np.broadcast_to(gch, (RET_HEADS // 2, 2 * RET_DK, RET_DV))
    f = lambda a: jnp.asarray(np.ascontiguousarray(a), dtype=F32)
    return f(dec), f(kfac), f(cfac), f(gch)


def _ret_kernel(*refs, c, cp, has_init, group, cps):
    if group == 1 and cps == 1:
        return _ret_chunk(*refs, c=c, cp=cp, has_init=has_init)
    if group == 1:
        ins, rest = refs[:6], refs[6:-2]
        o_ref, s_ref = refs[-2:]
        for ci in range(cps):
            rows = pl.ds(ci * c, c)
            _ret_chunk(*[r.at[rows, :] for r in ins], *rest, o_ref.at[rows, :], s_ref,
                       c=c, cp=cp, has_init=has_init, first=ci == 0)
        return
    rq, rk, rv0, rv1, rg0, rg1, s0_ref, gn_ref, dec_ref, kf_ref, cf_ref, gch_ref, o_ref, s_ref = refs
    for bi in range(group):
        rows = pl.ds(bi * c, c)
        _ret_chunk(rq.at[rows, :], rk.at[rows, :], rv0.at[rows, :], rv1.at[rows, :], rg0.at[rows, :], rg1.at[rows, :],
                   s0_ref.at[bi], gn_ref, dec_ref, kf_ref, cf_ref, gch_ref, o_ref.at[rows, :], s_ref.at[bi],
                   c=c, cp=cp, has_init=True)


def _ret_chunk(*refs, c, cp, has_init, first=True):
    if has_init:
        rq, rk, rv0, rv1, rg0, rg1, s0_ref, gn_ref, dec_ref, kf_ref, cf_ref, gch_ref, o_ref, s_ref = refs
    else:
        rq, rk, rv0, rv1, rg0, rg1, gn_ref, dec_ref, kf_ref, cf_ref, gch_ref, o_ref, s_ref = refs
    n = pl.program_id(1)

    if first:
        @pl.when(n == 0)
        def _():
            s_ref[...] = s0_ref[...] if has_init else jnp.zeros(s_ref.shape, F32)

    lo = lax.broadcasted_iota(jnp.int32, (c, LANES), 1) < RET_DK
    q = rq[...]
    kraw = rk[...]
    k = kraw * (RET_DK ** -0.5)
    kdec = kraw * kf_ref[...]
    rvs, rgs = (rv0, rv1), (rg0, rg1)

    def stack_heads(a):
        a0, a1 = jnp.where(lo, a, 0.0), jnp.where(lo, 0.0, a)
        if cp > c:
            z = jnp.zeros((cp - c, LANES), F32)
            return jnp.concatenate([a0, z, a1, z], axis=0)
        return jnp.concatenate([a0, a1], axis=0)

    npair = RET_HEADS // 2
    zc = jnp.zeros((c, LANES), F32)
    scores, vbds, vsts, inners, crosses = [], [], [], [], []
    for p in range(npair):
        cs = slice(LANES * p, LANES * p + LANES)
        scores.append(_nt_dot(q[:, cs].astype(BF16), stack_heads(k[:, cs]).astype(BF16)) * dec_ref[p])
        vs = [rvs[h // 4][:, LANES * (h % 4):LANES * (h % 4) + LANES] for h in (2 * p, 2 * p + 1)]
        rows0 = jnp.concatenate([vs[0], zc], axis=1)
        rows1 = jnp.concatenate([zc, vs[1]], axis=1)
        if cp > c:
            zp = jnp.zeros((cp - c, 2 * LANES), F32)
            zq = jnp.zeros((cp - c, LANES), F32)
            vbds.append(jnp.concatenate([rows0, zp, rows1, zp], axis=0).astype(BF16))
            vsts.append(jnp.concatenate([vs[0], zq, vs[1], zq], axis=0).astype(BF16))
        else:
            vbds.append(jnp.concatenate([rows0, rows1], axis=0).astype(BF16))
            vsts.append(jnp.concatenate([vs[0], vs[1]], axis=0).astype(BF16))
    for p in range(npair):
        cs = slice(LANES * p, LANES * p + LANES)
        qp = q[:, cs]
        inners.append(_dot(scores[p].astype(BF16), vbds[p]))
        sprev = s_ref[p]
        qst = jnp.concatenate([jnp.where(lo, qp, 0.0), jnp.where(lo, 0.0, qp)], axis=0)
        crosses.append(_dot(qst.astype(BF16), sprev.astype(BF16)))
        upd = _tn_dot(stack_heads(kdec[:, cs]).astype(BF16), vsts[p])
        s_ref[p] = gch_ref[p] * sprev + upd
    for p in range(npair):
        inner, cross = inners[p], crosses[p]
        for hh, h in enumerate((2 * p, 2 * p + 1)):
            hc = slice(LANES * h, LANES * h + LANES)
            o = inner[:, LANES * hh:LANES * hh + LANES] + cross[c * hh:c * hh + c, :] * cf_ref[:, hc]
            oc = o - jnp.mean(o, axis=-1, keepdims=True)
            var = jnp.mean(oc * oc, axis=-1, keepdims=True)
            gate = rgs[h // 4][:, LANES * (h % 4):LANES * (h % 4) + LANES]
            y = oc * lax.rsqrt(var + EPS) * gn_ref[:, hc] * _silu(gate)
            o_ref[:, hc] = y.astype(o_ref.dtype)


def _ret_call(z3, g_ret, layer, nb_batch, t_len, init=None):
    m = z3.shape[1]
    has_init = init is not None
    c = RET_CHUNK if t_len % RET_CHUNK == 0 else t_len
    cp = max(c, 64)
    nc = t_len // c
    depth = g_ret.shape[0]
    dec, kfac, cfac, gch = _ret_constants(c, cp)
    npair = RET_HEADS // 2

    group = _group_size(nb_batch, nc, has_init)
    cps = RET_CHUNKS_PER_STEP if group == 1 and nc % RET_CHUNKS_PER_STEP == 0 else 1
    ns = nc // cps
    rows = group * cps * c
    state_block = (npair, LANES, RET_DV) if group == 1 else (group, npair, LANES, RET_DV)
    lead = (None,) if group == 1 else ()

    def slab(j):
        return pl.BlockSpec((None, rows, SLAB), lambda b, n: (j, b * ns + n, 0))

    def const(shape):
        nd = len(shape)
        return pl.BlockSpec(shape, lambda b, n: (0,) * nd)

    in_specs = [slab(SLAB_RQ), slab(SLAB_RK), slab(SLAB_RV), slab(SLAB_RV + 1), slab(SLAB_RG), slab(SLAB_RG + 1)]
    args = [z3] * 6
    if has_init:
        in_specs.append(pl.BlockSpec((None,) + lead + state_block, lambda b, n: (layer, b, 0, 0, 0)))
        args.append(init)
    in_specs += [pl.BlockSpec((None, 1, BR_W), lambda b, n: (layer, 0, 0)),
                 const(dec.shape), const(kfac.shape), const(cfac.shape), const(gch.shape)]
    args += [g_ret.reshape(depth, 1, BR_W), dec, kfac, cfac, gch]
    return pl.pallas_call(
        functools.partial(_ret_kernel, c=c, cp=cp, has_init=has_init, group=group, cps=cps),
        grid=(nb_batch // group, ns),
        in_specs=in_specs,
        out_specs=[
            pl.BlockSpec((rows, BR_W), lambda b, n: (b * ns + n, 0)),
            pl.BlockSpec(lead + state_block, lambda b, n: (b, 0, 0, 0)),
        ],
        out_shape=[
            jax.ShapeDtypeStruct((m, BR_W), F32 if has_init else BF16),
            jax.ShapeDtypeStruct((nb_batch, npair, LANES, RET_DV), F32),
        ],
        compiler_params=_params("arbitrary", "arbitrary"),
        name="retention",
    )(*args)


def _merge_kernel(y0, y1, y2, y3, gate_ref, w_ref, o_ref, *rest):
    wb_ref, acc_ref = rest if len(rest) == 2 else (None, rest[0])
    r, c = pl.program_id(1), pl.program_id(2)
    if wb_ref is not None:
        wb_ref[...] = w_ref[...].astype(BF16)
        w_ref = wb_ref
    for k, y_ref in enumerate((y0, y1, y2, y3)):
        @pl.when(r == k)
        def _(k=k, y_ref=y_ref):
            val = gate_ref[...].astype(F32) * _dot(y_ref[...].astype(BF16), w_ref[...])
            if k == 0:
                acc_ref[c] = val
            elif k < N_BR - 1:
                acc_ref[c] += val
            else:
                o_ref[...] = (acc_ref[c] + val).astype(o_ref.dtype)


def _merge_call(ys, gates, w, tm):
    m = gates.shape[1]
    _single_tile(m, tm, w)
    tn = MERGE_TN
    nc = D_MODEL // tn
    y_spec = pl.BlockSpec((tm, BR_W), lambda i, r, c: (i, 0))
    wblock, windex = (None, BR_W, tn), lambda i, r, c: (r, 0, c)
    out_specs = [pl.BlockSpec((tm, tn), lambda i, r, c: (i, jnp.where(r == N_BR - 1, c, 0)))]
    out_shape = [jax.ShapeDtypeStruct((m, D_MODEL), BF16)]
    if w.master:
        spec, shape = w.emit(wblock, windex)
        out_specs.append(spec)
        out_shape.append(shape)
    return pl.pallas_call(
        _merge_kernel,
        grid=(m // tm, N_BR, nc),
        in_specs=[y_spec] * N_BR + [
            pl.BlockSpec((None, tm, tn), lambda i, r, c: (nc * r + c, i, 0)),
            w.spec(wblock, windex),
        ],
        out_specs=out_specs,
        out_shape=out_shape,
        scratch_shapes=[pltpu.VMEM((nc, tm, tn), F32)],
        compiler_params=_params("arbitrary", "arbitrary", "arbitrary"),
        name="merge",
    )(*ys, gates, w.arr)


def _outproj_kernel(m_ref, w_ref, x_ref, gt_ref, o_ref, wb_ref=None):
    o_ref[...] = x_ref[...] + gt_ref[...] * _dot(m_ref[...], _load_weight(w_ref, wb_ref))


def _outproj_call(merged, x, mod, w, tm):
    m, d = x.shape
    _single_tile(m, tm, w)
    wblock, windex = (d, SLAB), lambda i, c: (0, c)
    out_specs = [pl.BlockSpec((tm, SLAB), lambda i, c: (i, c))]
    out_shape = [jax.ShapeDtypeStruct((m, d), F32)]
    if w.master:
        spec, shape = w.emit(wblock, windex)
        out_specs.append(spec)
        out_shape.append(shape)
    return pl.pallas_call(
        _outproj_kernel,
        grid=(m // tm, d // SLAB),
        in_specs=[
            pl.BlockSpec((tm, d), lambda i, c: (i, 0)),
            w.spec(wblock, windex),
            pl.BlockSpec((tm, SLAB), lambda i, c: (i, c)),
            mod.spec(2, tm, width=SLAB, col=1),
        ],
        out_specs=out_specs,
        out_shape=out_shape,
        compiler_params=_params("arbitrary", "arbitrary"),
        name="outproj",
    )(merged, w.arr, x, mod.arr)


def _mlp_kernel(x_ref, sc_ref, sh_ref, gt_ref, g_ref, w1_ref, w2_ref, o_ref, *rest, rc, nf):
    (w1b_ref, w2b_ref, h_ref) = rest if len(rest) == 3 else (None, None, rest[0])
    f = pl.program_id(1)

    @pl.when(f == 0)
    def _():
        _modnorm_to(h_ref, x_ref, sc_ref, sh_ref, g_ref, rc)
        o_ref[...] = jnp.zeros(o_ref.shape, F32)

    a = _dot(h_ref[...], _load_weight(w1_ref, w1b_ref))
    a = jnp.square(jnp.maximum(a, 0.0)).astype(BF16)
    if w2b_ref is not None:
        w2b_ref[...] = w2_ref[...].astype(BF16)
        w2_ref = w2b_ref
    for c0 in range(0, o_ref.shape[1], SLAB):
        o_ref[:, c0:c0 + SLAB] += _dot(a, w2_ref[:, c0:c0 + SLAB])

    @pl.when(f == nf - 1)
    def _():
        def body(cidx, carry):
            r = pl.multiple_of(cidx * rc, rc)
            rows = pl.ds(r, rc)
            o_ref[rows, :] = x_ref[rows, :] + _mod_rows(gt_ref, r, rc) * o_ref[rows, :]
            return carry

        lax.fori_loop(0, x_ref.shape[0] // rc, body, 0)


def _mlp_call(x, mod, g_norm, w1, w2, layer, tm, tf):
    m, d = x.shape
    _single_tile(m, tm, w1)
    assert w1.master == w2.master
    depth = g_norm.shape[0]
    nf = w1.shape[1] // tf
    w1block, w1index = (d, tf), lambda i, f: (0, f)
    w2block, w2index = (tf, d), lambda i, f: (f, 0)
    out_specs = [pl.BlockSpec((tm, d), lambda i, f: (i, 0))]
    out_shape = [jax.ShapeDtypeStruct((m, d), F32)]
    if w1.master:
        for spec, shape in (w1.emit(w1block, w1index), w2.emit(w2block, w2index)):
            out_specs.append(spec)
            out_shape.append(shape)
    return pl.pallas_call(
        functools.partial(_mlp_kernel, rc=_row_chunk(tm), nf=nf),
        grid=(m // tm, nf),
        in_specs=[
            pl.BlockSpec((tm, d), lambda i, f: (i, 0)),
            mod.spec(4, tm), mod.spec(3, tm), mod.spec(5, tm),
            pl.BlockSpec((None, 1, d), lambda i, f: (layer, 0, 0)),
            w1.spec(w1block, w1index),
            w2.spec(w2block, w2index),
        ],
        out_specs=out_specs,
        out_shape=out_shape,
        scratch_shapes=[pltpu.VMEM((tm, d), BF16)],
        compiler_params=_params("arbitrary", "arbitrary"),
        name="mlp",
    )(x, mod.arr, mod.arr, mod.arr, g_norm.reshape(depth, 1, d), w1.arr, w2.arr)


def _layer(x, mod, p, w, layer, nb_batch, t_len, tm_in, tm, tf, cache):
    copies = {}

    def split(outs, *names):
        outs = list(outs)
        for name in reversed(names):
            if w[name].master:
                copies[name] = outs.pop()
        return outs if len(outs) > 1 else outs[0]

    z3, h = split(_inproj_call(x, mod, p["g_norm1"], w["in_mix"], layer, tm_in), "in_mix")
    gates = split(_gate_call(h, w["in_gate"], tm_in), "in_gate")
    if cache is None:
        attn_cache = conv_init = pool_init = ret_init = None
        pos0 = 0
    else:
        attn_cache, conv_init, pool_init, ret_init = cache
        pos0 = PAST_LEN
    y_att, k_norm = _attn_call(z3, p["attn_sinks"], p["gq_t"], p["gk_t"], layer, nb_batch, t_len, attn_cache)
    y_conv, conv_tail = _conv_call(z3, p["w_dw"], p["b_dw"], p["g_conv_ln"], p["b_conv_ln"], layer, nb_batch, t_len,
                                   conv_init)
    y_pool = _pool_call(z3, p["w_pool"], p["s_pool"], layer, nb_batch, t_len, pos0, pool_init)
    y_ret, s_new = _ret_call(z3, p["g_ret_norm"], layer, nb_batch, t_len, ret_init)
    merged = split(_merge_call((y_att, y_conv, y_pool, y_ret), gates, w["br"], tm), "br")
    x = split(_outproj_call(merged, x, mod, w["out"], tm_in), "out")
    x = split(_mlp_call(x, mod, p["g_norm2"], w["mlp1"], w["mlp2"], layer, tm, tf), "mlp1", "mlp2")
    return x, (z3, k_norm, conv_tail, s_new), copies


def kernel(x_prompt, x_sample, c_prompt, c_sample, cache_attn_k, cache_attn_v, state_conv, state_pool, state_ret,
           w_ada, b_ada, g_norm1, g_norm2, w_in, g_qnorm, g_knorm, attn_sinks, w_dw, b_dw, g_conv_ln, b_conv_ln,
           w_pool, s_pool, g_ret_norm, w_br, w_out, w_mlp1, w_mlp2):
    nb, t_len, d = x_prompt.shape
    nsb, st_len, _ = x_sample.shape
    depth = w_ada.shape[0]
    mp, ms = nb * t_len, nsb * st_len

    n_c = nb + nsb
    r_pad = -(-n_c // 16) * 16
    c_all = jnp.concatenate([c_prompt, c_sample, jnp.zeros((r_pad - n_c, d), F32)], axis=0)
    ada = _ada_call(c_all, w_ada, b_ada)
    ada_p = ada[:, :nb].reshape(depth, nb, 1, 6 * d)
    ada_s = jnp.repeat(ada[:, nb:n_c], st_len, axis=1)

    p = dict(g_norm1=g_norm1, g_norm2=g_norm2, attn_sinks=attn_sinks, w_dw=w_dw, b_dw=b_dw,
             g_conv_ln=g_conv_ln, b_conv_ln=b_conv_ln, w_pool=w_pool, s_pool=s_pool, g_ret_norm=g_ret_norm,
             gq_t=jnp.tile(g_qnorm, (1, SLAB // HEAD_DIM)).reshape(depth, 1, SLAB),
             gk_t=jnp.tile(g_knorm, (1, N_KV)).reshape(depth, 1, KV_W))

    tm_p = 1024 if t_len % 1024 == 0 else t_len
    tm_p_in = 2048 if t_len % 2048 == 0 else tm_p
    tm_s = ms
    n_mix = SLAB_GATE * SLAB
    cache_k2 = cache_attn_k.reshape(depth, nsb, WINDOW, KV_W)
    cache_v2 = cache_attn_v.reshape(depth, nsb, WINDOW, KV_W)
    ret_init = state_ret.reshape(depth, nsb, RET_HEADS // 2, 2 * RET_DK, RET_DV)

    xp = x_prompt.reshape(mp, d)
    xs = x_sample.reshape(ms, d)
    st_p = [[] for _ in range(5)]
    st_s = [[] for _ in range(5)]
    for l in range(depth):
        masters = dict(in_mix=_Weight(w_in, l, shape=(d, n_mix)),
                       in_gate=_Weight(w_in, l, shape=(d, N_IN - n_mix), col0=SLAB_GATE),
                       br=_Weight(w_br, l), out=_Weight(w_out, l), mlp1=_Weight(w_mlp1, l), mlp2=_Weight(w_mlp2, l))
        mod_s = _Mod(ada_s, l, True, st_len)
        cache = ((cache_k2, cache_v2), state_conv, state_pool, ret_init)
        xs, (z3, k_norm, conv_tail, s_new), copies = _layer(xs, mod_s, p, masters, l, nsb, st_len, tm_s, tm_s,
                                                            MLP_TF_MASTER, cache)
        st_s[0].append(k_norm.reshape(nsb, st_len, N_KV, HEAD_DIM))
        st_s[1].append(z3[SLAB_KV].reshape(nsb, st_len, SLAB)[:, :, KV_W:].reshape(nsb, st_len, N_KV, HEAD_DIM))
        st_s[2].append(conv_tail)
        pool_u = jnp.moveaxis(z3[SLAB_POOL:SLAB_POOL + 2].reshape(2, nsb, st_len, SLAB), 0, 2)
        st_s[3].append(pool_u.reshape(nsb, st_len, BR_W))
        st_s[4].append(s_new.reshape(nsb, RET_HEADS, RET_DK, RET_DV))

        mod_p = _Mod(ada_p, l, False, t_len)
        wcopy = {name: _Weight(arr) for name, arr in copies.items()}
        xp, (z3, k_norm, conv_tail, s_new), _ = _layer(xp, mod_p, p, wcopy, l, nb, t_len, tm_p_in, tm_p, MLP_TF, None)
        z4 = z3.reshape(SLAB_GATE, nb, t_len, SLAB)
        st_p[0].append(k_norm.reshape(nb, t_len, KV_W)[:, -WINDOW:].reshape(nb, WINDOW, N_KV, HEAD_DIM))
        st_p[1].append(z4[SLAB_KV, :, -WINDOW:, KV_W:].reshape(nb, WINDOW, N_KV, HEAD_DIM))
        st_p[2].append(conv_tail[:, CONV_HALO - (CONV_K - 1):])
        pool_u = z4[SLAB_POOL:SLAB_POOL + 2, :, -POOL_PAD:]
        st_p[3].append(jnp.moveaxis(pool_u, 0, 2).reshape(nb, POOL_PAD, BR_W))
        st_p[4].append(s_new.reshape(nb, RET_HEADS, RET_DK, RET_DV))

    def rolled(old, new_rows, keep):
        return jnp.concatenate([old, jnp.stack(new_rows)], axis=2)[:, :, -keep:]

    sample_states = (rolled(cache_attn_k, st_s[0], WINDOW), rolled(cache_attn_v, st_s[1], WINDOW),
                     jnp.stack(st_s[2])[:, :, CONV_HALO - (CONV_K - 1):], rolled(state_pool, st_s[3], POOL_PAD),
                     jnp.stack(st_s[4]))
    return (xp.reshape(nb, t_len, d), xs.reshape(nsb, st_len, d), *[jnp.stack(a) for a in st_p], *sample_states)
```
